```python
import math
import numpy as np
import jax
import jax.numpy as jnp
from jax import lax


D_MODEL = 1024
BATCH = 4
SEQ = 8192
DEPTH = 2

GRID_W = 64
CTX_LEN = 256
HEAD_DIM = 64
N_GROUP_HEADS = D_MODEL // (4 * HEAD_DIM)
N_HEADS_NA = N_GROUP_HEADS
N_BLOCKS_LRU = N_GROUP_HEADS
LRU_BLOCK = HEAD_DIM
N_HEADS_RET = N_GROUP_HEADS
N_HEADS_DIFF = N_GROUP_HEADS
W_NA = N_HEADS_NA * HEAD_DIM
W_LRU = N_BLOCKS_LRU * LRU_BLOCK
W_RET = N_HEADS_RET * HEAD_DIM
W_DIFF = N_HEADS_DIFF * HEAD_DIM
MIX_WIDTH = W_NA + W_LRU + W_RET + W_DIFF
DIFF_DIM = HEAD_DIM // 2
IN_SPLITS = (W_NA,) * 3 + (W_LRU,) * 2 + (W_RET,) * 4 + (W_DIFF,) * 3
IN_WIDTH = sum(IN_SPLITS)
SPLIT_POINTS = tuple(int(s) for s in np.cumsum(IN_SPLITS)[:-1])
NA_WIN_ROWS = 8
NA_WIN_COLS = 16
LRU_CONV = 4
LRU_C = 8.0
RET_CHUNK = 128
Q_BLOCK = 128
ROPE_BASE = 10000.0
EPS = 1e-6
D_FF = ((8 * D_MODEL + 3 * 256 - 1) // (3 * 256)) * 256
NA_SCALE = HEAD_DIM ** -0.5
RET_SCALE = HEAD_DIM ** -0.5

kernel_name = 'hybrid_grid_diffusion_block'


def rms_norm(x, g):
    xf = x.astype(jnp.float32)
    y = xf * lax.rsqrt(jnp.mean(xf * xf, axis=-1, keepdims=True) + EPS)
    return (y * g.astype(jnp.float32)).astype(x.dtype)


def head_rms(y):
    yf = y.astype(jnp.float32)
    return (yf * lax.rsqrt(jnp.mean(yf * yf, axis=-1, keepdims=True) + EPS)).astype(y.dtype)


def modulate(h, shift, scale):
    return h * (1.0 + scale) + shift


def to_heads(t, h):
    return t.reshape(t.shape[:2] + (h, -1))


def to_maps(t):
    return t.reshape(t.shape[:2] + (N_HEADS_DIFF, 2, DIFF_DIM))


def axial_rope(n, dim):
    t = jnp.arange(n)
    row = (t // GRID_W).astype(jnp.float32)
    col = (t % GRID_W).astype(jnp.float32)
    nf = dim // 4
    inv = ROPE_BASE ** (-jnp.arange(nf, dtype=jnp.float32) / nf)
    ang = jnp.concatenate([row[:, None] * inv, col[:, None] * inv], axis=-1)
    return jnp.cos(ang), jnp.sin(ang)


def apply_rope(x, cos, sin):
    shape = (1, x.shape[1]) + (1,) * (x.ndim - 3) + (cos.shape[-1],)
    cos = cos.reshape(shape).astype(x.dtype)
    sin = sin.reshape(shape).astype(x.dtype)
    half = x.shape[-1] // 2
    x1, x2 = x[..., :half], x[..., half:]
    return jnp.concatenate([x1 * cos - x2 * sin, x2 * cos + x1 * sin], axis=-1)


def softmax_attention(q, k, v):
    s = jnp.einsum('bqhd,bkhd->bhqk', q, k).astype(jnp.float32)
    p = jax.nn.softmax(s, axis=-1).astype(v.dtype)
    o = jnp.einsum('bhqk,bkhd->bqhd', p, v)
    return o.reshape(o.shape[:2] + (-1,))


def neighbourhood_attention(q, k, v, k_ctx, v_ctx, rpb, rows):
    b, n, h, d = q.shape
    wr = min(NA_WIN_ROWS, rows)
    qg = q.reshape(b, rows, GRID_W, h, d)
    kg = k.reshape(b, rows, GRID_W, h, d)
    vg = v.reshape(b, rows, GRID_W, h, d)
    col_start = np.clip(np.arange(GRID_W) - NA_WIN_COLS // 2, 0, GRID_W - NA_WIN_COLS)
    col_idx = col_start[:, None] + np.arange(NA_WIN_COLS)[None, :]
    dc = col_idx - np.arange(GRID_W)[:, None] + NA_WIN_COLS - 1
    n_win = wr * NA_WIN_COLS

    def one_row(r):
        r0 = jnp.clip(r - wr // 2, 0, rows - wr)
        kw = lax.dynamic_slice_in_dim(kg, r0, wr, axis=1)[:, :, col_idx]
        vw = lax.dynamic_slice_in_dim(vg, r0, wr, axis=1)[:, :, col_idx]
        qr = lax.dynamic_index_in_dim(qg, r, axis=1, keepdims=False)
        dr = r0 + jnp.arange(wr) - r + NA_WIN_ROWS - 1
        bias = jnp.transpose(rpb[:, dr][:, :, dc], (0, 2, 1, 3))
        s_win = (jnp.einsum('bqhd,brqjhd->bhqrj', qr, kw).astype(jnp.float32)
                 + bias[None].astype(jnp.float32))
        s_ctx = jnp.einsum('bqhd,bchd->bhqc', qr, k_ctx).astype(jnp.float32)
        s = jnp.concatenate([s_win.reshape(b, h, GRID_W, n_win), s_ctx], axis=-1)
        p = jax.nn.softmax(s, axis=-1).astype(v.dtype)
        p_win = p[..., :n_win].reshape(b, h, GRID_W, wr, NA_WIN_COLS)
        p_ctx = p[..., n_win:]
        return (jnp.einsum('bhqrj,brqjhd->bqhd', p_win, vw)
                + jnp.einsum('bhqc,bchd->bqhd', p_ctx, v_ctx))

    out = lax.map(one_row, jnp.arange(rows))
    return jnp.moveaxis(out, 0, 1).reshape(b, n, h * d)


def depthwise_conv(x, w, bias):
    k = w.shape[0]
    left = (k - 1) // 2
    y = lax.conv_general_dilated(x, w[:, None, :].astype(x.dtype), window_strides=(1,),
                                 padding=[(left, k - 1 - left)],
                                 dimension_numbers=('NWC', 'WIO', 'NWC'),
                                 feature_group_count=x.shape[-1])
    return y + bias


def rglru_coeffs(xb, w_g, b_g, lam):
    bsz, n = xb.shape[:2]
    xk = xb.reshape(bsz, n, N_BLOCKS_LRU, LRU_BLOCK)
    gates = jax.nn.sigmoid(jnp.einsum('bnkc,gkcd->gbnkd', xk, w_g) + b_g[:, None, None])
    r = gates[0].reshape(bsz, n, W_LRU).astype(jnp.float32)
    i = gates[1].reshape(bsz, n, W_LRU).astype(jnp.float32)
    log_a = -LRU_C * r * jax.nn.softplus(-lam.astype(jnp.float32))
    a = jnp.exp(log_a)
    u = jnp.sqrt(-jnp.expm1(2.0 * log_a)) * (i * xb.astype(jnp.float32))
    return a, u


def linear_scan(a, u, h0, reverse):
    def comb(left, right):
        al, ul = left
        ar, ur = right
        return al * ar, ar * ul + ur
    a_cum, u_cum = lax.associative_scan(comb, (a, u), axis=1, reverse=reverse)
    h = a_cum * h0[:, None, :] + u_cum
    final = h[:, 0] if reverse else h[:, -1]
    return h, final


def rglru_bidir(xb, w_g, b_g, lam, h0_fwd, h0_bwd):
    af, uf = rglru_coeffs(xb, w_g[0], b_g[0], lam[0])
    ab, ub = rglru_coeffs(xb, w_g[1], b_g[1], lam[1])
    hf, hf_last = linear_scan(af, uf, h0_fwd, reverse=False)
    hb, hb_last = linear_scan(ab, ub, h0_bwd, reverse=True)
    return (hf + hb).astype(xb.dtype), hf_last, hb_last


def retention_scan(q, k, v, log_gamma, s0, inclusive, reverse):
    dtype = v.dtype
    q, k, v = (t.astype(jnp.float32) for t in (q, k, v))
    if reverse:
        q, k, v = (jnp.flip(t, axis=1) for t in (q, k, v))
    b, n, h, d = q.shape
    cs = min(RET_CHUNK, n)
    nc = n // cs
    shift = 0.0 if inclusive else 1.0
    j = jnp.arange(cs, dtype=jnp.float32)
    rel = j[:, None] - j[None, :] - shift
    mask = rel >= 0
    dmat = jnp.where(mask[None], jnp.exp(log_gamma[:, None, None] * jnp.where(mask, rel, 0.0)[None]), 0.0)
    q_decay = jnp.exp(log_gamma[None, :] * (j[:, None] + 1.0 - shift))
    k_decay = jnp.exp(log_gamma[None, :] * (cs - 1.0 - j[:, None]))
    c_decay = jnp.exp(log_gamma * cs)

    def chunks(t):
        return jnp.moveaxis(t.reshape(b, nc, cs, h, d), 1, 0)

    def step(s, inp):
        qc, kc, vc = inp
        sc = jnp.einsum('bihd,bjhd->bhij', qc, kc) * dmat
        o = (jnp.einsum('bhij,bjhe->bihe', sc, vc)
             + jnp.einsum('bihd,bhde->bihe', qc * q_decay[:, :, None], s))
        s = s * c_decay[:, None, None] + jnp.einsum('bjhd,bjhe->bhde', kc * k_decay[:, :, None], vc)
        return s, o

    s_fin, o = lax.scan(step, s0, (chunks(q), chunks(k), chunks(v)))
    o = jnp.moveaxis(o, 0, 1).reshape(b, n, h, d)
    if reverse:
        o = jnp.flip(o, axis=1)
    return o.astype(dtype), s_fin


def diff_core(q, k, v, lam):
    s = jnp.einsum('bqhmd,bkhmd->bhmqk', q, k).astype(jnp.float32) * (DIFF_DIM ** -0.5)
    p = jax.nn.softmax(s, axis=-1)
    a = (p[:, :, 0] - lam * p[:, :, 1]).astype(v.dtype)
    return jnp.einsum('bhqk,bkhd->bqhd', a, v)


def diff_attention_latent(q, k, v, k_ctx, v_ctx, lam):
    b, n = q.shape[:2]
    keys = jnp.concatenate([k_ctx, k], axis=1)
    vals = jnp.concatenate([v_ctx, v], axis=1)
    qb = jnp.moveaxis(q.reshape((b, n // Q_BLOCK, Q_BLOCK) + q.shape[2:]), 1, 0)
    out = lax.map(lambda qi: diff_core(qi, keys, vals, lam), qb)
    return jnp.moveaxis(out, 0, 1).reshape((b, n) + v.shape[2:])


def diff_output(y, g, lam_init):
    y = head_rms(y) * g.astype(y.dtype) * (1.0 - lam_init)
    return y.reshape(y.shape[:2] + (-1,))


def swiglu(h, w1, w2):
    g, u = jnp.split(h @ w1, 2, axis=-1)
    return (jax.nn.silu(g) * u) @ w2


def setup_inputs(seed: int = 0) -> dict:
    key = jax.random.key(seed)
    ks = jax.random.split(key, 24)
    f32 = jnp.float32

    def nrm(k, shape, std):
        return jax.random.normal(k, shape, f32) * std

    gains = 1.0 + nrm(ks[6], (4, DEPTH, D_MODEL), 0.01)
    u = jax.random.uniform(ks[13], (DEPTH, 2, W_LRU), f32, 0.9, 0.999)
    gamma0 = 1.0 - 2.0 ** (-5.0 - jnp.arange(N_HEADS_RET, dtype=f32))
    return {
        'x': nrm(ks[0], (BATCH, SEQ, D_MODEL), 1.0),
        'c': nrm(ks[1], (BATCH, D_MODEL), 1.0),
        'ctx': nrm(ks[2], (BATCH, CTX_LEN, D_MODEL), 1.0),
        'c_ctx': nrm(ks[3], (D_MODEL,), 1.0),
        'w_mod': nrm(ks[4], (DEPTH, D_MODEL, 6 * D_MODEL), 0.5 * D_MODEL ** -0.5),
        'b_mod': nrm(ks[5], (DEPTH, 6 * D_MODEL), 0.01),
        'g_pre_mix': gains[0],
        'g_post_mix': gains[1],
        'g_pre_ffn': gains[2],
        'g_post_ffn': gains[3],
        'w_in': nrm(ks[7], (DEPTH, D_MODEL, IN_WIDTH), D_MODEL ** -0.5),
        'na_rpb': nrm(ks[8], (DEPTH, N_HEADS_NA, 2 * NA_WIN_ROWS - 1, 2 * NA_WIN_COLS - 1), 0.02),
        'lru_conv_w': nrm(ks[9], (DEPTH, LRU_CONV, W_LRU), LRU_CONV ** -0.5),
        'lru_conv_b': nrm(ks[10], (DEPTH, W_LRU), 0.01),
        'lru_gate_w': nrm(ks[11], (DEPTH, 2, 2, N_BLOCKS_LRU, LRU_BLOCK, LRU_BLOCK), LRU_BLOCK ** -0.5),
        'lru_gate_b': nrm(ks[12], (DEPTH, 2, 2, N_BLOCKS_LRU, LRU_BLOCK), 0.01),
        'lru_lambda': jnp.log(u) - jnp.log1p(-u),
        'ret_decay': (jnp.log(gamma0) - jnp.log1p(-gamma0)) + nrm(ks[14], (DEPTH, 2, N_HEADS_RET), 0.1),
        'diff_lambda': nrm(ks[15], (DEPTH, 4, DIFF_DIM), 0.1),
        'diff_subln': 1.0 + nrm(ks[16], (DEPTH, HEAD_DIM), 0.01),
        'w_out': nrm(ks[17], (DEPTH, MIX_WIDTH, D_MODEL), MIX_WIDTH ** -0.5),
        'w_ffn_in': nrm(ks[18], (DEPTH, D_MODEL, 2 * D_FF), D_MODEL ** -0.5),
        'w_ffn_out': nrm(ks[19], (DEPTH, D_FF, D_MODEL), D_FF ** -0.5),
    }


def reference(x, c, ctx, c_ctx, w_mod, b_mod, g_pre_mix, g_post_mix, g_pre_ffn, g_post_ffn,
              w_in, na_rpb, lru_conv_w, lru_conv_b, lru_gate_w, lru_gate_b, lru_lambda,
              ret_decay, diff_lambda, diff_subln, w_out, w_ffn_in, w_ffn_out):
    f32 = jnp.float32
    b, n_lat = x.shape[0], x.shape[1]
    rows = n_lat // GRID_W
    cos_r, sin_r = axial_rope(n_lat, HEAD_DIM)
    cos_d, sin_d = axial_rope(n_lat, DIFF_DIM)
    zeros_h = jnp.zeros((b, W_LRU), f32)
    zeros_s = jnp.zeros((b, N_HEADS_RET, HEAD_DIM, HEAD_DIM), f32)
    xl, xc = x, ctx
    for l in range(DEPTH):
        last = l == DEPTH - 1
        lam_init = 0.8 - 0.6 * math.exp(-0.3 * l)
        mod_l = jnp.split((jax.nn.silu(c) @ w_mod[l] + b_mod[l])[:, None, :], 6, axis=-1)
        mod_c = jnp.split(jax.nn.silu(c_ctx) @ w_mod[l] + b_mod[l], 6, axis=-1)

        hl = modulate(rms_norm(xl, g_pre_mix[l]), mod_l[0], mod_l[1])
        hc = modulate(rms_norm(xc, g_pre_mix[l]), mod_c[0], mod_c[1])
        pl = jnp.split(hl @ w_in[l], SPLIT_POINTS, axis=-1)
        pc = jnp.split(hc @ w_in[l], SPLIT_POINTS, axis=-1)

        qa_l, ka_l, va_l = (to_heads(t, N_HEADS_NA) for t in pl[0:3])
        qa_c, ka_c, va_c = (to_heads(t, N_HEADS_NA) for t in pc[0:3])
        ya_l = neighbourhood_attention(qa_l * NA_SCALE, ka_l, va_l, ka_c, va_c, na_rpb[l], rows)

        xb_c = depthwise_conv(pc[3], lru_conv_w[l], lru_conv_b[l])
        hb_c, hf_last, hb_last = rglru_bidir(xb_c, lru_gate_w[l], lru_gate_b[l], lru_lambda[l], zeros_h, zeros_h)
        xb_l = depthwise_conv(pl[3], lru_conv_w[l], lru_conv_b[l])
        hb_l, _, _ = rglru_bidir(xb_l, lru_gate_w[l], lru_gate_b[l], lru_lambda[l], hf_last, hb_last)
        yb_l = hb_l * jax.nn.gelu(pl[4])

        log_g = jax.nn.log_sigmoid(ret_decay[l].astype(f32))
        qr_c = to_heads(pc[5], N_HEADS_RET)
        kr_c = to_heads(pc[6], N_HEADS_RET) * RET_SCALE
        vr_c = to_heads(pc[7], N_HEADS_RET)
        of_c, s_fwd = retention_scan(qr_c, kr_c, vr_c, log_g[0], zeros_s, True, False)
        ob_c, s_bwd = retention_scan(qr_c, kr_c, vr_c, log_g[1], zeros_s, False, True)
        qr_l = apply_rope(to_heads(pl[5], N_HEADS_RET), cos_r, sin_r)
        kr_l = apply_rope(to_heads(pl[6], N_HEADS_RET) * RET_SCALE, cos_r, sin_r)
        vr_l = to_heads(pl[7], N_HEADS_RET)
        of_l, _ = retention_scan(qr_l, kr_l, vr_l, log_g[0], s_fwd, True, False)
        ob_l, _ = retention_scan(qr_l, kr_l, vr_l, log_g[1], s_bwd, False, True)
        yc_l = (head_rms(of_l + ob_l) * jax.nn.silu(to_heads(pl[8], N_HEADS_RET))).reshape(b, n_lat, W_RET)

        lq1, lk1, lq2, lk2 = diff_lambda[l].astype(f32)
        lam = jnp.exp(jnp.sum(lq1 * lk1)) - jnp.exp(jnp.sum(lq2 * lk2)) + lam_init
        qd_c, kd_c = to_maps(pc[9]), to_maps(pc[10])
        vd_c = to_heads(pc[11], N_HEADS_DIFF)
        qd_l = apply_rope(to_maps(pl[9]), cos_d, sin_d)
        kd_l = apply_rope(to_maps(pl[10]), cos_d, sin_d)
        vd_l = to_heads(pl[11], N_HEADS_DIFF)
        yd_l = diff_output(diff_attention_latent(qd_l, kd_l, vd_l, kd_c, vd_c, lam), diff_subln[l], lam_init)

        y_l = jnp.concatenate([ya_l, yb_l, yc_l, yd_l], axis=-1) @ w_out[l]
        xl_mid = xl + mod_l[2] * rms_norm(y_l, g_post_mix[l])

        h2 = modulate(rms_norm(xl_mid, g_pre_ffn[l]), mod_l[3], mod_l[4])
        xl_new = xl_mid + mod_l[5] * rms_norm(swiglu(h2, w_ffn_in[l], w_ffn_out[l]), g_post_ffn[l])

        if not last:
            ya_c = softmax_attention(qa_c * NA_SCALE, ka_c, va_c)
            yb_c = hb_c * jax.nn.gelu(pc[4])
            yc_c = (head_rms(of_c + ob_c) * jax.nn.silu(to_heads(pc[8], N_HEADS_RET))).reshape(b, -1, W_RET)
            yd_c = diff_output(diff_core(qd_c, kd_c, vd_c, lam), diff_subln[l], lam_init)
            y_c = jnp.concatenate([ya_c, yb_c, yc_c, yd_c], axis=-1) @ w_out[l]
            xc_mid = xc + mod_c[2] * rms_norm(y_c, g_post_mix[l])
            h2c = modulate(rms_norm(xc_mid, g_pre_ffn[l]), mod_c[3], mod_c[4])
            xc = xc_mid + mod_c[5] * rms_norm(swiglu(h2c, w_ffn_in[l], w_ffn_out[l]), g_post_ffn[l])
        xl = xl_new
    return xl
```

```python
import functools
import math

import numpy as np
import jax
import jax.numpy as jnp
from jax import lax
from jax.experimental import pallas as pl
from jax.experimental.pallas import tpu as pltpu

F32 = jnp.float32
BF16 = jnp.bfloat16

D_MODEL = 1024
GRID_W = 64
HEAD_DIM = 64
N_HEADS = 4
GW = N_HEADS * HEAD_DIM
N_PROJ = 12
D_FF = 2816
NA_WIN_ROWS = 8
NA_WIN_COLS = 16
LRU_C = 8.0
RET_CHUNK = 128
ROPE_BASE = 10000.0
EPS = 1e-6
NEG = -1e30

C_NA_Q, C_NA_K, C_NA_V, C_LRU_X, C_LRU_G, C_RET_Q, C_RET_K, C_RET_V, C_RET_G, C_DF_Q, C_DF_K, C_DF_V = range(12)
_COL_SCALE = {C_NA_Q: HEAD_DIM ** -0.5, C_RET_K: HEAD_DIM ** -0.5, C_DF_Q: (HEAD_DIM // 2) ** -0.5}

LRU_HALO = 16
NA_QROWS = 8
NA_KROWS = 16
V7X_VMEM_LIMIT = 56 * 1024 * 1024


def _cparams(*sem):
    return pltpu.CompilerParams(dimension_semantics=sem, vmem_limit_bytes=V7X_VMEM_LIMIT)


def _dot(a, b):
    return jnp.dot(a, b, preferred_element_type=F32)


def _dot_nt(a, b):
    return lax.dot_general(a, b, (((1,), (1,)), ((), ())), preferred_element_type=F32)


def _rms(x, g):
    return x * lax.rsqrt(jnp.mean(x * x, axis=-1, keepdims=True) + EPS) * g


def _sigmoid(x):
    return 1.0 / (1.0 + jnp.exp(-x))


def _head_mean_sq(y):
    y2 = y * y
    hi = y2.astype(BF16)
    lo = (y2 - hi.astype(F32)).astype(BF16)
    r = lax.broadcasted_iota(jnp.int32, (GW, GW), 0) // HEAD_DIM
    c = lax.broadcasted_iota(jnp.int32, (GW, GW), 1) // HEAD_DIM
    bd = jnp.where(r == c, 1.0, 0.0).astype(BF16)
    return (_dot(hi, bd) + _dot(lo, bd)) * (1.0 / HEAD_DIM)


def _mod_kernel(c_ref, w_ref, b_ref, o_ref):
    c = c_ref[...]
    s = c * _sigmoid(c)
    o_ref[0] = jnp.dot(s, w_ref[0], preferred_element_type=F32, precision=lax.Precision.HIGHEST) + b_ref[0]


def _mod(cc, w_mod, b_mod):
    depth = w_mod.shape[0]
    r = cc.shape[0]
    tn = 1536
    return pl.pallas_call(
        _mod_kernel,
        grid=(depth, 6 * D_MODEL // tn),
        in_specs=[pl.BlockSpec((r, D_MODEL), lambda l, j: (0, 0)),
                  pl.BlockSpec((1, D_MODEL, tn), lambda l, j: (l, 0, j)),
                  pl.BlockSpec((1, 1, tn), lambda l, j: (l, 0, j))],
        out_specs=pl.BlockSpec((1, r, tn), lambda l, j: (l, 0, j)),
        out_shape=jax.ShapeDtypeStruct((depth, r, 6 * D_MODEL), F32),
        compiler_params=_cparams("arbitrary", "arbitrary"),
        name="adaln_mod",
    )(cc, w_mod, b_mod.reshape(depth, 1, 6 * D_MODEL))


def _rope(p, cos, sin_signed, half):
    outs = []
    for c in range(GW // 128):
        xs = p[:, c * 128:(c + 1) * 128]
        lane = lax.broadcasted_iota(jnp.int32, xs.shape, 1)
        first = (lane % (2 * half)) < half
        partner = jnp.where(first, pltpu.roll(xs, 128 - half, 1), pltpu.roll(xs, half, 1))
        outs.append(xs * cos[:, c * 128:(c + 1) * 128] + partner * sin_signed[:, c * 128:(c + 1) * 128])
    return jnp.concatenate(outs, axis=1)


def _in_proj_kernel(*refs, rope):
    if rope:
        x_ref, g_ref, sh_ref, sc_ref, w_ref, tab_ref, o_ref, vt_ref = refs
    else:
        x_ref, g_ref, sh_ref, sc_ref, w_ref, o_ref, vt_ref = refs
    h = _rms(x_ref[...], g_ref[...]) * (1.0 + sc_ref[0]) + sh_ref[0]
    hb = h.astype(BF16)
    for j in range(N_PROJ):
        p = _dot(hb, w_ref[:, j * GW:(j + 1) * GW])
        if j in _COL_SCALE:
            p = p * _COL_SCALE[j]
        if rope and j in (C_RET_Q, C_RET_K):
            p = _rope(p, tab_ref[0], tab_ref[1], HEAD_DIM // 2)
        if rope and j in (C_DF_Q, C_DF_K):
            p = _rope(p, tab_ref[2], tab_ref[3], HEAD_DIM // 4)
        o_ref[:, j * GW:(j + 1) * GW] = p.astype(BF16)
        if j == C_DF_V:
            vt_ref[0] = p.T.astype(BF16)


def _in_proj(x2d, g, shift, scale, w, tabs, b, n, tm):
    nt = n // tm
    rope = tabs is not None
    in_specs = [pl.BlockSpec((tm, D_MODEL), lambda bi, i: (bi * nt + i, 0)),
                pl.BlockSpec((1, D_MODEL), lambda bi, i: (0, 0)),
                pl.BlockSpec((1, 1, D_MODEL), lambda bi, i: (bi, 0, 0)),
                pl.BlockSpec((1, 1, D_MODEL), lambda bi, i: (bi, 0, 0)),
                pl.BlockSpec((D_MODEL, N_PROJ * GW), lambda bi, i: (0, 0))]
    args = [x2d, g, shift, scale, w]
    if rope:
        in_specs.append(pl.BlockSpec((4, tm, GW), lambda bi, i: (0, i, 0)))
        args.append(tabs)
    return pl.pallas_call(
        functools.partial(_in_proj_kernel, rope=rope),
        grid=(b, nt),
        in_specs=in_specs,
        out_specs=[pl.BlockSpec((tm, N_PROJ * GW), lambda bi, i: (bi * nt + i, 0)),
                   pl.BlockSpec((1, GW, tm), lambda bi, i: (bi, 0, i))],
        out_shape=[jax.ShapeDtypeStruct((b * n, N_PROJ * GW), BF16), jax.ShapeDtypeStruct((b, GW, n), BF16)],
        compiler_params=_cparams("arbitrary", "arbitrary"),
        name="in_proj_rope" if rope else "in_proj",
    )(*args)


def _rope_tables(n):
    t = jnp.arange(n)
    row = (t // GRID_W).astype(F32)
    col = (t % GRID_W).astype(F32)

    def tab(dim):
        nf = dim // 4
        inv = ROPE_BASE ** (-jnp.arange(nf, dtype=F32) / nf)
        ang = jnp.concatenate([row[:, None] * inv, col[:, None] * inv], axis=-1)
        cos, sin = jnp.cos(ang), jnp.sin(ang)
        reps = GW // dim
        return (jnp.tile(jnp.concatenate([cos, cos], axis=-1), (1, reps)),
                jnp.tile(jnp.concatenate([-sin, sin], axis=-1), (1, reps)))

    cr, sr = tab(HEAD_DIM)
    cd, sd = tab(HEAD_DIM // 2)
    return jnp.stack([cr, sr, cd, sd])


def _lru_kernel(xf_ref, xfp_ref, xfn_ref, xb_ref, xbp_ref, xbn_ref, cw_ref, cb_ref, wg_ref, bg_ref, lam_ref,
                h0_ref, hf_ref, hb_ref, hfin_ref, a_scr, u_scr, hc_scr, *, tn, nt):
    i = pl.program_id(1)

    @pl.when(i == 0)
    def _():
        hc_scr[...] = h0_ref[0]

    cw = cw_ref[...]
    row = lax.broadcasted_iota(jnp.int32, (tn, GW), 0)

    def coeffs(x_ref, xp_ref, xn_ref, tile, d):
        xm = x_ref[...].astype(F32)
        prev = jnp.where(tile > 0, xp_ref[LRU_HALO - 1:LRU_HALO, :].astype(F32), 0.0)
        nxt = jnp.where(tile < nt - 1, xn_ref[0:2, :].astype(F32), 0.0)
        xm1 = jnp.where(row == 0, prev, pltpu.roll(xm, 1, 0))
        xp1 = jnp.where(row == tn - 1, nxt[0:1], pltpu.roll(xm, tn - 1, 0))
        xp2 = pltpu.roll(xm, tn - 2, 0)
        xp2 = jnp.where(row == tn - 2, nxt[0:1], xp2)
        xp2 = jnp.where(row == tn - 1, nxt[1:2], xp2)
        xb = cw[0:1] * xm1 + cw[1:2] * xm + cw[2:3] * xp1 + cw[3:4] * xp2 + cb_ref[...]
        gates = _sigmoid(_dot(xb.astype(BF16), wg_ref[d]) + bg_ref[d])
        r = gates[:, :GW]
        ig = gates[:, GW:]
        nl = -lam_ref[d]
        softplus = jnp.maximum(nl, 0.0) + jnp.log1p(jnp.exp(-jnp.abs(nl)))
        log_a = -LRU_C * r * softplus
        a_scr[d] = jnp.exp(log_a)
        th = jnp.tanh(log_a)
        u_scr[d] = jnp.sqrt(-2.0 * th / (1.0 - th)) * (ig * xb)

    coeffs(xf_ref, xfp_ref, xfn_ref, i, 0)
    coeffs(xb_ref, xbp_ref, xbn_ref, nt - 1 - i, 1)

    ng = tn // 8

    def body(g, carry):
        hf, hb = carry
        fb = pl.multiple_of(g * 8, 8)
        bb = pl.multiple_of((ng - 1 - g) * 8, 8)
        af = a_scr[0, pl.ds(fb, 8), :]
        uf = u_scr[0, pl.ds(fb, 8), :]
        ab = a_scr[1, pl.ds(bb, 8), :]
        ub = u_scr[1, pl.ds(bb, 8), :]
        frows = []
        brows = [None] * 8
        for j in range(8):
            hf = af[j:j + 1] * hf + uf[j:j + 1]
            frows.append(hf)
            jb = 7 - j
            hb = ab[jb:jb + 1] * hb + ub[jb:jb + 1]
            brows[jb] = hb
        hf_ref[pl.ds(fb, 8), :] = jnp.concatenate(frows, axis=0)
        hb_ref[pl.ds(bb, 8), :] = jnp.concatenate(brows, axis=0)
        return hf, hb

    hf, hb = lax.fori_loop(0, ng, body, (hc_scr[0:1, :], hc_scr[1:2, :]))
    hc_scr[0:1, :] = hf
    hc_scr[1:2, :] = hb

    @pl.when(i == nt - 1)
    def _():
        hfin_ref[0] = jnp.concatenate([hf, hb, jnp.zeros((6, GW), F32)], axis=0)


def _lru(proj, b, n, tn, cw, cb, wg, bg, lam, h0):
    nt = n // tn
    hb8 = tn // LRU_HALO

    def main(rev):
        return pl.BlockSpec((tn, GW), lambda bi, i: (bi * nt + (nt - 1 - i if rev else i), C_LRU_X))

    def prev(rev):
        def im(bi, i):
            t = nt - 1 - i if rev else i
            return (jnp.maximum((bi * nt + t) * hb8 - 1, 0), C_LRU_X)
        return pl.BlockSpec((LRU_HALO, GW), im)

    def nxt(rev):
        def im(bi, i):
            t = nt - 1 - i if rev else i
            return (jnp.minimum((bi * nt + t + 1) * hb8, b * nt * hb8 - 1), C_LRU_X)
        return pl.BlockSpec((LRU_HALO, GW), im)

    const2 = lambda bi, i: (0, 0)
    const3 = lambda bi, i: (0, 0, 0)
    return pl.pallas_call(
        functools.partial(_lru_kernel, tn=tn, nt=nt),
        grid=(b, nt),
        in_specs=[main(False), prev(False), nxt(False), main(True), prev(True), nxt(True),
                  pl.BlockSpec((4, GW), const2), pl.BlockSpec((1, GW), const2),
                  pl.BlockSpec((2, GW, 2 * GW), const3), pl.BlockSpec((2, 1, 2 * GW), const3),
                  pl.BlockSpec((2, 1, GW), const3),
                  pl.BlockSpec((1, 8, GW), lambda bi, i: (bi, 0, 0))],
        out_specs=[pl.BlockSpec((tn, GW), lambda bi, i: (bi * nt + i, 0)),
                   pl.BlockSpec((tn, GW), lambda bi, i: (bi * nt + nt - 1 - i, 0)),
                   pl.BlockSpec((1, 8, GW), lambda bi, i: (bi, 0, 0))],
        out_shape=[jax.ShapeDtypeStruct((b * n, GW), F32), jax.ShapeDtypeStruct((b * n, GW), F32),
                   jax.ShapeDtypeStruct((b, 8, GW), F32)],
        scratch_shapes=[pltpu.VMEM((2, tn, GW), F32), pltpu.VMEM((2, tn, GW), F32), pltpu.VMEM((8, GW), F32)],
        compiler_params=_cparams("arbitrary", "arbitrary"),
        name="rglru",
    )(proj, proj, proj, proj, proj, proj, cw, cb, wg, bg, lam, h0)


def _ret_kernel(qf_ref, kf_ref, vf_ref, qb_ref, kb_ref, vb_ref, lgl_ref, lgh_ref, s0_ref,
                of_ref, ob_ref, sfin_ref, s_scr, dm_scr, qd_scr, kd_scr, *, nc):
    c = pl.program_id(1)
    cs = RET_CHUNK

    @pl.when(c == 0)
    def _():
        s_scr[...] = s0_ref[0]
        ii = lax.broadcasted_iota(jnp.int32, (cs, cs), 0)
        jj = lax.broadcasted_iota(jnp.int32, (cs, cs), 1)
        t = lax.broadcasted_iota(jnp.int32, (cs, GW), 0).astype(F32)
        for d in range(2):
            rel = ii - jj if d == 0 else jj - ii - 1
            ok = rel >= 0
            relf = jnp.where(ok, rel, 0).astype(F32)
            for h in range(N_HEADS):
                lg = lgh_ref[d * N_HEADS + h:d * N_HEADS + h + 1, :]
                dm_scr[d, h] = jnp.where(ok, jnp.exp(lg * relf), 0.0)
            lgl = lgl_ref[d]
            if d == 0:
                qd_scr[d] = jnp.exp(lgl * (t + 1.0))
                kd_scr[d] = jnp.exp(lgl * (cs - 1.0 - t))
            else:
                qd_scr[d] = jnp.exp(lgl * (cs - 1.0 - t))
                kd_scr[d] = jnp.exp(lgl * t)

    lane = lax.broadcasted_iota(jnp.int32, (cs, GW), 1) // HEAD_DIM
    br = lax.broadcasted_iota(jnp.int32, (GW, GW), 0) // HEAD_DIM
    bc = lax.broadcasted_iota(jnp.int32, (GW, GW), 1) // HEAD_DIM

    def direction(d, q_ref, k_ref, v_ref, o_ref):
        k = k_ref[...]
        v = v_ref[...]
        q32 = q_ref[...].astype(F32)
        k32 = k.astype(F32)
        v32 = v.astype(F32)
        s = s_scr[d]
        parts = []
        vparts = []
        for h in range(N_HEADS):
            mk = lane == h
            qh = jnp.where(mk, q32, 0.0).astype(BF16)
            parts.append((_dot_nt(qh, k) * dm_scr[d, h]).astype(BF16))
            vparts.append(jnp.where(mk, v32, 0.0).astype(BF16))
        p = jnp.concatenate(parts, axis=1)
        vs = jnp.concatenate(vparts, axis=0)
        o_ref[...] = _dot(p, vs) + _dot((q32 * qd_scr[d]).astype(BF16), s.astype(BF16))
        kd = (k32 * kd_scr[d]).T.astype(BF16)
        cdec = jnp.exp(lgl_ref[d] * float(cs))
        s_scr[d] = s * cdec + jnp.where(br == bc, _dot(kd, v), 0.0)

    direction(0, qf_ref, kf_ref, vf_ref, of_ref)
    direction(1, qb_ref, kb_ref, vb_ref, ob_ref)

    @pl.when(c == nc - 1)
    def _():
        sfin_ref[0] = s_scr[...]


def _ret(proj, b, n, lgl, lgh, s0):
    nc = n // RET_CHUNK

    def blk(col, rev):
        return pl.BlockSpec((RET_CHUNK, GW), lambda bi, c: (bi * nc + (nc - 1 - c if rev else c), col))

    return pl.pallas_call(
        functools.partial(_ret_kernel, nc=nc),
        grid=(b, nc),
        in_specs=[blk(C_RET_Q, False), blk(C_RET_K, False), blk(C_RET_V, False),
                  blk(C_RET_Q, True), blk(C_RET_K, True), blk(C_RET_V, True),
                  pl.BlockSpec((2, 1, GW), lambda bi, c: (0, 0, 0)),
                  pl.BlockSpec((2 * N_HEADS, 128), lambda bi, c: (0, 0)),
                  pl.BlockSpec((1, 2, GW, GW), lambda bi, c: (bi, 0, 0, 0))],
        out_specs=[pl.BlockSpec((RET_CHUNK, GW), lambda bi, c: (bi * nc + c, 0)),
                   pl.BlockSpec((RET_CHUNK, GW), lambda bi, c: (bi * nc + nc - 1 - c, 0)),
                   pl.BlockSpec((1, 2, GW, GW), lambda bi, c: (bi, 0, 0, 0))],
        out_shape=[jax.ShapeDtypeStruct((b * n, GW), F32), jax.ShapeDtypeStruct((b * n, GW), F32),
                   jax.ShapeDtypeStruct((b, 2, GW, GW), F32)],
        scratch_shapes=[pltpu.VMEM((2, GW, GW), F32), pltpu.VMEM((2, N_HEADS, RET_CHUNK, RET_CHUNK), F32),
                        pltpu.VMEM((2, RET_CHUNK, GW), F32), pltpu.VMEM((2, RET_CHUNK, GW), F32)],
        compiler_params=_cparams("arbitrary", "arbitrary"),
        name="retention",
    )(proj, proj, proj, proj, proj, proj, lgl, lgh, s0)


def _na_kernel(q_ref, k_ref, v_ref, kc_ref, vc_ref, bias_ref, o_ref, *, rows):
    i = pl.program_id(1)
    tq = NA_QROWS * GRID_W
    nk = NA_KROWS * GRID_W
    ks = jnp.clip(i * NA_QROWS - NA_WIN_ROWS // 2, 0, rows - NA_KROWS)
    start = pl.multiple_of(ks * GRID_W, GRID_W)
    kw = k_ref[pl.ds(start, nk), :]
    vw = v_ref[pl.ds(start, nk), :]
    kc = kc_ref[...]
    vc = vc_ref[...]
    q32 = q_ref[...].astype(F32)
    lane = lax.broadcasted_iota(jnp.int32, (tq, GW), 1) // HEAD_DIM
    out = jnp.zeros((tq, GW), F32)
    for h in range(N_HEADS):
        mk = lane == h
        qh = jnp.where(mk, q32, 0.0).astype(BF16)
        sw = _dot_nt(qh, kw) + bias_ref[0, h]
        sc = _dot_nt(qh, kc)
        m = jnp.maximum(jnp.max(sw, axis=-1, keepdims=True), jnp.max(sc, axis=-1, keepdims=True))
        ew = jnp.exp(sw - m)
        ec = jnp.exp(sc - m)
        l = jnp.sum(ew, axis=-1, keepdims=True) + jnp.sum(ec, axis=-1, keepdims=True)
        o = _dot(ew.astype(BF16), vw) + _dot(ec.astype(BF16), vc)
        out = jnp.where(mk, o * (1.0 / l), out)
    o_ref[...] = out.astype(BF16)


def _na(proj_l, proj_c, bias, b, n, c):
    rows = n // GRID_W
    nrb = rows // NA_QROWS
    tq = NA_QROWS * GRID_W
    nk = NA_KROWS * GRID_W

    def bias_map(bi, i):
        return (jnp.where(i == 0, 0, jnp.where(i == nrb - 1, 2, 1)), 0, 0, 0)

    return pl.pallas_call(
        functools.partial(_na_kernel, rows=rows),
        grid=(b, nrb),
        in_specs=[pl.BlockSpec((tq, GW), lambda bi, i: (bi * nrb + i, C_NA_Q)),
                  pl.BlockSpec((n, GW), lambda bi, i: (bi, C_NA_K)),
                  pl.BlockSpec((n, GW), lambda bi, i: (bi, C_NA_V)),
                  pl.BlockSpec((c, GW), lambda bi, i: (bi, C_NA_K)),
                  pl.BlockSpec((c, GW), lambda bi, i: (bi, C_NA_V)),
                  pl.BlockSpec((1, N_HEADS, tq, nk), bias_map)],
        out_specs=pl.BlockSpec((tq, GW), lambda bi, i: (bi * nrb + i, 0)),
        out_shape=jax.ShapeDtypeStruct((b * n, GW), BF16),
        compiler_params=_cparams("arbitrary", "arbitrary"),
        name="na_attention",
    )(proj_l, proj_l, proj_l, proj_c, proj_c, bias)


def _na_bias_tables(rpb):
    i = np.arange(NA_QROWS)[:, None]
    j = np.arange(NA_KROWS)[None, :]
    half = NA_WIN_ROWS // 2
    r0 = np.maximum(i - half, 0)
    lo = half + np.minimum(i, half)
    dr = np.stack([j - i + NA_WIN_ROWS - 1, j - i + half - 1, j - i - 1])
    okr = np.stack([(j >= r0) & (j < r0 + NA_WIN_ROWS), (j - i >= 0) & (j - i < NA_WIN_ROWS),
                    (j >= lo) & (j < lo + NA_WIN_ROWS)])
    cq = np.arange(GRID_W)[:, None]
    ck = np.arange(GRID_W)[None, :]
    cstart = np.clip(cq - NA_WIN_COLS // 2, 0, GRID_W - NA_WIN_COLS)
    okc = (ck >= cstart) & (ck < cstart + NA_WIN_COLS)
    dc = ck - cq + NA_WIN_COLS - 1
    nr, ncol = 2 * NA_WIN_ROWS - 1, 2 * NA_WIN_COLS - 1
    ohr = (np.clip(dr, 0, nr - 1)[..., None] == np.arange(nr)) & okr[..., None]
    ohc = (np.clip(dc, 0, ncol - 1)[..., None] == np.arange(ncol)) & okc[..., None]
    bias = jnp.einsum('vijr,hrd,ckd->vhicjk', jnp.asarray(ohr, F32), rpb.astype(F32), jnp.asarray(ohc, F32),
                      precision=lax.Precision.HIGHEST)
    valid = okr[:, None, :, None, :, None] & okc[None, None, None, :, None, :]
    bias = jnp.where(jnp.asarray(valid), bias, NEG)
    return bias.reshape(3, N_HEADS, NA_QROWS * GRID_W, NA_KROWS * GRID_W)


def _ctx_attn_kernel(*refs, diff, out_scale):
    if diff:
        q_ref, k_ref, v_ref, lam_ref, g_ref, o_ref = refs
    else:
        q_ref, k_ref, v_ref, o_ref = refs
    k = k_ref[...]
    v = v_ref[...]
    q32 = q_ref[...].astype(F32)
    lane = lax.broadcasted_iota(jnp.int32, q32.shape, 1)

    def softmax_pv(mk):
        s = _dot_nt(jnp.where(mk, q32, 0.0).astype(BF16), k)
        e = jnp.exp(s - jnp.max(s, axis=-1, keepdims=True))
        return _dot(e.astype(BF16), v) * (1.0 / jnp.sum(e, axis=-1, keepdims=True))

    out = jnp.zeros(q32.shape, F32)
    for h in range(N_HEADS):
        if diff:
            o = (softmax_pv(lane // (HEAD_DIM // 2) == 2 * h)
                 - lam_ref[...] * softmax_pv(lane // (HEAD_DIM // 2) == 2 * h + 1))
        else:
            o = softmax_pv(lane // HEAD_DIM == h)
        out = jnp.where(lane // HEAD_DIM == h, o, out)
    if diff:
        out = out * lax.rsqrt(_head_mean_sq(out) + EPS) * g_ref[...] * out_scale
    o_ref[...] = out.astype(BF16)


def _ctx_attn(proj_c, b, c, qcol, kcol, vcol, lam=None, g=None, out_scale=1.0):
    diff = lam is not None
    in_specs = [pl.BlockSpec((c, GW), lambda bi: (bi, qcol)),
                pl.BlockSpec((c, GW), lambda bi: (bi, kcol)),
                pl.BlockSpec((c, GW), lambda bi: (bi, vcol))]
    args = [proj_c, proj_c, proj_c]
    if diff:
        in_specs += [pl.BlockSpec((1, GW), lambda bi: (0, 0)), pl.BlockSpec((1, GW), lambda bi: (0, 0))]
        args += [lam, g]
    return pl.pallas_call(
        functools.partial(_ctx_attn_kernel, diff=diff, out_scale=out_scale),
        grid=(b,),
        in_specs=in_specs,
        out_specs=pl.BlockSpec((c, GW), lambda bi: (bi, 0)),
        out_shape=jax.ShapeDtypeStruct((b * c, GW), BF16),
        compiler_params=_cparams("arbitrary"),
        name="ctx_diff_attention" if diff else "ctx_attention",
    )(*args)


N_MAPS = 2 * N_HEADS
MAP_DIM = HEAD_DIM // 2


def _diff_lat_kernel(q_ref, kc_ref, vtc_ref, kl_ref, vtl_ref, lam_ref, g_ref, o_ref,
                     qm_scr, m_scr, l_scr, acc_scr, *, tq, tk, n_lat, out_scale):
    qt = q_ref[...].astype(F32).T
    rowg = lax.broadcasted_iota(jnp.int32, (GW, tq), 0) // MAP_DIM
    for hm in range(N_MAPS):
        qm_scr[hm] = jnp.where(rowg == hm, qt, 0.0).astype(BF16)
    m_scr[...] = jnp.full(m_scr.shape, NEG, F32)
    l_scr[...] = jnp.zeros(l_scr.shape, F32)
    acc_scr[...] = jnp.zeros(acc_scr.shape, F32)

    def tile(kt, vt_of_head):
        for hm in range(N_MAPS):
            s = _dot(kt, qm_scr[hm])
            mo = m_scr[hm:hm + 1, :]
            mn = jnp.maximum(mo, jnp.max(s, axis=0, keepdims=True))
            alpha = jnp.exp(mo - mn)
            e = jnp.exp(s - mn)
            l_scr[hm:hm + 1, :] = alpha * l_scr[hm:hm + 1, :] + jnp.sum(e, axis=0, keepdims=True)
            m_scr[hm:hm + 1, :] = mn
            acc_scr[hm] = alpha * acc_scr[hm] + _dot(vt_of_head(hm // 2), e.astype(BF16))

    tile(kc_ref[...], lambda h: vtc_ref[0, h * HEAD_DIM:(h + 1) * HEAD_DIM, :])

    def body(t, carry):
        st = pl.multiple_of(t * tk, tk)
        tile(kl_ref[pl.ds(st, tk), :], lambda h: vtl_ref[0, h * HEAD_DIM:(h + 1) * HEAD_DIM, pl.ds(st, tk)])
        return carry

    lax.fori_loop(0, n_lat // tk, body, 0)

    lam = lam_ref[0:1, 0:1]
    outs = []
    for h in range(N_HEADS):
        o1 = acc_scr[2 * h] * (1.0 / l_scr[2 * h:2 * h + 1, :])
        o2 = acc_scr[2 * h + 1] * (1.0 / l_scr[2 * h + 1:2 * h + 2, :])
        outs.append(o1 - lam * o2)
    out = jnp.concatenate(outs, axis=0).T
    out = out * lax.rsqrt(_head_mean_sq(out) + EPS) * g_ref[...] * out_scale
    o_ref[...] = out.astype(BF16)


def _diff_lat(proj_l, proj_c, vt_l, vt_c, lam, g, b, n, c, tq, tk, out_scale):
    nt = n // tq
    return pl.pallas_call(
        functools.partial(_diff_lat_kernel, tq=tq, tk=tk, n_lat=n, out_scale=out_scale),
        grid=(b, nt),
        in_specs=[pl.BlockSpec((tq, GW), lambda bi, i: (bi * nt + i, C_DF_Q)),
                  pl.BlockSpec((c, GW), lambda bi, i: (bi, C_DF_K)),
                  pl.BlockSpec((1, GW, c), lambda bi, i: (bi, 0, 0)),
                  pl.BlockSpec((n, GW), lambda bi, i: (bi, C_DF_K)),
                  pl.BlockSpec((1, GW, n), lambda bi, i: (bi, 0, 0)),
                  pl.BlockSpec((1, GW), lambda bi, i: (0, 0)),
                  pl.BlockSpec((1, GW), lambda bi, i: (0, 0))],
        out_specs=pl.BlockSpec((tq, GW), lambda bi, i: (bi * nt + i, 0)),
        out_shape=jax.ShapeDtypeStruct((b * n, GW), BF16),
        scratch_shapes=[pltpu.VMEM((N_MAPS, GW, tq), BF16), pltpu.VMEM((N_MAPS, tq), F32),
                        pltpu.VMEM((N_MAPS, tq), F32), pltpu.VMEM((N_MAPS, HEAD_DIM, tq), F32)],
        compiler_params=_cparams("arbitrary", "arbitrary"),
        name="diff_attention_lat",
    )(proj_l, proj_c, vt_c, proj_l, vt_l, lam, g)


def _out_proj_kernel(x_ref, ya_ref, hf_ref, hb_ref, gl_ref, of_ref, ob_ref, gr_ref, yd_ref, w_ref, g_ref, gate_ref,
                     o_ref):
    gl = gl_ref[...].astype(F32)
    gelu = 0.5 * gl * (1.0 + jnp.tanh(math.sqrt(2.0 / math.pi) * (gl + 0.044715 * (gl * gl * gl))))
    yb = (hf_ref[...] + hb_ref[...]) * gelu
    r = of_ref[...] + ob_ref[...]
    gr = gr_ref[...].astype(F32)
    yc = r * lax.rsqrt(_head_mean_sq(r) + EPS) * (gr * _sigmoid(gr))
    y = (_dot(ya_ref[...], w_ref[0:GW, :]) + _dot(yb.astype(BF16), w_ref[GW:2 * GW, :])
         + _dot(yc.astype(BF16), w_ref[2 * GW:3 * GW, :]) + _dot(yd_ref[...], w_ref[3 * GW:4 * GW, :]))
    o_ref[...] = x_ref[...] + gate_ref[0] * _rms(y, g_ref[...])


def _out_proj(x2d, ya, hf, hb, of, ob, yd, proj, w, g, gate, b, n, tm):
    nt = n // tm
    tok = lambda col: pl.BlockSpec((tm, GW), lambda bi, i: (bi * nt + i, col))
    return pl.pallas_call(
        _out_proj_kernel,
        grid=(b, nt),
        in_specs=[pl.BlockSpec((tm, D_MODEL), lambda bi, i: (bi * nt + i, 0)),
                  tok(0), tok(0), tok(0), tok(C_LRU_G), tok(0), tok(0), tok(C_RET_G), tok(0),
                  pl.BlockSpec((D_MODEL, D_MODEL), lambda bi, i: (0, 0)),
                  pl.BlockSpec((1, D_MODEL), lambda bi, i: (0, 0)),
                  pl.BlockSpec((1, 1, D_MODEL), lambda bi, i: (bi, 0, 0))],
        out_specs=pl.BlockSpec((tm, D_MODEL), lambda bi, i: (bi * nt + i, 0)),
        out_shape=jax.ShapeDtypeStruct((b * n, D_MODEL), F32),
        compiler_params=_cparams("arbitrary", "arbitrary"),
        name="out_proj",
    )(x2d, ya, hf, hb, proj, of, ob, proj, yd, w, g, gate)


FFN_CHUNK = 256


def _ffn_kernel(x_ref, g1_ref, sh_ref, sc_ref, w1_ref, w2_ref, g2_ref, gate_ref, o_ref):
    x = x_ref[...]
    h = (_rms(x, g1_ref[...]) * (1.0 + sc_ref[0]) + sh_ref[0]).astype(BF16)
    acc = jnp.zeros(x.shape, F32)
    for c in range(D_FF // FFN_CHUNK):
        lo, hi = c * FFN_CHUNK, (c + 1) * FFN_CHUNK
        gt = _dot(h, w1_ref[:, lo:hi])
        up = _dot(h, w1_ref[:, D_FF + lo:D_FF + hi])
        act = (gt * _sigmoid(gt) * up).astype(BF16)
        acc = acc + _dot(act, w2_ref[lo:hi, :])
    o_ref[...] = x + gate_ref[0] * _rms(acc, g2_ref[...])


def _ffn(x2d, g1, shift, scale, w1, w2, g2, gate, b, n, tm):
    nt = n // tm
    vec = pl.BlockSpec((1, D_MODEL), lambda bi, i: (0, 0))
    per_b = pl.BlockSpec((1, 1, D_MODEL), lambda bi, i: (bi, 0, 0))
    return pl.pallas_call(
        _ffn_kernel,
        grid=(b, nt),
        in_specs=[pl.BlockSpec((tm, D_MODEL), lambda bi, i: (bi * nt + i, 0)), vec, per_b, per_b,
                  pl.BlockSpec((D_MODEL, 2 * D_FF), lambda bi, i: (0, 0)),
                  pl.BlockSpec((D_FF, D_MODEL), lambda bi, i: (0, 0)), vec, per_b],
        out_specs=pl.BlockSpec((tm, D_MODEL), lambda bi, i: (bi * nt + i, 0)),
        out_shape=jax.ShapeDtypeStruct((b * n, D_MODEL), F32),
        compiler_params=_cparams("arbitrary", "arbitrary"),
        name="ffn",
    )(x2d, g1, shift, scale, w1, w2, g2, gate)


def _blockdiag(w):
    nb, bs, _ = w.shape
    return jnp.einsum('kcd,kj->kcjd', w, jnp.eye(nb, dtype=w.dtype)).reshape(nb * bs, nb * bs)


def _lru_params(conv_w, conv_b, gate_w, gate_b, lam):
    wg = jnp.stack([jnp.concatenate([_blockdiag(gate_w[d, 0]), _blockdiag(gate_w[d, 1])], axis=1)
                    for d in range(2)]).astype(BF16)
    bg = jnp.stack([jnp.concatenate([gate_b[d, 0].reshape(1, GW), gate_b[d, 1].reshape(1, GW)], axis=1)
                    for d in range(2)])
    return conv_w, conv_b.reshape(1, GW), wg, bg, lam.reshape(2, 1, GW)


def kernel(x, c, ctx, c_ctx, w_mod, b_mod, g_pre_mix, g_post_mix, g_pre_ffn, g_post_ffn, w_in, na_rpb, lru_conv_w,
           lru_conv_b, lru_gate_w, lru_gate_b, lru_lambda, ret_decay, diff_lambda, diff_subln, w_out, w_ffn_in,
           w_ffn_out):
    b, n, _ = x.shape
    nc = ctx.shape[1]
    depth = w_mod.shape[0]
    assert n % 1024 == 0 and n // GRID_W >= NA_KROWS and nc % RET_CHUNK == 0

    r = -(-(b + 1) // 8) * 8
    cc = jnp.zeros((r, D_MODEL), F32).at[:b].set(c).at[b].set(c_ctx)
    mod = _mod(cc, w_mod, b_mod)
    tabs = _rope_tables(n)

    xl = x.reshape(b * n, D_MODEL)
    xc = ctx.reshape(b * nc, D_MODEL)
    tm = 512
    vec = lambda a: a.reshape(1, D_MODEL)
    for l in range(depth):
        last = l == depth - 1
        lam_init = 0.8 - 0.6 * math.exp(-0.3 * l)
        ml = [mod[l, :b, k * D_MODEL:(k + 1) * D_MODEL].reshape(b, 1, D_MODEL) for k in range(6)]
        mc = [jnp.broadcast_to(mod[l, b, k * D_MODEL:(k + 1) * D_MODEL], (b, 1, D_MODEL)) for k in range(6)]
        w_in_l = w_in[l].astype(BF16)
        w_out_l = w_out[l].astype(BF16)
        w1_l = w_ffn_in[l].astype(BF16)
        w2_l = w_ffn_out[l].astype(BF16)

        pc, vt_c = _in_proj(xc, vec(g_pre_mix[l]), mc[0], mc[1], w_in_l, None, b, nc, nc)
        pt, vt_l = _in_proj(xl, vec(g_pre_mix[l]), ml[0], ml[1], w_in_l, tabs, b, n, tm)

        lru_p = _lru_params(lru_conv_w[l], lru_conv_b[l], lru_gate_w[l], lru_gate_b[l], lru_lambda[l])
        hf_c, hb_c, hfin = _lru(pc, b, nc, nc, *lru_p, jnp.zeros((b, 8, GW), F32))
        hf_l, hb_l, _ = _lru(pt, b, n, 1024, *lru_p, hfin)

        log_g = jax.nn.log_sigmoid(ret_decay[l].astype(F32))
        lgl = jnp.repeat(log_g, HEAD_DIM, axis=-1).reshape(2, 1, GW)
        lgh = jnp.broadcast_to(log_g.reshape(2 * N_HEADS, 1), (2 * N_HEADS, 128))
        of_c, ob_c, sfin = _ret(pc, b, nc, lgl, lgh, jnp.zeros((b, 2, GW, GW), F32))
        of_l, ob_l, _ = _ret(pt, b, n, lgl, lgh, sfin)

        ya_l = _na(pt, pc, _na_bias_tables(na_rpb[l]), b, n, nc)

        lq1, lk1, lq2, lk2 = diff_lambda[l].astype(F32)
        lam = jnp.exp(jnp.sum(lq1 * lk1)) - jnp.exp(jnp.sum(lq2 * lk2)) + lam_init
        lam_v = jnp.broadcast_to(lam, (1, GW)).astype(F32)
        g_sub = jnp.tile(diff_subln[l].astype(F32), N_HEADS).reshape(1, GW)
        yd_l = _diff_lat(pt, pc, vt_l, vt_c, lam_v, g_sub, b, n, nc, 512, 512, 1.0 - lam_init)

        x_mid = _out_proj(xl, ya_l, hf_l, hb_l, of_l, ob_l, yd_l, pt, w_out_l, vec(g_post_mix[l]), ml[2], b, n, tm)
        xl_new = _ffn(x_mid, vec(g_pre_ffn[l]), ml[3], ml[4], w1_l, w2_l, vec(g_post_ffn[l]), ml[5], b, n, tm)

        if not last:
            ya_c = _ctx_attn(pc, b, nc, C_NA_Q, C_NA_K, C_NA_V)
            yd_c = _ctx_attn(pc, b, nc, C_DF_Q, C_DF_K, C_DF_V, lam_v, g_sub, 1.0 - lam_init)
            xc_mid = _out_proj(xc, ya_c, hf_c, hb_c, of_c, ob_c, yd_c, pc, w_out_l, vec(g_post_mix[l]), mc[2],
                               b, nc, nc)
            xc = _ffn(xc_mid, vec(g_pre_ffn[l]), mc[3], mc[4], w1_l, w2_l, vec(g_post_ffn[l]), mc[5], b, nc, nc)
        xl = xl_new
    return xl.reshape(b, n, D_MODEL)
```

```python
import functools
import math

import numpy as np
import jax
import jax.numpy as jnp
from jax import lax
from jax.experimental import pallas as pl
from jax.experimental.pallas import tpu as pltpu

F32 = jnp.float32
BF16 = jnp.bfloat16

D_MODEL = 1024
GRID_W = 64
HEAD_DIM = 64
N_HEADS = 4
GW = N_HEADS * HEAD_DIM
N_PROJ = 12
D_FF = 2816
NA_WIN_ROWS = 8
NA_WIN_COLS = 16
LRU_C = 8.0
RET_CHUNK = 128
ROPE_BASE = 10000.0
EPS = 1e-6
NEG = -1e30

C_NA_Q, C_NA_K, C_NA_V, C_LRU_X, C_LRU_G, C_RET_Q, C_RET_K, C_RET_V, C_RET_G, C_DF_Q, C_DF_K, C_DF_V = range(12)
_COL_SCALE = {C_NA_Q: HEAD_DIM ** -0.5, C_RET_K: HEAD_DIM ** -0.5,
              C_DF_Q: (HEAD_DIM // 2) ** -0.5 * math.log2(math.e)}

LRU_HALO = 16
NA_QROWS = 8
NA_KROWS = 16
V7X_VMEM_LIMIT = 56 * 1024 * 1024


def _cparams(*sem):
    return pltpu.CompilerParams(dimension_semantics=sem, vmem_limit_bytes=V7X_VMEM_LIMIT)


def _dot(a, b):
    return jnp.dot(a, b, preferred_element_type=F32)


def _dot_nt(a, b):
    return lax.dot_general(a, b, (((1,), (1,)), ((), ())), preferred_element_type=F32)


def _rms(x, g):
    return x * lax.rsqrt(jnp.mean(x * x, axis=-1, keepdims=True) + EPS) * g


def _sigmoid(x):
    return 1.0 / (1.0 + jnp.exp(-x))


def _head_mean_sq(y):
    y2 = y * y
    hi = y2.astype(BF16)
    lo = (y2 - hi.astype(F32)).astype(BF16)
    r = lax.broadcasted_iota(jnp.int32, (GW, GW), 0) // HEAD_DIM
    c = lax.broadcasted_iota(jnp.int32, (GW, GW), 1) // HEAD_DIM
    bd = jnp.where(r == c, 1.0, 0.0).astype(BF16)
    return (_dot(hi, bd) + _dot(lo, bd)) * (1.0 / HEAD_DIM)


def _mod_kernel(c_ref, w_ref, b_ref, o_ref):
    c = c_ref[...]
    s = c * _sigmoid(c)
    o_ref[0] = jnp.dot(s, w_ref[0], preferred_element_type=F32, precision=lax.Precision.HIGHEST) + b_ref[0]


def _mod(cc, w_mod, b_mod):
    depth = w_mod.shape[0]
    r = cc.shape[0]
    tn = 1536
    return pl.pallas_call(
        _mod_kernel,
        grid=(depth, 6 * D_MODEL // tn),
        in_specs=[pl.BlockSpec((r, D_MODEL), lambda l, j: (0, 0)),
                  pl.BlockSpec((1, D_MODEL, tn), lambda l, j: (l, 0, j)),
                  pl.BlockSpec((1, 1, tn), lambda l, j: (l, 0, j))],
        out_specs=pl.BlockSpec((1, r, tn), lambda l, j: (l, 0, j)),
        out_shape=jax.ShapeDtypeStruct((depth, r, 6 * D_MODEL), F32),
        compiler_params=_cparams("arbitrary", "arbitrary"),
        name="adaln_mod",
    )(cc, w_mod, b_mod.reshape(depth, 1, 6 * D_MODEL))


def _rope(p, cos, sin_signed, half):
    outs = []
    for c in range(GW // 128):
        xs = p[:, c * 128:(c + 1) * 128]
        lane = lax.broadcasted_iota(jnp.int32, xs.shape, 1)
        first = (lane % (2 * half)) < half
        partner = jnp.where(first, pltpu.roll(xs, 128 - half, 1), pltpu.roll(xs, half, 1))
        outs.append(xs * cos[:, c * 128:(c + 1) * 128] + partner * sin_signed[:, c * 128:(c + 1) * 128])
    return jnp.concatenate(outs, axis=1)


def _in_proj_kernel(*refs, rope):
    if rope:
        x_ref, g_ref, sh_ref, sc_ref, w_ref, tab_ref, o_ref, vt_ref = refs
    else:
        x_ref, g_ref, sh_ref, sc_ref, w_ref, o_ref, vt_ref = refs
    h = _rms(x_ref[...], g_ref[...]) * (1.0 + sc_ref[0]) + sh_ref[0]
    hb = h.astype(BF16)
    for j in range(N_PROJ):
        p = _dot(hb, w_ref[:, j * GW:(j + 1) * GW])
        if j in _COL_SCALE:
            p = p * _COL_SCALE[j]
        if rope and j in (C_RET_Q, C_RET_K):
            p = _rope(p, tab_ref[0], tab_ref[1], HEAD_DIM // 2)
        if rope and j in (C_DF_Q, C_DF_K):
            p = _rope(p, tab_ref[2], tab_ref[3], HEAD_DIM // 4)
        o_ref[:, j * GW:(j + 1) * GW] = p.astype(BF16)
        if j == C_DF_V:
            vt_ref[0] = p.T.astype(BF16)


def _in_proj(x2d, g, shift, scale, w, tabs, b, n, tm):
    nt = n // tm
    rope = tabs is not None
    in_specs = [pl.BlockSpec((tm, D_MODEL), lambda bi, i: (bi * nt + i, 0)),
                pl.BlockSpec((1, D_MODEL), lambda bi, i: (0, 0)),
                pl.BlockSpec((1, 1, D_MODEL), lambda bi, i: (bi, 0, 0)),
                pl.BlockSpec((1, 1, D_MODEL), lambda bi, i: (bi, 0, 0)),
                pl.BlockSpec((D_MODEL, N_PROJ * GW), lambda bi, i: (0, 0))]
    args = [x2d, g, shift, scale, w]
    if rope:
        in_specs.append(pl.BlockSpec((4, tm, GW), lambda bi, i: (0, i, 0)))
        args.append(tabs)
    return pl.pallas_call(
        functools.partial(_in_proj_kernel, rope=rope),
        grid=(b, nt),
        in_specs=in_specs,
        out_specs=[pl.BlockSpec((tm, N_PROJ * GW), lambda bi, i: (bi * nt + i, 0)),
                   pl.BlockSpec((1, GW, tm), lambda bi, i: (bi, 0, i))],
        out_shape=[jax.ShapeDtypeStruct((b * n, N_PROJ * GW), BF16), jax.ShapeDtypeStruct((b, GW, n), BF16)],
        compiler_params=_cparams("arbitrary", "arbitrary"),
        name="in_proj_rope" if rope else "in_proj",
    )(*args)


def _rope_tables(n):
    t = jnp.arange(n)
    row = (t // GRID_W).astype(F32)
    col = (t % GRID_W).astype(F32)

    def tab(dim):
        nf = dim // 4
        inv = ROPE_BASE ** (-jnp.arange(nf, dtype=F32) / nf)
        ang = jnp.concatenate([row[:, None] * inv, col[:, None] * inv], axis=-1)
        cos, sin = jnp.cos(ang), jnp.sin(ang)
        reps = GW // dim
        return (jnp.tile(jnp.concatenate([cos, cos], axis=-1), (1, reps)),
                jnp.tile(jnp.concatenate([-sin, sin], axis=-1), (1, reps)))

    cr, sr = tab(HEAD_DIM)
    cd, sd = tab(HEAD_DIM // 2)
    return jnp.stack([cr, sr, cd, sd])


def _lru_kernel(xf_ref, xfp_ref, xfn_ref, xb_ref, xbp_ref, xbn_ref, cw_ref, cb_ref, wg_ref, bg_ref, lam_ref,
                h0_ref, hf_ref, hb_ref, hfin_ref, a_scr, u_scr, hc_scr, *, tn, nt):
    i = pl.program_id(1)

    @pl.when(i == 0)
    def _():
        hc_scr[...] = h0_ref[0]

    cw = cw_ref[...]
    row = lax.broadcasted_iota(jnp.int32, (tn, GW), 0)

    def coeffs(x_ref, xp_ref, xn_ref, tile, d):
        xm = x_ref[...].astype(F32)
        prev = jnp.where(tile > 0, xp_ref[LRU_HALO - 1:LRU_HALO, :].astype(F32), 0.0)
        nxt = jnp.where(tile < nt - 1, xn_ref[0:2, :].astype(F32), 0.0)
        xm1 = jnp.where(row == 0, prev, pltpu.roll(xm, 1, 0))
        xp1 = jnp.where(row == tn - 1, nxt[0:1], pltpu.roll(xm, tn - 1, 0))
        xp2 = pltpu.roll(xm, tn - 2, 0)
        xp2 = jnp.where(row == tn - 2, nxt[0:1], xp2)
        xp2 = jnp.where(row == tn - 1, nxt[1:2], xp2)
        xb = cw[0:1] * xm1 + cw[1:2] * xm + cw[2:3] * xp1 + cw[3:4] * xp2 + cb_ref[...]
        gates = _sigmoid(_dot(xb.astype(BF16), wg_ref[d]) + bg_ref[d])
        r = gates[:, :GW]
        ig = gates[:, GW:]
        nl = -lam_ref[d]
        softplus = jnp.maximum(nl, 0.0) + jnp.log1p(jnp.exp(-jnp.abs(nl)))
        log_a = -LRU_C * r * softplus
        a_scr[d] = jnp.exp(log_a)
        th = jnp.tanh(log_a)
        u_scr[d] = jnp.sqrt(-2.0 * th / (1.0 - th)) * (ig * xb)

    coeffs(xf_ref, xfp_ref, xfn_ref, i, 0)
    coeffs(xb_ref, xbp_ref, xbn_ref, nt - 1 - i, 1)

    ng = tn // 8

    def body(g, carry):
        hf, hb = carry
        fb = pl.multiple_of(g * 8, 8)
        bb = pl.multiple_of((ng - 1 - g) * 8, 8)
        af = a_scr[0, pl.ds(fb, 8), :]
        uf = u_scr[0, pl.ds(fb, 8), :]
        ab = a_scr[1, pl.ds(bb, 8), :]
        ub = u_scr[1, pl.ds(bb, 8), :]
        frows = []
        brows = [None] * 8
        for j in range(8):
            hf = af[j:j + 1] * hf + uf[j:j + 1]
            frows.append(hf)
            jb = 7 - j
            hb = ab[jb:jb + 1] * hb + ub[jb:jb + 1]
            brows[jb] = hb
        hf_ref[pl.ds(fb, 8), :] = jnp.concatenate(frows, axis=0)
        hb_ref[pl.ds(bb, 8), :] = jnp.concatenate(brows, axis=0)
        return hf, hb

    hf, hb = lax.fori_loop(0, ng, body, (hc_scr[0:1, :], hc_scr[1:2, :]))
    hc_scr[0:1, :] = hf
    hc_scr[1:2, :] = hb

    @pl.when(i == nt - 1)
    def _():
        hfin_ref[0] = jnp.concatenate([hf, hb, jnp.zeros((6, GW), F32)], axis=0)


def _lru(proj, b, n, tn, cw, cb, wg, bg, lam, h0):
    nt = n // tn
    hb8 = tn // LRU_HALO

    def main(rev):
        return pl.BlockSpec((tn, GW), lambda bi, i: (bi * nt + (nt - 1 - i if rev else i), C_LRU_X))

    def prev(rev):
        def im(bi, i):
            t = nt - 1 - i if rev else i
            return (jnp.maximum((bi * nt + t) * hb8 - 1, 0), C_LRU_X)
        return pl.BlockSpec((LRU_HALO, GW), im)

    def nxt(rev):
        def im(bi, i):
            t = nt - 1 - i if rev else i
            return (jnp.minimum((bi * nt + t + 1) * hb8, b * nt * hb8 - 1), C_LRU_X)
        return pl.BlockSpec((LRU_HALO, GW), im)

    const2 = lambda bi, i: (0, 0)
    const3 = lambda bi, i: (0, 0, 0)
    return pl.pallas_call(
        functools.partial(_lru_kernel, tn=tn, nt=nt),
        grid=(b, nt),
        in_specs=[main(False), prev(False), nxt(False), main(True), prev(True), nxt(True),
                  pl.BlockSpec((4, GW), const2), pl.BlockSpec((1, GW), const2),
                  pl.BlockSpec((2, GW, 2 * GW), const3), pl.BlockSpec((2, 1, 2 * GW), const3),
                  pl.BlockSpec((2, 1, GW), const3),
                  pl.BlockSpec((1, 8, GW), lambda bi, i: (bi, 0, 0))],
        out_specs=[pl.BlockSpec((tn, GW), lambda bi, i: (bi * nt + i, 0)),
                   pl.BlockSpec((tn, GW), lambda bi, i: (bi * nt + nt - 1 - i, 0)),
                   pl.BlockSpec((1, 8, GW), lambda bi, i: (bi, 0, 0))],
        out_shape=[jax.ShapeDtypeStruct((b * n, GW), F32), jax.ShapeDtypeStruct((b * n, GW), F32),
                   jax.ShapeDtypeStruct((b, 8, GW), F32)],
        scratch_shapes=[pltpu.VMEM((2, tn, GW), F32), pltpu.VMEM((2, tn, GW), F32), pltpu.VMEM((8, GW), F32)],
        compiler_params=_cparams("arbitrary", "arbitrary"),
        name="rglru",
    )(proj, proj, proj, proj, proj, proj, cw, cb, wg, bg, lam, h0)


def _ret_kernel(qf_ref, kf_ref, vf_ref, qb_ref, kb_ref, vb_ref, lgl_ref, lgh_ref, s0_ref,
                of_ref, ob_ref, sfin_ref, s_scr, dm_scr, qd_scr, kd_scr, *, nc):
    c = pl.program_id(1)
    cs = RET_CHUNK

    @pl.when(c == 0)
    def _():
        s_scr[...] = s0_ref[0]
        ii = lax.broadcasted_iota(jnp.int32, (cs, cs), 0)
        jj = lax.broadcasted_iota(jnp.int32, (cs, cs), 1)
        t = lax.broadcasted_iota(jnp.int32, (cs, GW), 0).astype(F32)
        for d in range(2):
            rel = ii - jj if d == 0 else jj - ii - 1
            ok = rel >= 0
            relf = jnp.where(ok, rel, 0).astype(F32)
            for h in range(N_HEADS):
                lg = lgh_ref[d * N_HEADS + h:d * N_HEADS + h + 1, :]
                dm_scr[d, h] = jnp.where(ok, jnp.exp(lg * relf), 0.0)
            lgl = lgl_ref[d]
            if d == 0:
                qd_scr[d] = jnp.exp(lgl * (t + 1.0))
                kd_scr[d] = jnp.exp(lgl * (cs - 1.0 - t))
            else:
                qd_scr[d] = jnp.exp(lgl * (cs - 1.0 - t))
                kd_scr[d] = jnp.exp(lgl * t)

    lane = lax.broadcasted_iota(jnp.int32, (cs, GW), 1) // HEAD_DIM
    br = lax.broadcasted_iota(jnp.int32, (GW, GW), 0) // HEAD_DIM
    bc = lax.broadcasted_iota(jnp.int32, (GW, GW), 1) // HEAD_DIM

    def direction(d, q_ref, k_ref, v_ref, o_ref):
        k = k_ref[...]
        v = v_ref[...]
        q32 = q_ref[...].astype(F32)
        k32 = k.astype(F32)
        v32 = v.astype(F32)
        s = s_scr[d]
        parts = []
        vparts = []
        for h in range(N_HEADS):
            mk = lane == h
            qh = jnp.where(mk, q32, 0.0).astype(BF16)
            parts.append((_dot_nt(qh, k) * dm_scr[d, h]).astype(BF16))
            vparts.append(jnp.where(mk, v32, 0.0).astype(BF16))
        p = jnp.concatenate(parts, axis=1)
        vs = jnp.concatenate(vparts, axis=0)
        o_ref[...] = _dot(p, vs) + _dot((q32 * qd_scr[d]).astype(BF16), s.astype(BF16))
        kd = (k32 * kd_scr[d]).T.astype(BF16)
        cdec = jnp.exp(lgl_ref[d] * float(cs))
        s_scr[d] = s * cdec + jnp.where(br == bc, _dot(kd, v), 0.0)

    direction(0, qf_ref, kf_ref, vf_ref, of_ref)
    direction(1, qb_ref, kb_ref, vb_ref, ob_ref)

    @pl.when(c == nc - 1)
    def _():
        sfin_ref[0] = s_scr[...]


def _ret(proj, b, n, lgl, lgh, s0):
    nc = n // RET_CHUNK

    def blk(col, rev):
        return pl.BlockSpec((RET_CHUNK, GW), lambda bi, c: (bi * nc + (nc - 1 - c if rev else c), col))

    return pl.pallas_call(
        functools.partial(_ret_kernel, nc=nc),
        grid=(b, nc),
        in_specs=[blk(C_RET_Q, False), blk(C_RET_K, False), blk(C_RET_V, False),
                  blk(C_RET_Q, True), blk(C_RET_K, True), blk(C_RET_V, True),
                  pl.BlockSpec((2, 1, GW), lambda bi, c: (0, 0, 0)),
                  pl.BlockSpec((2 * N_HEADS, 128), lambda bi, c: (0, 0)),
                  pl.BlockSpec((1, 2, GW, GW), lambda bi, c: (bi, 0, 0, 0))],
        out_specs=[pl.BlockSpec((RET_CHUNK, GW), lambda bi, c: (bi * nc + c, 0)),
                   pl.BlockSpec((RET_CHUNK, GW), lambda bi, c: (bi * nc + nc - 1 - c, 0)),
                   pl.BlockSpec((1, 2, GW, GW), lambda bi, c: (bi, 0, 0, 0))],
        out_shape=[jax.ShapeDtypeStruct((b * n, GW), F32), jax.ShapeDtypeStruct((b * n, GW), F32),
                   jax.ShapeDtypeStruct((b, 2, GW, GW), F32)],
        scratch_shapes=[pltpu.VMEM((2, GW, GW), F32), pltpu.VMEM((2, N_HEADS, RET_CHUNK, RET_CHUNK), F32),
                        pltpu.VMEM((2, RET_CHUNK, GW), F32), pltpu.VMEM((2, RET_CHUNK, GW), F32)],
        compiler_params=_cparams("arbitrary", "arbitrary"),
        name="retention",
    )(proj, proj, proj, proj, proj, proj, lgl, lgh, s0)


def _na_kernel(q_ref, k_ref, v_ref, kc_ref, vc_ref, bias_ref, o_ref, *, rows):
    i = pl.program_id(1)
    tq = NA_QROWS * GRID_W
    nk = NA_KROWS * GRID_W
    ks = jnp.clip(i * NA_QROWS - NA_WIN_ROWS // 2, 0, rows - NA_KROWS)
    start = pl.multiple_of(ks * GRID_W, GRID_W)
    kw = k_ref[pl.ds(start, nk), :]
    vw = v_ref[pl.ds(start, nk), :]
    kc = kc_ref[...]
    vc = vc_ref[...]
    q32 = q_ref[...].astype(F32)
    lane = lax.broadcasted_iota(jnp.int32, (tq, GW), 1) // HEAD_DIM
    out = jnp.zeros((tq, GW), F32)
    for h in range(N_HEADS):
        mk = lane == h
        qh = jnp.where(mk, q32, 0.0).astype(BF16)
        sw = _dot_nt(qh, kw) + bias_ref[0, h]
        sc = _dot_nt(qh, kc)
        m = jnp.maximum(jnp.max(sw, axis=-1, keepdims=True), jnp.max(sc, axis=-1, keepdims=True))
        ew = jnp.exp(sw - m)
        ec = jnp.exp(sc - m)
        l = jnp.sum(ew, axis=-1, keepdims=True) + jnp.sum(ec, axis=-1, keepdims=True)
        o = _dot(ew.astype(BF16), vw) + _dot(ec.astype(BF16), vc)
        out = jnp.where(mk, o * (1.0 / l), out)
    o_ref[...] = out.astype(BF16)


def _na(proj_l, proj_c, bias, b, n, c):
    rows = n // GRID_W
    nrb = rows // NA_QROWS
    tq = NA_QROWS * GRID_W
    nk = NA_KROWS * GRID_W

    def bias_map(bi, i):
        return (jnp.where(i == 0, 0, jnp.where(i == nrb - 1, 2, 1)), 0, 0, 0)

    return pl.pallas_call(
        functools.partial(_na_kernel, rows=rows),
        grid=(b, nrb),
        in_specs=[pl.BlockSpec((tq, GW), lambda bi, i: (bi * nrb + i, C_NA_Q)),
                  pl.BlockSpec((n, GW), lambda bi, i: (bi, C_NA_K)),
                  pl.BlockSpec((n, GW), lambda bi, i: (bi, C_NA_V)),
                  pl.BlockSpec((c, GW), lambda bi, i: (bi, C_NA_K)),
                  pl.BlockSpec((c, GW), lambda bi, i: (bi, C_NA_V)),
                  pl.BlockSpec((1, N_HEADS, tq, nk), bias_map)],
        out_specs=pl.BlockSpec((tq, GW), lambda bi, i: (bi * nrb + i, 0)),
        out_shape=jax.ShapeDtypeStruct((b * n, GW), BF16),
        compiler_params=_cparams("arbitrary", "arbitrary"),
        name="na_attention",
    )(proj_l, proj_l, proj_l, proj_c, proj_c, bias)


def _na_bias_tables(rpb):
    i = np.arange(NA_QROWS)[:, None]
    j = np.arange(NA_KROWS)[None, :]
    half = NA_WIN_ROWS // 2
    r0 = np.maximum(i - half, 0)
    lo = half + np.minimum(i, half)
    dr = np.stack([j - i + NA_WIN_ROWS - 1, j - i + half - 1, j - i - 1])
    okr = np.stack([(j >= r0) & (j < r0 + NA_WIN_ROWS), (j - i >= 0) & (j - i < NA_WIN_ROWS),
                    (j >= lo) & (j < lo + NA_WIN_ROWS)])
    cq = np.arange(GRID_W)[:, None]
    ck = np.arange(GRID_W)[None, :]
    cstart = np.clip(cq - NA_WIN_COLS // 2, 0, GRID_W - NA_WIN_COLS)
    okc = (ck >= cstart) & (ck < cstart + NA_WIN_COLS)
    dc = ck - cq + NA_WIN_COLS - 1
    nr, ncol = 2 * NA_WIN_ROWS - 1, 2 * NA_WIN_COLS - 1
    ohr = (np.clip(dr, 0, nr - 1)[..., None] == np.arange(nr)) & okr[..., None]
    ohc = (np.clip(dc, 0, ncol - 1)[..., None] == np.arange(ncol)) & okc[..., None]
    bias = jnp.einsum('vijr,hrd,ckd->vhicjk', jnp.asarray(ohr, F32), rpb.astype(F32), jnp.asarray(ohc, F32),
                      precision=lax.Precision.HIGHEST)
    valid = okr[:, None, :, None, :, None] & okc[None, None, None, :, None, :]
    bias = jnp.where(jnp.asarray(valid), bias, NEG)
    return bias.reshape(3, N_HEADS, NA_QROWS * GRID_W, NA_KROWS * GRID_W)


def _ctx_attn_kernel(*refs, diff, out_scale):
    if diff:
        q_ref, k_ref, v_ref, lam_ref, g_ref, o_ref = refs
    else:
        q_ref, k_ref, v_ref, o_ref = refs
    k = k_ref[...]
    v = v_ref[...]
    q32 = q_ref[...].astype(F32)
    lane = lax.broadcasted_iota(jnp.int32, q32.shape, 1)

    def softmax_pv(mk):
        s = _dot_nt(jnp.where(mk, q32, 0.0).astype(BF16), k)
        z = s - jnp.max(s, axis=-1, keepdims=True)
        e = jnp.exp2(z) if diff else jnp.exp(z)
        return _dot(e.astype(BF16), v) * (1.0 / jnp.sum(e, axis=-1, keepdims=True))

    out = jnp.zeros(q32.shape, F32)
    for h in range(N_HEADS):
        if diff:
            o = (softmax_pv(lane // (HEAD_DIM // 2) == 2 * h)
                 - lam_ref[...] * softmax_pv(lane // (HEAD_DIM // 2) == 2 * h + 1))
        else:
            o = softmax_pv(lane // HEAD_DIM == h)
        out = jnp.where(lane // HEAD_DIM == h, o, out)
    if diff:
        out = out * lax.rsqrt(_head_mean_sq(out) + EPS) * g_ref[...] * out_scale
    o_ref[...] = out.astype(BF16)


def _ctx_attn(proj_c, b, c, qcol, kcol, vcol, lam=None, g=None, out_scale=1.0):
    diff = lam is not None
    in_specs = [pl.BlockSpec((c, GW), lambda bi: (bi, qcol)),
                pl.BlockSpec((c, GW), lambda bi: (bi, kcol)),
                pl.BlockSpec((c, GW), lambda bi: (bi, vcol))]
    args = [proj_c, proj_c, proj_c]
    if diff:
        in_specs += [pl.BlockSpec((1, GW), lambda bi: (0, 0)), pl.BlockSpec((1, GW), lambda bi: (0, 0))]
        args += [lam, g]
    return pl.pallas_call(
        functools.partial(_ctx_attn_kernel, diff=diff, out_scale=out_scale),
        grid=(b,),
        in_specs=in_specs,
        out_specs=pl.BlockSpec((c, GW), lambda bi: (bi, 0)),
        out_shape=jax.ShapeDtypeStruct((b * c, GW), BF16),
        compiler_params=_cparams("arbitrary"),
        name="ctx_diff_attention" if diff else "ctx_attention",
    )(*args)


N_MAPS = 2 * N_HEADS
MAP_DIM = HEAD_DIM // 2


def _diff_key_tile(nk):
    return next(t for t in (768, 512, 384, 256, 128) if nk % t == 0)


V_AUG = HEAD_DIM + 16


def _diff_lat_kernel(q_ref, k_ref, vt_ref, lam_ref, g_ref, o_ref,
                     qm_scr, s_scr, e_scr, mx_scr, m_scr, acc_scr, *, tq, tk, nk, out_scale):
    nt = nk // tk
    qt = q_ref[...].astype(F32).T
    rowg = lax.broadcasted_iota(jnp.int32, (GW, tq), 0) // MAP_DIM
    for u in range(N_MAPS):
        qm_scr[u] = jnp.where(rowg == u, qt, 0.0).astype(BF16)
    m_scr[...] = jnp.full(m_scr.shape, NEG, F32)
    acc_scr[...] = jnp.zeros(acc_scr.shape, F32)

    def scores(t, u, slot):
        st = pl.multiple_of(t * tk, tk)
        s = _dot(k_ref[0, pl.ds(st, tk), :], qm_scr[u])
        s_scr[slot] = s
        mx_scr[slot:slot + 1, :] = jnp.max(s, axis=0, keepdims=True)

    def accumulate(t, u, slot):
        st = pl.multiple_of(t * tk, tk)
        h = u // 2
        mo = m_scr[u:u + 1, :]
        mn = jnp.maximum(mo, mx_scr[slot:slot + 1, :])
        alpha = jnp.exp2(mo - mn)
        e_scr[slot] = jnp.exp2((s_scr[slot] - mn).astype(BF16))
        m_scr[u:u + 1, :] = mn
        acc_scr[u] = alpha * acc_scr[u] + _dot(vt_ref[0, h * V_AUG:(h + 1) * V_AUG, pl.ds(st, tk)], e_scr[slot])

    scores(0, 0, 0)

    def body(t, carry):
        for u in range(N_MAPS):
            slot = u % 2
            if u + 1 < N_MAPS:
                scores(t, u + 1, 1 - slot)
            else:
                scores(jnp.minimum(t + 1, nt - 1), 0, 1 - slot)
            accumulate(t, u, slot)
        return carry

    lax.fori_loop(0, nt, body, 0)

    lam = lam_ref[0:1, 0:1]
    outs = []
    for h in range(N_HEADS):
        a1 = acc_scr[2 * h]
        a2 = acc_scr[2 * h + 1]
        o1 = a1[:HEAD_DIM] * (1.0 / a1[HEAD_DIM:HEAD_DIM + 1])
        o2 = a2[:HEAD_DIM] * (1.0 / a2[HEAD_DIM:HEAD_DIM + 1])
        outs.append(o1 - lam * o2)
    out = jnp.concatenate(outs, axis=0).T
    out = out * lax.rsqrt(_head_mean_sq(out) + EPS) * g_ref[...] * out_scale
    o_ref[...] = out.astype(BF16)


def _diff_lat(proj_l, k_all, vt_aug, lam, g, b, n, tq, tk, out_scale):
    nt = n // tq
    nk = k_all.shape[1]
    assert nk % tk == 0 and N_MAPS % 2 == 0
    return pl.pallas_call(
        functools.partial(_diff_lat_kernel, tq=tq, tk=tk, nk=nk, out_scale=out_scale),
        grid=(b, nt),
        in_specs=[pl.BlockSpec((tq, GW), lambda bi, i: (bi * nt + i, C_DF_Q)),
                  pl.BlockSpec((1, nk, GW), lambda bi, i: (bi, 0, 0)),
                  pl.BlockSpec((1, N_HEADS * V_AUG, nk), lambda bi, i: (bi, 0, 0)),
                  pl.BlockSpec((1, GW), lambda bi, i: (0, 0)),
                  pl.BlockSpec((1, GW), lambda bi, i: (0, 0))],
        out_specs=pl.BlockSpec((tq, GW), lambda bi, i: (bi * nt + i, 0)),
        out_shape=jax.ShapeDtypeStruct((b * n, GW), BF16),
        scratch_shapes=[pltpu.VMEM((N_MAPS, GW, tq), BF16), pltpu.VMEM((2, tk, tq), F32),
                        pltpu.VMEM((2, tk, tq), BF16), pltpu.VMEM((8, tq), F32), pltpu.VMEM((N_MAPS, tq), F32),
                        pltpu.VMEM((N_MAPS, V_AUG, tq), F32)],
        compiler_params=_cparams("arbitrary", "arbitrary"),
        name="diff_attention_lat",
    )(proj_l, k_all, vt_aug, lam, g)


def _out_proj_kernel(x_ref, ya_ref, hf_ref, hb_ref, gl_ref, of_ref, ob_ref, gr_ref, yd_ref, w_ref, g_ref, gate_ref,
                     o_ref):
    gl = gl_ref[...].astype(F32)
    gelu = 0.5 * gl * (1.0 + jnp.tanh(math.sqrt(2.0 / math.pi) * (gl + 0.044715 * (gl * gl * gl))))
    yb = (hf_ref[...] + hb_ref[...]) * gelu
    r = of_ref[...] + ob_ref[...]
    gr = gr_ref[...].astype(F32)
    yc = r * lax.rsqrt(_head_mean_sq(r) + EPS) * (gr * _sigmoid(gr))
    y = (_dot(ya_ref[...], w_ref[0:GW, :]) + _dot(yb.astype(BF16), w_ref[GW:2 * GW, :])
         + _dot(yc.astype(BF16), w_ref[2 * GW:3 * GW, :]) + _dot(yd_ref[...], w_ref[3 * GW:4 * GW, :]))
    o_ref[...] = x_ref[...] + gate_ref[0] * _rms(y, g_ref[...])


def _out_proj(x2d, ya, hf, hb, of, ob, yd, proj, w, g, gate, b, n, tm):
    nt = n // tm
    tok = lambda col: pl.BlockSpec((tm, GW), lambda bi, i: (bi * nt + i, col))
    return pl.pallas_call(
        _out_proj_kernel,
        grid=(b, nt),
        in_specs=[pl.BlockSpec((tm, D_MODEL), lambda bi, i: (bi * nt + i, 0)),
                  tok(0), tok(0), tok(0), tok(C_LRU_G), tok(0), tok(0), tok(C_RET_G), tok(0),
                  pl.BlockSpec((D_MODEL, D_MODEL), lambda bi, i: (0, 0)),
                  pl.BlockSpec((1, D_MODEL), lambda bi, i: (0, 0)),
                  pl.BlockSpec((1, 1, D_MODEL), lambda bi, i: (bi, 0, 0))],
        out_specs=pl.BlockSpec((tm, D_MODEL), lambda bi, i: (bi * nt + i, 0)),
        out_shape=jax.ShapeDtypeStruct((b * n, D_MODEL), F32),
        compiler_params=_cparams("arbitrary", "arbitrary"),
        name="out_proj",
    )(x2d, ya, hf, hb, proj, of, ob, proj, yd, w, g, gate)


FFN_CHUNK = 256


def _ffn_kernel(x_ref, g1_ref, sh_ref, sc_ref, w1_ref, w2_ref, g2_ref, gate_ref, o_ref):
    x = x_ref[...]
    h = (_rms(x, g1_ref[...]) * (1.0 + sc_ref[0]) + sh_ref[0]).astype(BF16)
    acc = jnp.zeros(x.shape, F32)
    for c in range(D_FF // FFN_CHUNK):
        lo, hi = c * FFN_CHUNK, (c + 1) * FFN_CHUNK
        gt = _dot(h, w1_ref[:, lo:hi])
        up = _dot(h, w1_ref[:, D_FF + lo:D_FF + hi])
        act = (gt * _sigmoid(gt) * up).astype(BF16)
        acc = acc + _dot(act, w2_ref[lo:hi, :])
    o_ref[...] = x + gate_ref[0] * _rms(acc, g2_ref[...])


def _ffn(x2d, g1, shift, scale, w1, w2, g2, gate, b, n, tm):
    nt = n // tm
    vec = pl.BlockSpec((1, D_MODEL), lambda bi, i: (0, 0))
    per_b = pl.BlockSpec((1, 1, D_MODEL), lambda bi, i: (bi, 0, 0))
    return pl.pallas_call(
        _ffn_kernel,
        grid=(b, nt),
        in_specs=[pl.BlockSpec((tm, D_MODEL), lambda bi, i: (bi * nt + i, 0)), vec, per_b, per_b,
                  pl.BlockSpec((D_MODEL, 2 * D_FF), lambda bi, i: (0, 0)),
                  pl.BlockSpec((D_FF, D_MODEL), lambda bi, i: (0, 0)), vec, per_b],
        out_specs=pl.BlockSpec((tm, D_MODEL), lambda bi, i: (bi * nt + i, 0)),
        out_shape=jax.ShapeDtypeStruct((b * n, D_MODEL), F32),
        compiler_params=_cparams("arbitrary", "arbitrary"),
        name="ffn",
    )(x2d, g1, shift, scale, w1, w2, g2, gate)


def _blockdiag(w):
    nb, bs, _ = w.shape
    return jnp.einsum('kcd,kj->kcjd', w, jnp.eye(nb, dtype=w.dtype)).reshape(nb * bs, nb * bs)


def _lru_params(conv_w, conv_b, gate_w, gate_b, lam):
    wg = jnp.stack([jnp.concatenate([_blockdiag(gate_w[d, 0]), _blockdiag(gate_w[d, 1])], axis=1)
                    for d in range(2)]).astype(BF16)
    bg = jnp.stack([jnp.concatenate([gate_b[d, 0].reshape(1, GW), gate_b[d, 1].reshape(1, GW)], axis=1)
                    for d in range(2)])
    return conv_w, conv_b.reshape(1, GW), wg, bg, lam.reshape(2, 1, GW)


def kernel(x, c, ctx, c_ctx, w_mod, b_mod, g_pre_mix, g_post_mix, g_pre_ffn, g_post_ffn, w_in, na_rpb, lru_conv_w,
           lru_conv_b, lru_gate_w, lru_gate_b, lru_lambda, ret_decay, diff_lambda, diff_subln, w_out, w_ffn_in,
           w_ffn_out):
    b, n, _ = x.shape
    nc = ctx.shape[1]
    depth = w_mod.shape[0]
    assert n % 1024 == 0 and n // GRID_W >= NA_KROWS and nc % RET_CHUNK == 0

    r = -(-(b + 1) // 8) * 8
    cc = jnp.zeros((r, D_MODEL), F32).at[:b].set(c).at[b].set(c_ctx)
    mod = _mod(cc, w_mod, b_mod)
    tabs = _rope_tables(n)

    xl = x.reshape(b * n, D_MODEL)
    xc = ctx.reshape(b * nc, D_MODEL)
    tm = 512
    vec = lambda a: a.reshape(1, D_MODEL)
    for l in range(depth):
        last = l == depth - 1
        lam_init = 0.8 - 0.6 * math.exp(-0.3 * l)
        ml = [mod[l, :b, k * D_MODEL:(k + 1) * D_MODEL].reshape(b, 1, D_MODEL) for k in range(6)]
        mc = [jnp.broadcast_to(mod[l, b, k * D_MODEL:(k + 1) * D_MODEL], (b, 1, D_MODEL)) for k in range(6)]
        w_in_l = w_in[l].astype(BF16)
        w_out_l = w_out[l].astype(BF16)
        w1_l = w_ffn_in[l].astype(BF16)
        w2_l = w_ffn_out[l].astype(BF16)

        pc, vt_c = _in_proj(xc, vec(g_pre_mix[l]), mc[0], mc[1], w_in_l, None, b, nc, nc)
        pt, vt_l = _in_proj(xl, vec(g_pre_mix[l]), ml[0], ml[1], w_in_l, tabs, b, n, tm)

        lru_p = _lru_params(lru_conv_w[l], lru_conv_b[l], lru_gate_w[l], lru_gate_b[l], lru_lambda[l])
        hf_c, hb_c, hfin = _lru(pc, b, nc, nc, *lru_p, jnp.zeros((b, 8, GW), F32))
        hf_l, hb_l, _ = _lru(pt, b, n, 1024, *lru_p, hfin)

        log_g = jax.nn.log_sigmoid(ret_decay[l].astype(F32))
        lgl = jnp.repeat(log_g, HEAD_DIM, axis=-1).reshape(2, 1, GW)
        lgh = jnp.broadcast_to(log_g.reshape(2 * N_HEADS, 1), (2 * N_HEADS, 128))
        of_c, ob_c, sfin = _ret(pc, b, nc, lgl, lgh, jnp.zeros((b, 2, GW, GW), F32))
        of_l, ob_l, _ = _ret(pt, b, n, lgl, lgh, sfin)

        ya_l = _na(pt, pc, _na_bias_tables(na_rpb[l]), b, n, nc)

        lq1, lk1, lq2, lk2 = diff_lambda[l].astype(F32)
        lam = jnp.exp(jnp.sum(lq1 * lk1)) - jnp.exp(jnp.sum(lq2 * lk2)) + lam_init
        lam_v = jnp.broadcast_to(lam, (1, GW)).astype(F32)
        g_sub = jnp.tile(diff_subln[l].astype(F32), N_HEADS).reshape(1, GW)
        k_all = jnp.concatenate([pc[:, C_DF_K * GW:(C_DF_K + 1) * GW].reshape(b, nc, GW),
                                 pt[:, C_DF_K * GW:(C_DF_K + 1) * GW].reshape(b, n, GW)], axis=1)
        vt_all = jnp.concatenate([vt_c, vt_l], axis=2).reshape(b, N_HEADS, HEAD_DIM, nc + n)
        vt_aug = jnp.concatenate([vt_all, jnp.ones((b, N_HEADS, V_AUG - HEAD_DIM, nc + n), BF16)],
                                 axis=2).reshape(b, N_HEADS * V_AUG, nc + n)
        yd_l = _diff_lat(pt, k_all, vt_aug, lam_v, g_sub, b, n, 512, _diff_key_tile(nc + n), 1.0 - lam_init)

        x_mid = _out_proj(xl, ya_l, hf_l, hb_l, of_l, ob_l, yd_l, pt, w_out_l, vec(g_post_mix[l]), ml[2], b, n, tm)
        xl_new = _ffn(x_mid, vec(g_pre_ffn[l]), ml[3], ml[4], w1_l, w2_l, vec(g_post_ffn[l]), ml[5], b, n, tm)

        if not last:
            ya_c = _ctx_attn(pc, b, nc, C_NA_Q, C_NA_K, C_NA_V)
            yd_c = _ctx_attn(pc, b, nc, C_DF_Q, C_DF_K, C_DF_V, lam_v, g_sub, 1.0 - lam_init)
            xc_mid = _out_proj(xc, ya_c, hf_c, hb_c, of_c, ob_c, yd_c, pc, w_out_l, vec(g_post_mix[l]), mc[2],
                               b, nc, nc)
            xc = _ffn(xc_mid, vec(g_pre_ffn[l]), mc[3], mc[4], w1_l, w2_l, vec(g_post_ffn[l]), mc[5], b, nc, nc)
        xl = xl_new
    return xl.reshape(b, n, D_MODEL)
```

```python
import functools
import math

import numpy as np
import jax
import jax.numpy as jnp
from jax import lax
from jax.experimental import pallas as pl
from jax.experimental.pallas import tpu as pltpu

F32 = jnp.float32
BF16 = jnp.bfloat16

D_MODEL = 1024
GRID_W = 64
HEAD_DIM = 64
N_HEADS = 4
GW = N_HEADS * HEAD_DIM
N_PROJ = 12
D_FF = 2816
NA_WIN_ROWS = 8
NA_WIN_COLS = 16
LRU_C = 8.0
RET_CHUNK = 128
ROPE_BASE = 10000.0
EPS = 1e-6
NEG = -1e30

C_NA_Q, C_NA_K, C_NA_V, C_LRU_X, C_LRU_G, C_RET_Q, C_RET_K, C_RET_V, C_RET_G, C_DF_Q, C_DF_K, C_DF_V = range(12)
_COL_SCALE = {C_NA_Q: HEAD_DIM ** -0.5, C_RET_K: HEAD_DIM ** -0.5,
              C_DF_Q: (HEAD_DIM // 2) ** -0.5 * math.log2(math.e)}

LRU_HALO = 16
V7X_VMEM_LIMIT = 56 * 1024 * 1024


def _cparams(*sem):
    return pltpu.CompilerParams(dimension_semantics=sem, vmem_limit_bytes=V7X_VMEM_LIMIT)


def _dot(a, b):
    return jnp.dot(a, b, preferred_element_type=F32)


def _dot_nt(a, b):
    return lax.dot_general(a, b, (((1,), (1,)), ((), ())), preferred_element_type=F32)


def _rms(x, g):
    return x * lax.rsqrt(jnp.mean(x * x, axis=-1, keepdims=True) + EPS) * g


def _sigmoid(x):
    return 1.0 / (1.0 + jnp.exp(-x))


def _head_mean_sq(y):
    y2 = y * y
    hi = y2.astype(BF16)
    lo = (y2 - hi.astype(F32)).astype(BF16)
    r = lax.broadcasted_iota(jnp.int32, (GW, GW), 0) // HEAD_DIM
    c = lax.broadcasted_iota(jnp.int32, (GW, GW), 1) // HEAD_DIM
    bd = jnp.where(r == c, 1.0, 0.0).astype(BF16)
    return (_dot(hi, bd) + _dot(lo, bd)) * (1.0 / HEAD_DIM)


def _mod_kernel(c_ref, w_ref, b_ref, o_ref):
    c = c_ref[...]
    s = c * _sigmoid(c)
    o_ref[0] = jnp.dot(s, w_ref[0], preferred_element_type=F32, precision=lax.Precision.HIGHEST) + b_ref[0]


def _mod(cc, w_mod, b_mod):
    depth = w_mod.shape[0]
    r = cc.shape[0]
    tn = 1536
    return pl.pallas_call(
        _mod_kernel,
        grid=(depth, 6 * D_MODEL // tn),
        in_specs=[pl.BlockSpec((r, D_MODEL), lambda l, j: (0, 0)),
                  pl.BlockSpec((1, D_MODEL, tn), lambda l, j: (l, 0, j)),
                  pl.BlockSpec((1, 1, tn), lambda l, j: (l, 0, j))],
        out_specs=pl.BlockSpec((1, r, tn), lambda l, j: (l, 0, j)),
        out_shape=jax.ShapeDtypeStruct((depth, r, 6 * D_MODEL), F32),
        compiler_params=_cparams("arbitrary", "arbitrary"),
        name="adaln_mod",
    )(cc, w_mod, b_mod.reshape(depth, 1, 6 * D_MODEL))


def _rope(p, cos, sin_signed, half):
    outs = []
    for c in range(GW // 128):
        xs = p[:, c * 128:(c + 1) * 128]
        lane = lax.broadcasted_iota(jnp.int32, xs.shape, 1)
        first = (lane % (2 * half)) < half
        partner = jnp.where(first, pltpu.roll(xs, 128 - half, 1), pltpu.roll(xs, half, 1))
        outs.append(xs * cos[:, c * 128:(c + 1) * 128] + partner * sin_signed[:, c * 128:(c + 1) * 128])
    return jnp.concatenate(outs, axis=1)


def _in_proj_kernel(*refs, rope):
    if rope:
        x_ref, g_ref, sh_ref, sc_ref, w_ref, tab_ref, o_ref, vt_ref = refs
    else:
        x_ref, g_ref, sh_ref, sc_ref, w_ref, o_ref, vt_ref = refs
    h = _rms(x_ref[...], g_ref[...]) * (1.0 + sc_ref[0]) + sh_ref[0]
    hb = h.astype(BF16)
    for j in range(N_PROJ):
        p = _dot(hb, w_ref[:, j * GW:(j + 1) * GW])
        if j in _COL_SCALE:
            p = p * _COL_SCALE[j]
        if rope and j in (C_RET_Q, C_RET_K):
            p = _rope(p, tab_ref[0], tab_ref[1], HEAD_DIM // 2)
        if rope and j in (C_DF_Q, C_DF_K):
            p = _rope(p, tab_ref[2], tab_ref[3], HEAD_DIM // 4)
        o_ref[:, j * GW:(j + 1) * GW] = p.astype(BF16)
        if j == C_DF_V:
            vt_ref[0] = p.T.astype(BF16)


def _in_proj(x2d, g, shift, scale, w, tabs, b, n, tm):
    nt = n // tm
    rope = tabs is not None
    in_specs = [pl.BlockSpec((tm, D_MODEL), lambda bi, i: (bi * nt + i, 0)),
                pl.BlockSpec((1, D_MODEL), lambda bi, i: (0, 0)),
                pl.BlockSpec((1, 1, D_MODEL), lambda bi, i: (bi, 0, 0)),
                pl.BlockSpec((1, 1, D_MODEL), lambda bi, i: (bi, 0, 0)),
                pl.BlockSpec((D_MODEL, N_PROJ * GW), lambda bi, i: (0, 0))]
    args = [x2d, g, shift, scale, w]
    if rope:
        in_specs.append(pl.BlockSpec((4, tm, GW), lambda bi, i: (0, i, 0)))
        args.append(tabs)
    return pl.pallas_call(
        functools.partial(_in_proj_kernel, rope=rope),
        grid=(b, nt),
        in_specs=in_specs,
        out_specs=[pl.BlockSpec((tm, N_PROJ * GW), lambda bi, i: (bi * nt + i, 0)),
                   pl.BlockSpec((1, GW, tm), lambda bi, i: (bi, 0, i))],
        out_shape=[jax.ShapeDtypeStruct((b * n, N_PROJ * GW), BF16), jax.ShapeDtypeStruct((b, GW, n), BF16)],
        compiler_params=_cparams("arbitrary", "arbitrary"),
        name="in_proj_rope" if rope else "in_proj",
    )(*args)


def _rope_tables(n):
    t = jnp.arange(n)
    row = (t // GRID_W).astype(F32)
    col = (t % GRID_W).astype(F32)

    def tab(dim):
        nf = dim // 4
        inv = ROPE_BASE ** (-jnp.arange(nf, dtype=F32) / nf)
        ang = jnp.concatenate([row[:, None] * inv, col[:, None] * inv], axis=-1)
        cos, sin = jnp.cos(ang), jnp.sin(ang)
        reps = GW // dim
        return (jnp.tile(jnp.concatenate([cos, cos], axis=-1), (1, reps)),
                jnp.tile(jnp.concatenate([-sin, sin], axis=-1), (1, reps)))

    cr, sr = tab(HEAD_DIM)
    cd, sd = tab(HEAD_DIM // 2)
    return jnp.stack([cr, sr, cd, sd])


def _lru_kernel(xf_ref, xfp_ref, xfn_ref, xb_ref, xbp_ref, xbn_ref, cw_ref, cb_ref, wg_ref, bg_ref, lam_ref,
                h0_ref, hf_ref, hb_ref, hfin_ref, a_scr, u_scr, hc_scr, *, tn, nt):
    i = pl.program_id(1)

    @pl.when(i == 0)
    def _():
        hc_scr[...] = h0_ref[0]

    cw = cw_ref[...]
    row = lax.broadcasted_iota(jnp.int32, (tn, GW), 0)

    def coeffs(x_ref, xp_ref, xn_ref, tile, d):
        xm = x_ref[...].astype(F32)
        prev = jnp.where(tile > 0, xp_ref[LRU_HALO - 1:LRU_HALO, :].astype(F32), 0.0)
        nxt = jnp.where(tile < nt - 1, xn_ref[0:2, :].astype(F32), 0.0)
        xm1 = jnp.where(row == 0, prev, pltpu.roll(xm, 1, 0))
        xp1 = jnp.where(row == tn - 1, nxt[0:1], pltpu.roll(xm, tn - 1, 0))
        xp2 = pltpu.roll(xm, tn - 2, 0)
        xp2 = jnp.where(row == tn - 2, nxt[0:1], xp2)
        xp2 = jnp.where(row == tn - 1, nxt[1:2], xp2)
        xb = cw[0:1] * xm1 + cw[1:2] * xm + cw[2:3] * xp1 + cw[3:4] * xp2 + cb_ref[...]
        gates = _sigmoid(_dot(xb.astype(BF16), wg_ref[d]) + bg_ref[d])
        r = gates[:, :GW]
        ig = gates[:, GW:]
        nl = -lam_ref[d]
        softplus = jnp.maximum(nl, 0.0) + jnp.log1p(jnp.exp(-jnp.abs(nl)))
        log_a = -LRU_C * r * softplus
        a_scr[d] = jnp.exp(log_a)
        th = jnp.tanh(log_a)
        u_scr[d] = jnp.sqrt(-2.0 * th / (1.0 - th)) * (ig * xb)

    coeffs(xf_ref, xfp_ref, xfn_ref, i, 0)
    coeffs(xb_ref, xbp_ref, xbn_ref, nt - 1 - i, 1)

    gs = LRU_HALO
    ng = tn // gs

    def body(g, carry):
        hf, hb = carry
        fb = pl.multiple_of(g * gs, gs)
        bb = pl.multiple_of((ng - 1 - g) * gs, gs)
        af = a_scr[0, pl.ds(fb, gs), :]
        uf = u_scr[0, pl.ds(fb, gs), :]
        ab = a_scr[1, pl.ds(bb, gs), :]
        ub = u_scr[1, pl.ds(bb, gs), :]
        frows = []
        brows = [None] * gs
        for j in range(gs):
            hf = af[j:j + 1] * hf + uf[j:j + 1]
            frows.append(hf)
            jb = gs - 1 - j
            hb = ab[jb:jb + 1] * hb + ub[jb:jb + 1]
            brows[jb] = hb
        hf_ref[pl.ds(fb, gs), :] = jnp.concatenate(frows, axis=0).astype(BF16)
        hb_ref[pl.ds(bb, gs), :] = jnp.concatenate(brows, axis=0).astype(BF16)
        return hf, hb

    hf, hb = lax.fori_loop(0, ng, body, (hc_scr[0:1, :], hc_scr[1:2, :]))
    hc_scr[0:1, :] = hf
    hc_scr[1:2, :] = hb

    @pl.when(i == nt - 1)
    def _():
        hfin_ref[0] = jnp.concatenate([hf, hb, jnp.zeros((6, GW), F32)], axis=0)


def _lru(proj, b, n, tn, cw, cb, wg, bg, lam, h0):
    nt = n // tn
    hb8 = tn // LRU_HALO

    def main(rev):
        return pl.BlockSpec((tn, GW), lambda bi, i: (bi * nt + (nt - 1 - i if rev else i), C_LRU_X))

    def prev(rev):
        def im(bi, i):
            t = nt - 1 - i if rev else i
            return (jnp.maximum((bi * nt + t) * hb8 - 1, 0), C_LRU_X)
        return pl.BlockSpec((LRU_HALO, GW), im)

    def nxt(rev):
        def im(bi, i):
            t = nt - 1 - i if rev else i
            return (jnp.minimum((bi * nt + t + 1) * hb8, b * nt * hb8 - 1), C_LRU_X)
        return pl.BlockSpec((LRU_HALO, GW), im)

    const2 = lambda bi, i: (0, 0)
    const3 = lambda bi, i: (0, 0, 0)
    return pl.pallas_call(
        functools.partial(_lru_kernel, tn=tn, nt=nt),
        grid=(b, nt),
        in_specs=[main(False), prev(False), nxt(False), main(True), prev(True), nxt(True),
                  pl.BlockSpec((4, GW), const2), pl.BlockSpec((1, GW), const2),
                  pl.BlockSpec((2, GW, 2 * GW), const3), pl.BlockSpec((2, 1, 2 * GW), const3),
                  pl.BlockSpec((2, 1, GW), const3),
                  pl.BlockSpec((1, 8, GW), lambda bi, i: (bi, 0, 0))],
        out_specs=[pl.BlockSpec((tn, GW), lambda bi, i: (bi * nt + i, 0)),
                   pl.BlockSpec((tn, GW), lambda bi, i: (bi * nt + nt - 1 - i, 0)),
                   pl.BlockSpec((1, 8, GW), lambda bi, i: (bi, 0, 0))],
        out_shape=[jax.ShapeDtypeStruct((b * n, GW), BF16), jax.ShapeDtypeStruct((b * n, GW), BF16),
                   jax.ShapeDtypeStruct((b, 8, GW), F32)],
        scratch_shapes=[pltpu.VMEM((2, tn, GW), F32), pltpu.VMEM((2, tn, GW), F32), pltpu.VMEM((8, GW), F32)],
        compiler_params=_cparams("arbitrary", "arbitrary"),
        name="rglru",
    )(proj, proj, proj, proj, proj, proj, cw, cb, wg, bg, lam, h0)


def _ret_kernel(qf_ref, kf_ref, vf_ref, qb_ref, kb_ref, vb_ref, lgl_ref, lgh_ref, s0_ref,
                of_ref, ob_ref, sfin_ref, s_scr, dm_scr, qd_scr, kd_scr, *, nc, cps):
    c = pl.program_id(1)
    cs = RET_CHUNK

    @pl.when(c == 0)
    def _():
        s_scr[...] = s0_ref[0]
        ii = lax.broadcasted_iota(jnp.int32, (cs, cs), 0)
        jj = lax.broadcasted_iota(jnp.int32, (cs, cs), 1)
        t = lax.broadcasted_iota(jnp.int32, (cs, GW), 0).astype(F32)
        for d in range(2):
            rel = ii - jj if d == 0 else jj - ii - 1
            ok = rel >= 0
            relf = jnp.where(ok, rel, 0).astype(F32)
            for h in range(N_HEADS):
                lg = lgh_ref[d * N_HEADS + h:d * N_HEADS + h + 1, :]
                dm_scr[d, h] = jnp.where(ok, jnp.exp(lg * relf), 0.0)
            lgl = lgl_ref[d]
            if d == 0:
                qd_scr[d] = jnp.exp(lgl * (t + 1.0))
                kd_scr[d] = jnp.exp(lgl * (cs - 1.0 - t))
            else:
                qd_scr[d] = jnp.exp(lgl * (cs - 1.0 - t))
                kd_scr[d] = jnp.exp(lgl * t)

    lane = lax.broadcasted_iota(jnp.int32, (cs, GW), 1) // HEAD_DIM
    br = lax.broadcasted_iota(jnp.int32, (GW, GW), 0) // HEAD_DIM
    bc = lax.broadcasted_iota(jnp.int32, (GW, GW), 1) // HEAD_DIM

    def direction(d, q_ref, k_ref, v_ref, o_ref):
        s = s_scr[d]
        cdec = jnp.exp(lgl_ref[d] * float(cs))
        for step in range(cps):
            j = step if d == 0 else cps - 1 - step
            rs = slice(j * cs, (j + 1) * cs)
            k = k_ref[rs, :]
            v = v_ref[rs, :]
            q32 = q_ref[rs, :].astype(F32)
            v32 = v.astype(F32)
            parts = []
            vparts = []
            for h in range(N_HEADS):
                mk = lane == h
                qh = jnp.where(mk, q32, 0.0).astype(BF16)
                parts.append((_dot_nt(qh, k) * dm_scr[d, h]).astype(BF16))
                vparts.append(jnp.where(mk, v32, 0.0).astype(BF16))
            p = jnp.concatenate(parts, axis=1)
            vs = jnp.concatenate(vparts, axis=0)
            o = _dot(p, vs) + _dot((q32 * qd_scr[d]).astype(BF16), s.astype(BF16))
            o_ref[rs, :] = o.astype(BF16)
            kd = (k.astype(F32) * kd_scr[d]).T.astype(BF16)
            s = s * cdec + jnp.where(br == bc, _dot(kd, v), 0.0)
        s_scr[d] = s

    direction(0, qf_ref, kf_ref, vf_ref, of_ref)
    direction(1, qb_ref, kb_ref, vb_ref, ob_ref)

    @pl.when(c == nc - 1)
    def _():
        sfin_ref[0] = s_scr[...]


RET_CHUNKS_PER_STEP = 4


def _ret(proj, b, n, lgl, lgh, s0):
    cps = min(RET_CHUNKS_PER_STEP, n // RET_CHUNK)
    tb = cps * RET_CHUNK
    nc = n // tb

    def blk(col, rev):
        return pl.BlockSpec((tb, GW), lambda bi, c: (bi * nc + (nc - 1 - c if rev else c), col))

    return pl.pallas_call(
        functools.partial(_ret_kernel, nc=nc, cps=cps),
        grid=(b, nc),
        in_specs=[blk(C_RET_Q, False), blk(C_RET_K, False), blk(C_RET_V, False),
                  blk(C_RET_Q, True), blk(C_RET_K, True), blk(C_RET_V, True),
                  pl.BlockSpec((2, 1, GW), lambda bi, c: (0, 0, 0)),
                  pl.BlockSpec((2 * N_HEADS, 128), lambda bi, c: (0, 0)),
                  pl.BlockSpec((1, 2, GW, GW), lambda bi, c: (bi, 0, 0, 0))],
        out_specs=[pl.BlockSpec((tb, GW), lambda bi, c: (bi * nc + c, 0)),
                   pl.BlockSpec((tb, GW), lambda bi, c: (bi * nc + nc - 1 - c, 0)),
                   pl.BlockSpec((1, 2, GW, GW), lambda bi, c: (bi, 0, 0, 0))],
        out_shape=[jax.ShapeDtypeStruct((b * n, GW), BF16), jax.ShapeDtypeStruct((b * n, GW), BF16),
                   jax.ShapeDtypeStruct((b, 2, GW, GW), F32)],
        scratch_shapes=[pltpu.VMEM((2, GW, GW), F32), pltpu.VMEM((2, N_HEADS, RET_CHUNK, RET_CHUNK), F32),
                        pltpu.VMEM((2, RET_CHUNK, GW), F32), pltpu.VMEM((2, RET_CHUNK, GW), F32)],
        compiler_params=_cparams("arbitrary", "arbitrary"),
        name="retention",
    )(proj, proj, proj, proj, proj, proj, lgl, lgh, s0)


def _na_kernel(q_ref, k_ref, v_ref, kc_ref, vc_ref, bias_ref, o_ref, *, rows, rb):
    i0 = pl.program_id(1) * rb
    nwin = NA_WIN_ROWS * GRID_W
    kc = kc_ref[...]
    vc = vc_ref[...]
    lane = lax.broadcasted_iota(jnp.int32, (GRID_W, GW), 1) // HEAD_DIM

    def row(i, carry):
        r = i0 + i
        r0 = jnp.clip(r - NA_WIN_ROWS // 2, 0, rows - NA_WIN_ROWS)
        start = pl.multiple_of(r0 * GRID_W, GRID_W)
        qrow = pl.multiple_of(i * GRID_W, GRID_W)
        kw = k_ref[pl.ds(start, nwin), :]
        vw = v_ref[pl.ds(start, nwin), :]
        q32 = q_ref[pl.ds(qrow, GRID_W), :].astype(F32)
        qs = jnp.concatenate([jnp.where(lane == h, q32, 0.0) for h in range(N_HEADS)], axis=0).astype(BF16)
        sw = _dot_nt(qs, kw) + bias_ref[r0 - r + NA_WIN_ROWS - 1]
        sc = _dot_nt(qs, kc)
        m = jnp.maximum(jnp.max(sw, axis=-1, keepdims=True), jnp.max(sc, axis=-1, keepdims=True))
        ew = jnp.exp(sw - m)
        ec = jnp.exp(sc - m)
        l = jnp.sum(ew, axis=-1, keepdims=True) + jnp.sum(ec, axis=-1, keepdims=True)
        o = (_dot(ew.astype(BF16), vw) + _dot(ec.astype(BF16), vc)) * (1.0 / l)
        out = jnp.zeros((GRID_W, GW), F32)
        for h in range(N_HEADS):
            out = jnp.where(lane == h, o[h * GRID_W:(h + 1) * GRID_W], out)
        o_ref[pl.ds(qrow, GRID_W), :] = out.astype(BF16)
        return carry

    lax.fori_loop(0, rb, row, 0, unroll=NA_UNROLL)


NA_ROWS_PER_STEP = 16
NA_UNROLL = 8


def _na(proj_l, proj_c, bias, b, n, c):
    rows = n // GRID_W
    rb = NA_ROWS_PER_STEP
    nrb = rows // rb
    tq = rb * GRID_W
    return pl.pallas_call(
        functools.partial(_na_kernel, rows=rows, rb=rb),
        grid=(b, nrb),
        in_specs=[pl.BlockSpec((tq, GW), lambda bi, i: (bi * nrb + i, C_NA_Q)),
                  pl.BlockSpec((n, GW), lambda bi, i: (bi, C_NA_K)),
                  pl.BlockSpec((n, GW), lambda bi, i: (bi, C_NA_V)),
                  pl.BlockSpec((c, GW), lambda bi, i: (bi, C_NA_K)),
                  pl.BlockSpec((c, GW), lambda bi, i: (bi, C_NA_V)),
                  pl.BlockSpec((NA_WIN_ROWS, N_HEADS * GRID_W, NA_WIN_ROWS * GRID_W), lambda bi, i: (0, 0, 0))],
        out_specs=pl.BlockSpec((tq, GW), lambda bi, i: (bi * nrb + i, 0)),
        out_shape=jax.ShapeDtypeStruct((b * n, GW), BF16),
        compiler_params=_cparams("arbitrary", "arbitrary"),
        name="na_attention",
    )(proj_l, proj_l, proj_l, proj_c, proj_c, bias)


def _na_bias_tables(rpb):
    nr, ncol = 2 * NA_WIN_ROWS - 1, 2 * NA_WIN_COLS - 1
    cq = np.arange(GRID_W)[:, None]
    ck = np.arange(GRID_W)[None, :]
    cstart = np.clip(cq - NA_WIN_COLS // 2, 0, GRID_W - NA_WIN_COLS)
    okc = (ck >= cstart) & (ck < cstart + NA_WIN_COLS)
    dc = np.clip(ck - cq + NA_WIN_COLS - 1, 0, ncol - 1)
    ohc = (dc[..., None] == np.arange(ncol)) & okc[..., None]
    dr = np.arange(NA_WIN_ROWS)[:, None] + np.arange(NA_WIN_ROWS)[None, :]
    ohr = dr[..., None] == np.arange(nr)
    bias = jnp.einsum('vjr,hrd,ckd->vhcjk', jnp.asarray(ohr, F32), rpb.astype(F32), jnp.asarray(ohc, F32),
                      precision=lax.Precision.HIGHEST)
    bias = jnp.where(jnp.asarray(okc)[None, None, :, None, :], bias, NEG)
    return bias.reshape(NA_WIN_ROWS, N_HEADS * GRID_W, NA_WIN_ROWS * GRID_W)


def _ctx_attn_kernel(*refs, diff, out_scale):
    if diff:
        q_ref, k_ref, v_ref, lam_ref, g_ref, o_ref = refs
    else:
        q_ref, k_ref, v_ref, o_ref = refs
    k = k_ref[...]
    v = v_ref[...]
    q32 = q_ref[...].astype(F32)
    lane = lax.broadcasted_iota(jnp.int32, q32.shape, 1)

    def softmax_pv(mk):
        s = _dot_nt(jnp.where(mk, q32, 0.0).astype(BF16), k)
        z = s - jnp.max(s, axis=-1, keepdims=True)
        e = jnp.exp2(z) if diff else jnp.exp(z)
        return _dot(e.astype(BF16), v) * (1.0 / jnp.sum(e, axis=-1, keepdims=True))

    out = jnp.zeros(q32.shape, F32)
    for h in range(N_HEADS):
        if diff:
            o = (softmax_pv(lane // (HEAD_DIM // 2) == 2 * h)
                 - lam_ref[...] * softmax_pv(lane // (HEAD_DIM // 2) == 2 * h + 1))
        else:
            o = softmax_pv(lane // HEAD_DIM == h)
        out = jnp.where(lane // HEAD_DIM == h, o, out)
    if diff:
        out = out * lax.rsqrt(_head_mean_sq(out) + EPS) * g_ref[...] * out_scale
    o_ref[...] = out.astype(BF16)


def _ctx_attn(proj_c, b, c, qcol, kcol, vcol, lam=None, g=None, out_scale=1.0):
    diff = lam is not None
    in_specs = [pl.BlockSpec((c, GW), lambda bi: (bi, qcol)),
                pl.BlockSpec((c, GW), lambda bi: (bi, kcol)),
                pl.BlockSpec((c, GW), lambda bi: (bi, vcol))]
    args = [proj_c, proj_c, proj_c]
    if diff:
        in_specs += [pl.BlockSpec((1, GW), lambda bi: (0, 0)), pl.BlockSpec((1, GW), lambda bi: (0, 0))]
        args += [lam, g]
    return pl.pallas_call(
        functools.partial(_ctx_attn_kernel, diff=diff, out_scale=out_scale),
        grid=(b,),
        in_specs=in_specs,
        out_specs=pl.BlockSpec((c, GW), lambda bi: (bi, 0)),
        out_shape=jax.ShapeDtypeStruct((b * c, GW), BF16),
        compiler_params=_cparams("arbitrary"),
        name="ctx_diff_attention" if diff else "ctx_attention",
    )(*args)


N_MAPS = 2 * N_HEADS
MAP_DIM = HEAD_DIM // 2


def _diff_key_tile(nk):
    return next(t for t in (768, 512, 384, 256, 128) if nk % t == 0)


V_AUG = HEAD_DIM + 16


def _diff_lat_kernel(q_ref, k_ref, vt_ref, lam_ref, g_ref, o_ref,
                     qm_scr, s_scr, mx_scr, m_scr, acc_scr, *, tq, tk, nk, out_scale):
    nt = nk // tk
    qt = q_ref[...].astype(F32).T
    rowg = lax.broadcasted_iota(jnp.int32, (GW, tq), 0) // MAP_DIM
    for u in range(N_MAPS):
        qm_scr[u] = jnp.where(rowg == u, qt, 0.0).astype(BF16)
    m_scr[...] = jnp.full(m_scr.shape, NEG, F32)
    acc_scr[...] = jnp.zeros(acc_scr.shape, F32)

    def scores(t, u, slot):
        st = pl.multiple_of(t * tk, tk)
        s = _dot(k_ref[0, pl.ds(st, tk), :], qm_scr[u])
        s_scr[slot] = s
        mx_scr[slot:slot + 1, :] = jnp.max(s, axis=0, keepdims=True)

    def accumulate(t, u, slot):
        st = pl.multiple_of(t * tk, tk)
        h = u // 2
        mo = m_scr[u:u + 1, :]
        mn = jnp.maximum(mo, mx_scr[slot:slot + 1, :])
        alpha = jnp.exp2(mo - mn)
        e = jnp.exp2((s_scr[slot] - mn).astype(BF16))
        m_scr[u:u + 1, :] = mn
        acc_scr[u] = alpha * acc_scr[u] + _dot(vt_ref[0, h * V_AUG:(h + 1) * V_AUG, pl.ds(st, tk)], e)

    scores(0, 0, 0)

    def body(t, carry):
        for u in range(N_MAPS):
            slot = u % 2
            if u + 1 < N_MAPS:
                scores(t, u + 1, 1 - slot)
            else:
                scores(jnp.minimum(t + 1, nt - 1), 0, 1 - slot)
            accumulate(t, u, slot)
        return carry

    lax.fori_loop(0, nt, body, 0)

    lam = lam_ref[0:1, 0:1]
    outs = []
    for h in range(N_HEADS):
        a1 = acc_scr[2 * h]
        a2 = acc_scr[2 * h + 1]
        o1 = a1[:HEAD_DIM] * (1.0 / a1[HEAD_DIM:HEAD_DIM + 1])
        o2 = a2[:HEAD_DIM] * (1.0 / a2[HEAD_DIM:HEAD_DIM + 1])
        outs.append(o1 - lam * o2)
    out = jnp.concatenate(outs, axis=0).T
    out = out * lax.rsqrt(_head_mean_sq(out) + EPS) * g_ref[...] * out_scale
    o_ref[...] = out.astype(BF16)


def _diff_lat(proj_l, k_all, vt_aug, lam, g, b, n, tq, tk, out_scale):
    nt = n // tq
    nk = k_all.shape[1]
    assert nk % tk == 0 and N_MAPS % 2 == 0
    return pl.pallas_call(
        functools.partial(_diff_lat_kernel, tq=tq, tk=tk, nk=nk, out_scale=out_scale),
        grid=(b, nt),
        in_specs=[pl.BlockSpec((tq, GW), lambda bi, i: (bi * nt + i, C_DF_Q)),
                  pl.BlockSpec((1, nk, GW), lambda bi, i: (bi, 0, 0)),
                  pl.BlockSpec((1, N_HEADS * V_AUG, nk), lambda bi, i: (bi, 0, 0)),
                  pl.BlockSpec((1, GW), lambda bi, i: (0, 0)),
                  pl.BlockSpec((1, GW), lambda bi, i: (0, 0))],
        out_specs=pl.BlockSpec((tq, GW), lambda bi, i: (bi * nt + i, 0)),
        out_shape=jax.ShapeDtypeStruct((b * n, GW), BF16),
        scratch_shapes=[pltpu.VMEM((N_MAPS, GW, tq), BF16), pltpu.VMEM((2, tk, tq), F32),
                        pltpu.VMEM((2, tq), F32), pltpu.VMEM((N_MAPS, tq), F32),
                        pltpu.VMEM((N_MAPS, V_AUG, tq), F32)],
        compiler_params=_cparams("arbitrary", "arbitrary"),
        name="diff_attention_lat",
    )(proj_l, k_all, vt_aug, lam, g)


def _out_proj_kernel(x_ref, ya_ref, hf_ref, hb_ref, gl_ref, of_ref, ob_ref, gr_ref, yd_ref, w_ref, g_ref, gate_ref,
                     o_ref):
    gl = gl_ref[...].astype(F32)
    gelu = 0.5 * gl * (1.0 + jnp.tanh(math.sqrt(2.0 / math.pi) * (gl + 0.044715 * (gl * gl * gl))))
    yb = (hf_ref[...].astype(F32) + hb_ref[...].astype(F32)) * gelu
    r = of_ref[...].astype(F32) + ob_ref[...].astype(F32)
    gr = gr_ref[...].astype(F32)
    yc = r * lax.rsqrt(_head_mean_sq(r) + EPS) * (gr * _sigmoid(gr))
    y = (_dot(ya_ref[...], w_ref[0:GW, :]) + _dot(yb.astype(BF16), w_ref[GW:2 * GW, :])
         + _dot(yc.astype(BF16), w_ref[2 * GW:3 * GW, :]) + _dot(yd_ref[...], w_ref[3 * GW:4 * GW, :]))
    o_ref[...] = x_ref[...] + gate_ref[0] * _rms(y, g_ref[...])


def _out_proj(x2d, ya, hf, hb, of, ob, yd, proj, w, g, gate, b, n, tm):
    nt = n // tm
    tok = lambda col: pl.BlockSpec((tm, GW), lambda bi, i: (bi * nt + i, col))
    return pl.pallas_call(
        _out_proj_kernel,
        grid=(b, nt),
        in_specs=[pl.BlockSpec((tm, D_MODEL), lambda bi, i: (bi * nt + i, 0)),
                  tok(0), tok(0), tok(0), tok(C_LRU_G), tok(0), tok(0), tok(C_RET_G), tok(0),
                  pl.BlockSpec((D_MODEL, D_MODEL), lambda bi, i: (0, 0)),
                  pl.BlockSpec((1, D_MODEL), lambda bi, i: (0, 0)),
                  pl.BlockSpec((1, 1, D_MODEL), lambda bi, i: (bi, 0, 0))],
        out_specs=pl.BlockSpec((tm, D_MODEL), lambda bi, i: (bi * nt + i, 0)),
        out_shape=jax.ShapeDtypeStruct((b * n, D_MODEL), F32),
        compiler_params=_cparams("arbitrary", "arbitrary"),
        name="out_proj",
    )(x2d, ya, hf, hb, proj, of, ob, proj, yd, w, g, gate)


FFN_CHUNK = 256


def _ffn_kernel(x_ref, g1_ref, sh_ref, sc_ref, w1_ref, w2_ref, g2_ref, gate_ref, o_ref):
    x = x_ref[...]
    h = (_rms(x, g1_ref[...]) * (1.0 + sc_ref[0]) + sh_ref[0]).astype(BF16)
    acc = jnp.zeros(x.shape, F32)
    for c in range(D_FF // FFN_CHUNK):
        lo, hi = c * FFN_CHUNK, (c + 1) * FFN_CHUNK
        gt = _dot(h, w1_ref[:, lo:hi])
        up = _dot(h, w1_ref[:, D_FF + lo:D_FF + hi])
        act = (gt * _sigmoid(gt) * up).astype(BF16)
        acc = acc + _dot(act, w2_ref[lo:hi, :])
    o_ref[...] = x + gate_ref[0] * _rms(acc, g2_ref[...])


def _ffn(x2d, g1, shift, scale, w1, w2, g2, gate, b, n, tm):
    nt = n // tm
    vec = pl.BlockSpec((1, D_MODEL), lambda bi, i: (0, 0))
    per_b = pl.BlockSpec((1, 1, D_MODEL), lambda bi, i: (bi, 0, 0))
    return pl.pallas_call(
        _ffn_kernel,
        grid=(b, nt),
        in_specs=[pl.BlockSpec((tm, D_MODEL), lambda bi, i: (bi * nt + i, 0)), vec, per_b, per_b,
                  pl.BlockSpec((D_MODEL, 2 * D_FF), lambda bi, i: (0, 0)),
                  pl.BlockSpec((D_FF, D_MODEL), lambda bi, i: (0, 0)), vec, per_b],
        out_specs=pl.BlockSpec((tm, D_MODEL), lambda bi, i: (bi * nt + i, 0)),
        out_shape=jax.ShapeDtypeStruct((b * n, D_MODEL), F32),
        compiler_params=_cparams("arbitrary", "arbitrary"),
        name="ffn",
    )(x2d, g1, shift, scale, w1, w2, g2, gate)


def _blockdiag(w):
    nb, bs, _ = w.shape
    return jnp.einsum('kcd,kj->kcjd', w, jnp.eye(nb, dtype=w.dtype)).reshape(nb * bs, nb * bs)


def _lru_params(conv_w, conv_b, gate_w, gate_b, lam):
    wg = jnp.stack([jnp.concatenate([_blockdiag(gate_w[d, 0]), _blockdiag(gate_w[d, 1])], axis=1)
                    for d in range(2)]).astype(BF16)
    bg = jnp.stack([jnp.concatenate([gate_b[d, 0].reshape(1, GW), gate_b[d, 1].reshape(1, GW)], axis=1)
                    for d in range(2)])
    return conv_w, conv_b.reshape(1, GW), wg, bg, lam.reshape(2, 1, GW)


def kernel(x, c, ctx, c_ctx, w_mod, b_mod, g_pre_mix, g_post_mix, g_pre_ffn, g_post_ffn, w_in, na_rpb, lru_conv_w,
           lru_conv_b, lru_gate_w, lru_gate_b, lru_lambda, ret_decay, diff_lambda, diff_subln, w_out, w_ffn_in,
           w_ffn_out):
    b, n, _ = x.shape
    nc = ctx.shape[1]
    depth = w_mod.shape[0]
    assert n % 1024 == 0 and nc % RET_CHUNK == 0

    r = -(-(b + 1) // 8) * 8
    cc = jnp.zeros((r, D_MODEL), F32).at[:b].set(c).at[b].set(c_ctx)
    mod = _mod(cc, w_mod, b_mod)
    tabs = _rope_tables(n)

    xl = x.reshape(b * n, D_MODEL)
    xc = ctx.reshape(b * nc, D_MODEL)
    tm = 512
    vec = lambda a: a.reshape(1, D_MODEL)
    for l in range(depth):
        last = l == depth - 1
        lam_init = 0.8 - 0.6 * math.exp(-0.3 * l)
        ml = [mod[l, :b, k * D_MODEL:(k + 1) * D_MODEL].reshape(b, 1, D_MODEL) for k in range(6)]
        mc = [jnp.broadcast_to(mod[l, b, k * D_MODEL:(k + 1) * D_MODEL], (b, 1, D_MODEL)) for k in range(6)]
        w_in_l = w_in[l].astype(BF16)
        w_out_l = w_out[l].astype(BF16)
        w1_l = w_ffn_in[l].astype(BF16)
        w2_l = w_ffn_out[l].astype(BF16)

        pc, vt_c = _in_proj(xc, vec(g_pre_mix[l]), mc[0], mc[1], w_in_l, None, b, nc, nc)
        pt, vt_l = _in_proj(xl, vec(g_pre_mix[l]), ml[0], ml[1], w_in_l, tabs, b, n, tm)

        lru_p = _lru_params(lru_conv_w[l], lru_conv_b[l], lru_gate_w[l], lru_gate_b[l], lru_lambda[l])
        hf_c, hb_c, hfin = _lru(pc, b, nc, nc, *lru_p, jnp.zeros((b, 8, GW), F32))
        hf_l, hb_l, _ = _lru(pt, b, n, 1024, *lru_p, hfin)

        log_g = jax.nn.log_sigmoid(ret_decay[l].astype(F32))
        lgl = jnp.repeat(log_g, HEAD_DIM, axis=-1).reshape(2, 1, GW)
        lgh = jnp.broadcast_to(log_g.reshape(2 * N_HEADS, 1), (2 * N_HEADS, 128))
        of_c, ob_c, sfin = _ret(pc, b, nc, lgl, lgh, jnp.zeros((b, 2, GW, GW), F32))
        of_l, ob_l, _ = _ret(pt, b, n, lgl, lgh, sfin)

        ya_l = _na(pt, pc, _na_bias_tables(na_rpb[l]), b, n, nc)

        lq1, lk1, lq2, lk2 = diff_lambda[l].astype(F32)
        lam = jnp.exp(jnp.sum(lq1 * lk1)) - jnp.exp(jnp.sum(lq2 * lk2)) + lam_init
        lam_v = jnp.broadcast_to(lam, (1, GW)).astype(F32)
        g_sub = jnp.tile(diff_subln[l].astype(F32), N_HEADS).reshape(1, GW)
        k_all = jnp.concatenate([pc[:, C_DF_K * GW:(C_DF_K + 1) * GW].reshape(b, nc, GW),
                                 pt[:, C_DF_K * GW:(C_DF_K + 1) * GW].reshape(b, n, GW)], axis=1)
        vt_all = jnp.concatenate([vt_c, vt_l], axis=2).reshape(b, N_HEADS, HEAD_DIM, nc + n)
        vt_aug = jnp.concatenate([vt_all, jnp.ones((b, N_HEADS, V_AUG - HEAD_DIM, nc + n), BF16)],
                                 axis=2).reshape(b, N_HEADS * V_AUG, nc + n)
        yd_l = _diff_lat(pt, k_all, vt_aug, lam_v, g_sub, b, n, 512, _diff_key_tile(nc + n), 1.0 - lam_init)

        x_mid = _out_proj(xl, ya_l, hf_l, hb_l, of_l, ob_l, yd_l, pt, w_out_l, vec(g_post_mix[l]), ml[2], b, n, tm)
        xl_new = _ffn(x_mid, vec(g_pre_ffn[l]), ml[3], ml[4], w1_l, w2_l, vec(g_post_ffn[l]), ml[5], b, n, tm)

        if not last:
            ya_c = _ctx_attn(pc, b, nc, C_NA_Q, C_NA_K, C_NA_V)
            yd_c = _ctx_attn(pc, b, nc, C_DF_Q, C_DF_K, C_DF_V, lam_v, g_sub, 1.0 - lam_init)
            xc_mid = _out_proj(xc, ya_c, hf_c, hb_c, of_c, ob_c, yd_c, pc, w_out_l, vec(g_post_mix[l]), mc[2],
                               b, nc, nc)
            xc = _ffn(xc_mid, vec(g_pre_ffn[l]), mc[3], mc[4], w1_l, w2_l, vec(g_post_ffn[l]), mc[5], b, nc, nc)
        xl = xl_new
    return xl.reshape(b, n, D_MODEL)
```

```python
import functools
import math

import numpy as np
import jax
import jax.numpy as jnp
from jax import lax
from jax.experimental import pallas as pl
from jax.experimental.pallas import tpu as pltpu

F32 = jnp.float32
BF16 = jnp.bfloat16

D_MODEL = 1024
GRID_W = 64
HEAD_DIM = 64
N_HEADS = 4
GW = N_HEADS * HEAD_DIM
N_PROJ = 12
N_MAPS = 2 * N_HEADS
MAP_DIM = HEAD_DIM // 2
D_FF = 2816
NA_WIN_ROWS = 8
NA_WIN_COLS = 16
LRU_C = 8.0
RET_CHUNK = 128
ROPE_BASE = 10000.0
EPS = 1e-6
NEG = -1e30

C_NA_Q, C_NA_K, C_NA_V, C_LRU_X, C_LRU_G, C_RET_Q, C_RET_K, C_RET_V, C_RET_G, C_DF_Q, C_DF_K, C_DF_V = range(12)
_COL_SCALE = {C_NA_Q: HEAD_DIM ** -0.5, C_RET_K: HEAD_DIM ** -0.5,
              C_DF_Q: (HEAD_DIM // 2) ** -0.5 * math.log2(math.e)}

LRU_HALO = 16
V7X_VMEM_LIMIT = 56 * 1024 * 1024


def _cparams(*sem):
    return pltpu.CompilerParams(dimension_semantics=sem, vmem_limit_bytes=V7X_VMEM_LIMIT)


def _dot(a, b):
    return jnp.dot(a, b, preferred_element_type=F32)


def _dot_nt(a, b):
    return lax.dot_general(a, b, (((1,), (1,)), ((), ())), preferred_element_type=F32)


def _rms(x, g):
    return x * lax.rsqrt(jnp.mean(x * x, axis=-1, keepdims=True) + EPS) * g


def _sigmoid(x):
    return 1.0 / (1.0 + jnp.exp(-x))


def _group_sum_sq(y, group):
    y2 = y * y
    hi = y2.astype(BF16)
    lo = (y2 - hi.astype(F32)).astype(BF16)
    r = lax.broadcasted_iota(jnp.int32, (GW, GW), 0) // group
    c = lax.broadcasted_iota(jnp.int32, (GW, GW), 1) // group
    bd = jnp.where(r == c, 1.0, 0.0).astype(BF16)
    return _dot(hi, bd) + _dot(lo, bd)


def _head_mean_sq(y):
    return _group_sum_sq(y, HEAD_DIM) * (1.0 / HEAD_DIM)


def _mod_kernel(c_ref, w_ref, b_ref, o_ref):
    c = c_ref[...]
    s = c * _sigmoid(c)
    o_ref[0] = jnp.dot(s, w_ref[0], preferred_element_type=F32, precision=lax.Precision.HIGHEST) + b_ref[0]


def _mod(cc, w_mod, b_mod):
    depth = w_mod.shape[0]
    r = cc.shape[0]
    tn = 1536
    return pl.pallas_call(
        _mod_kernel,
        grid=(depth, 6 * D_MODEL // tn),
        in_specs=[pl.BlockSpec((r, D_MODEL), lambda l, j: (0, 0)),
                  pl.BlockSpec((1, D_MODEL, tn), lambda l, j: (l, 0, j)),
                  pl.BlockSpec((1, 1, tn), lambda l, j: (l, 0, j))],
        out_specs=pl.BlockSpec((1, r, tn), lambda l, j: (l, 0, j)),
        out_shape=jax.ShapeDtypeStruct((depth, r, 6 * D_MODEL), F32),
        compiler_params=_cparams("arbitrary", "arbitrary"),
        name="adaln_mod",
    )(cc, w_mod, b_mod.reshape(depth, 1, 6 * D_MODEL))


def _rope(p, cos, sin_signed, half):
    outs = []
    for c in range(GW // 128):
        xs = p[:, c * 128:(c + 1) * 128]
        lane = lax.broadcasted_iota(jnp.int32, xs.shape, 1)
        first = (lane % (2 * half)) < half
        partner = jnp.where(first, pltpu.roll(xs, 128 - half, 1), pltpu.roll(xs, half, 1))
        outs.append(xs * cos[:, c * 128:(c + 1) * 128] + partner * sin_signed[:, c * 128:(c + 1) * 128])
    return jnp.concatenate(outs, axis=1)


def _in_proj_kernel(*refs, rope):
    if rope:
        x_ref, g_ref, sh_ref, sc_ref, w_ref, tab_ref, o_ref, vt_ref, kn_ref = refs
    else:
        x_ref, g_ref, sh_ref, sc_ref, w_ref, o_ref, vt_ref, kn_ref = refs

    @pl.when(pl.program_id(1) == 0)
    def _():
        kn_ref[...] = jnp.zeros(kn_ref.shape, F32)

    h = _rms(x_ref[...], g_ref[...]) * (1.0 + sc_ref[0]) + sh_ref[0]
    hb = h.astype(BF16)
    for j in range(N_PROJ):
        p = _dot(hb, w_ref[:, j * GW:(j + 1) * GW])
        if j in _COL_SCALE:
            p = p * _COL_SCALE[j]
        if rope and j in (C_RET_Q, C_RET_K):
            p = _rope(p, tab_ref[0], tab_ref[1], HEAD_DIM // 2)
        if rope and j in (C_DF_Q, C_DF_K):
            p = _rope(p, tab_ref[2], tab_ref[3], HEAD_DIM // 4)
        pb = p.astype(BF16)
        o_ref[:, j * GW:(j + 1) * GW] = pb
        if j == C_DF_K:
            kn2 = jnp.max(_group_sum_sq(pb.astype(F32), MAP_DIM), axis=0, keepdims=True)
            kn_ref[0] = jnp.maximum(kn_ref[0], jnp.broadcast_to(kn2, (8, GW)))
        if j == C_DF_V:
            vt_ref[0] = p.T.astype(BF16)


def _in_proj(x2d, g, shift, scale, w, tabs, b, n, tm):
    nt = n // tm
    rope = tabs is not None
    in_specs = [pl.BlockSpec((tm, D_MODEL), lambda bi, i: (bi * nt + i, 0)),
                pl.BlockSpec((1, D_MODEL), lambda bi, i: (0, 0)),
                pl.BlockSpec((1, 1, D_MODEL), lambda bi, i: (bi, 0, 0)),
                pl.BlockSpec((1, 1, D_MODEL), lambda bi, i: (bi, 0, 0)),
                pl.BlockSpec((D_MODEL, N_PROJ * GW), lambda bi, i: (0, 0))]
    args = [x2d, g, shift, scale, w]
    if rope:
        in_specs.append(pl.BlockSpec((4, tm, GW), lambda bi, i: (0, i, 0)))
        args.append(tabs)
    return pl.pallas_call(
        functools.partial(_in_proj_kernel, rope=rope),
        grid=(b, nt),
        in_specs=in_specs,
        out_specs=[pl.BlockSpec((tm, N_PROJ * GW), lambda bi, i: (bi * nt + i, 0)),
                   pl.BlockSpec((1, GW, tm), lambda bi, i: (bi, 0, i)),
                   pl.BlockSpec((1, 8, GW), lambda bi, i: (bi, 0, 0))],
        out_shape=[jax.ShapeDtypeStruct((b * n, N_PROJ * GW), BF16), jax.ShapeDtypeStruct((b, GW, n), BF16),
                   jax.ShapeDtypeStruct((b, 8, GW), F32)],
        compiler_params=_cparams("arbitrary", "arbitrary"),
        name="in_proj_rope" if rope else "in_proj",
    )(*args)


def _rope_tables(n):
    t = jnp.arange(n)
    row = (t // GRID_W).astype(F32)
    col = (t % GRID_W).astype(F32)

    def tab(dim):
        nf = dim // 4
        inv = ROPE_BASE ** (-jnp.arange(nf, dtype=F32) / nf)
        ang = jnp.concatenate([row[:, None] * inv, col[:, None] * inv], axis=-1)
        cos, sin = jnp.cos(ang), jnp.sin(ang)
        reps = GW // dim
        return (jnp.tile(jnp.concatenate([cos, cos], axis=-1), (1, reps)),
                jnp.tile(jnp.concatenate([-sin, sin], axis=-1), (1, reps)))

    cr, sr = tab(HEAD_DIM)
    cd, sd = tab(HEAD_DIM // 2)
    return jnp.stack([cr, sr, cd, sd])


def _lru_kernel(xf_ref, xfp_ref, xfn_ref, xb_ref, xbp_ref, xbn_ref, cw_ref, cb_ref, wg_ref, bg_ref, lam_ref,
                h0_ref, hf_ref, hb_ref, hfin_ref, a_scr, u_scr, hc_scr, *, tn, nt):
    i = pl.program_id(1)

    @pl.when(i == 0)
    def _():
        hc_scr[...] = h0_ref[0]

    cw = cw_ref[...]
    row = lax.broadcasted_iota(jnp.int32, (tn, GW), 0)

    def coeffs(x_ref, xp_ref, xn_ref, tile, d):
        xm = x_ref[...].astype(F32)
        prev = jnp.where(tile > 0, xp_ref[LRU_HALO - 1:LRU_HALO, :].astype(F32), 0.0)
        nxt = jnp.where(tile < nt - 1, xn_ref[0:2, :].astype(F32), 0.0)
        xm1 = jnp.where(row == 0, prev, pltpu.roll(xm, 1, 0))
        xp1 = jnp.where(row == tn - 1, nxt[0:1], pltpu.roll(xm, tn - 1, 0))
        xp2 = pltpu.roll(xm, tn - 2, 0)
        xp2 = jnp.where(row == tn - 2, nxt[0:1], xp2)
        xp2 = jnp.where(row == tn - 1, nxt[1:2], xp2)
        xb = cw[0:1] * xm1 + cw[1:2] * xm + cw[2:3] * xp1 + cw[3:4] * xp2 + cb_ref[...]
        gates = _sigmoid(_dot(xb.astype(BF16), wg_ref[d]) + bg_ref[d])
        r = gates[:, :GW]
        ig = gates[:, GW:]
        nl = -lam_ref[d]
        softplus = jnp.maximum(nl, 0.0) + jnp.log1p(jnp.exp(-jnp.abs(nl)))
        log_a = -LRU_C * r * softplus
        a_scr[d] = jnp.exp(log_a)
        th = jnp.tanh(log_a)
        u_scr[d] = jnp.sqrt(-2.0 * th / (1.0 - th)) * (ig * xb)

    coeffs(xf_ref, xfp_ref, xfn_ref, i, 0)
    coeffs(xb_ref, xbp_ref, xbn_ref, nt - 1 - i, 1)

    gs = LRU_HALO
    ng = tn // gs

    def body(g, carry):
        hf, hb = carry
        fb = pl.multiple_of(g * gs, gs)
        bb = pl.multiple_of((ng - 1 - g) * gs, gs)
        af = a_scr[0, pl.ds(fb, gs), :]
        uf = u_scr[0, pl.ds(fb, gs), :]
        ab = a_scr[1, pl.ds(bb, gs), :]
        ub = u_scr[1, pl.ds(bb, gs), :]
        frows = []
        brows = [None] * gs
        for j in range(gs):
            hf = af[j:j + 1] * hf + uf[j:j + 1]
            frows.append(hf)
            jb = gs - 1 - j
            hb = ab[jb:jb + 1] * hb + ub[jb:jb + 1]
            brows[jb] = hb
        hf_ref[pl.ds(fb, gs), :] = jnp.concatenate(frows, axis=0).astype(BF16)
        hb_ref[pl.ds(bb, gs), :] = jnp.concatenate(brows, axis=0).astype(BF16)
        return hf, hb

    hf, hb = lax.fori_loop(0, ng, body, (hc_scr[0:1, :], hc_scr[1:2, :]))
    hc_scr[0:1, :] = hf
    hc_scr[1:2, :] = hb

    @pl.when(i == nt - 1)
    def _():
        hfin_ref[0] = jnp.concatenate([hf, hb, jnp.zeros((6, GW), F32)], axis=0)


def _lru(proj, b, n, tn, cw, cb, wg, bg, lam, h0):
    nt = n // tn
    hb8 = tn // LRU_HALO

    def main(rev):
        return pl.BlockSpec((tn, GW), lambda bi, i: (bi * nt + (nt - 1 - i if rev else i), C_LRU_X))

    def prev(rev):
        def im(bi, i):
            t = nt - 1 - i if rev else i
            return (jnp.maximum((bi * nt + t) * hb8 - 1, 0), C_LRU_X)
        return pl.BlockSpec((LRU_HALO, GW), im)

    def nxt(rev):
        def im(bi, i):
            t = nt - 1 - i if rev else i
            return (jnp.minimum((bi * nt + t + 1) * hb8, b * nt * hb8 - 1), C_LRU_X)
        return pl.BlockSpec((LRU_HALO, GW), im)

    const2 = lambda bi, i: (0, 0)
    const3 = lambda bi, i: (0, 0, 0)
    return pl.pallas_call(
        functools.partial(_lru_kernel, tn=tn, nt=nt),
        grid=(b, nt),
        in_specs=[main(False), prev(False), nxt(False), main(True), prev(True), nxt(True),
                  pl.BlockSpec((4, GW), const2), pl.BlockSpec((1, GW), const2),
                  pl.BlockSpec((2, GW, 2 * GW), const3), pl.BlockSpec((2, 1, 2 * GW), const3),
                  pl.BlockSpec((2, 1, GW), const3),
                  pl.BlockSpec((1, 8, GW), lambda bi, i: (bi, 0, 0))],
        out_specs=[pl.BlockSpec((tn, GW), lambda bi, i: (bi * nt + i, 0)),
                   pl.BlockSpec((tn, GW), lambda bi, i: (bi * nt + nt - 1 - i, 0)),
                   pl.BlockSpec((1, 8, GW), lambda bi, i: (bi, 0, 0))],
        out_shape=[jax.ShapeDtypeStruct((b * n, GW), BF16), jax.ShapeDtypeStruct((b * n, GW), BF16),
                   jax.ShapeDtypeStruct((b, 8, GW), F32)],
        scratch_shapes=[pltpu.VMEM((2, tn, GW), F32), pltpu.VMEM((2, tn, GW), F32), pltpu.VMEM((8, GW), F32)],
        compiler_params=_cparams("arbitrary", "arbitrary"),
        name="rglru",
    )(proj, proj, proj, proj, proj, proj, cw, cb, wg, bg, lam, h0)


def _ret_kernel(qf_ref, kf_ref, vf_ref, qb_ref, kb_ref, vb_ref, lgl_ref, lgh_ref, s0_ref,
                of_ref, ob_ref, sfin_ref, s_scr, dm_scr, qd_scr, kd_scr, *, nc, cps):
    c = pl.program_id(1)
    cs = RET_CHUNK

    @pl.when(c == 0)
    def _():
        s_scr[...] = s0_ref[0]
        ii = lax.broadcasted_iota(jnp.int32, (cs, cs), 0)
        jj = lax.broadcasted_iota(jnp.int32, (cs, cs), 1)
        t = lax.broadcasted_iota(jnp.int32, (cs, GW), 0).astype(F32)
        for d in range(2):
            rel = ii - jj if d == 0 else jj - ii - 1
            ok = rel >= 0
            relf = jnp.where(ok, rel, 0).astype(F32)
            for h in range(N_HEADS):
                lg = lgh_ref[d * N_HEADS + h:d * N_HEADS + h + 1, :]
                dm_scr[d, h] = jnp.where(ok, jnp.exp(lg * relf), 0.0)
            lgl = lgl_ref[d]
            if d == 0:
                qd_scr[d] = jnp.exp(lgl * (t + 1.0))
                kd_scr[d] = jnp.exp(lgl * (cs - 1.0 - t))
            else:
                qd_scr[d] = jnp.exp(lgl * (cs - 1.0 - t))
                kd_scr[d] = jnp.exp(lgl * t)

    lane = lax.broadcasted_iota(jnp.int32, (cs, GW), 1) // HEAD_DIM
    br = lax.broadcasted_iota(jnp.int32, (GW, GW), 0) // HEAD_DIM
    bc = lax.broadcasted_iota(jnp.int32, (GW, GW), 1) // HEAD_DIM

    def direction(d, q_ref, k_ref, v_ref, o_ref):
        s = s_scr[d]
        cdec = jnp.exp(lgl_ref[d] * float(cs))
        for step in range(cps):
            j = step if d == 0 else cps - 1 - step
            rs = slice(j * cs, (j + 1) * cs)
            k = k_ref[rs, :]
            v = v_ref[rs, :]
            q32 = q_ref[rs, :].astype(F32)
            v32 = v.astype(F32)
            parts = []
            vparts = []
            for h in range(N_HEADS):
                mk = lane == h
                qh = jnp.where(mk, q32, 0.0).astype(BF16)
                parts.append((_dot_nt(qh, k) * dm_scr[d, h]).astype(BF16))
                vparts.append(jnp.where(mk, v32, 0.0).astype(BF16))
            p = jnp.concatenate(parts, axis=1)
            vs = jnp.concatenate(vparts, axis=0)
            o = _dot(p, vs) + _dot((q32 * qd_scr[d]).astype(BF16), s.astype(BF16))
            o_ref[rs, :] = o.astype(BF16)
            kd = (k.astype(F32) * kd_scr[d]).T.astype(BF16)
            s = s * cdec + jnp.where(br == bc, _dot(kd, v), 0.0)
        s_scr[d] = s

    direction(0, qf_ref, kf_ref, vf_ref, of_ref)
    direction(1, qb_ref, kb_ref, vb_ref, ob_ref)

    @pl.when(c == nc - 1)
    def _():
        sfin_ref[0] = s_scr[...]


RET_CHUNKS_PER_STEP = 4


def _ret(proj, b, n, lgl, lgh, s0):
    cps = min(RET_CHUNKS_PER_STEP, n // RET_CHUNK)
    tb = cps * RET_CHUNK
    nc = n // tb

    def blk(col, rev):
        return pl.BlockSpec((tb, GW), lambda bi, c: (bi * nc + (nc - 1 - c if rev else c), col))

    return pl.pallas_call(
        functools.partial(_ret_kernel, nc=nc, cps=cps),
        grid=(b, nc),
        in_specs=[blk(C_RET_Q, False), blk(C_RET_K, False), blk(C_RET_V, False),
                  blk(C_RET_Q, True), blk(C_RET_K, True), blk(C_RET_V, True),
                  pl.BlockSpec((2, 1, GW), lambda bi, c: (0, 0, 0)),
                  pl.BlockSpec((2 * N_HEADS, 128), lambda bi, c: (0, 0)),
                  pl.BlockSpec((1, 2, GW, GW), lambda bi, c: (bi, 0, 0, 0))],
        out_specs=[pl.BlockSpec((tb, GW), lambda bi, c: (bi * nc + c, 0)),
                   pl.BlockSpec((tb, GW), lambda bi, c: (bi * nc + nc - 1 - c, 0)),
                   pl.BlockSpec((1, 2, GW, GW), lambda bi, c: (bi, 0, 0, 0))],
        out_shape=[jax.ShapeDtypeStruct((b * n, GW), BF16), jax.ShapeDtypeStruct((b * n, GW), BF16),
                   jax.ShapeDtypeStruct((b, 2, GW, GW), F32)],
        scratch_shapes=[pltpu.VMEM((2, GW, GW), F32), pltpu.VMEM((2, N_HEADS, RET_CHUNK, RET_CHUNK), F32),
                        pltpu.VMEM((2, RET_CHUNK, GW), F32), pltpu.VMEM((2, RET_CHUNK, GW), F32)],
        compiler_params=_cparams("arbitrary", "arbitrary"),
        name="retention",
    )(proj, proj, proj, proj, proj, proj, lgl, lgh, s0)


def _na_kernel(q_ref, k_ref, v_ref, kc_ref, vc_ref, bias_ref, o_ref, *, rows, rb):
    i0 = pl.program_id(1) * rb
    nwin = NA_WIN_ROWS * GRID_W
    kc = kc_ref[...]
    vc = vc_ref[...]
    lane = lax.broadcasted_iota(jnp.int32, (GRID_W, GW), 1) // HEAD_DIM

    def row(i, carry):
        r = i0 + i
        r0 = jnp.clip(r - NA_WIN_ROWS // 2, 0, rows - NA_WIN_ROWS)
        start = pl.multiple_of(r0 * GRID_W, GRID_W)
        qrow = pl.multiple_of(i * GRID_W, GRID_W)
        kw = k_ref[pl.ds(start, nwin), :]
        vw = v_ref[pl.ds(start, nwin), :]
        q32 = q_ref[pl.ds(qrow, GRID_W), :].astype(F32)
        qs = jnp.concatenate([jnp.where(lane == h, q32, 0.0) for h in range(N_HEADS)], axis=0).astype(BF16)
        sw = _dot_nt(qs, kw) + bias_ref[r0 - r + NA_WIN_ROWS - 1]
        sc = _dot_nt(qs, kc)
        m = jnp.maximum(jnp.max(sw, axis=-1, keepdims=True), jnp.max(sc, axis=-1, keepdims=True))
        ew = jnp.exp(sw - m)
        ec = jnp.exp(sc - m)
        l = jnp.sum(ew, axis=-1, keepdims=True) + jnp.sum(ec, axis=-1, keepdims=True)
        o = (_dot(ew.astype(BF16), vw) + _dot(ec.astype(BF16), vc)) * (1.0 / l)
        out = jnp.zeros((GRID_W, GW), F32)
        for h in range(N_HEADS):
            out = jnp.where(lane == h, o[h * GRID_W:(h + 1) * GRID_W], out)
        o_ref[pl.ds(qrow, GRID_W), :] = out.astype(BF16)
        return carry

    lax.fori_loop(0, rb, row, 0, unroll=NA_UNROLL)


NA_ROWS_PER_STEP = 16
NA_UNROLL = 8


def _na(proj_l, proj_c, bias, b, n, c):
    rows = n // GRID_W
    rb = NA_ROWS_PER_STEP
    nrb = rows // rb
    tq = rb * GRID_W
    return pl.pallas_call(
        functools.partial(_na_kernel, rows=rows, rb=rb),
        grid=(b, nrb),
        in_specs=[pl.BlockSpec((tq, GW), lambda bi, i: (bi * nrb + i, C_NA_Q)),
                  pl.BlockSpec((n, GW), lambda bi, i: (bi, C_NA_K)),
                  pl.BlockSpec((n, GW), lambda bi, i: (bi, C_NA_V)),
                  pl.BlockSpec((c, GW), lambda bi, i: (bi, C_NA_K)),
                  pl.BlockSpec((c, GW), lambda bi, i: (bi, C_NA_V)),
                  pl.BlockSpec((NA_WIN_ROWS, N_HEADS * GRID_W, NA_WIN_ROWS * GRID_W), lambda bi, i: (0, 0, 0))],
        out_specs=pl.BlockSpec((tq, GW), lambda bi, i: (bi * nrb + i, 0)),
        out_shape=jax.ShapeDtypeStruct((b * n, GW), BF16),
        compiler_params=_cparams("arbitrary", "arbitrary"),
        name="na_attention",
    )(proj_l, proj_l, proj_l, proj_c, proj_c, bias)


def _na_bias_tables(rpb):
    nr, ncol = 2 * NA_WIN_ROWS - 1, 2 * NA_WIN_COLS - 1
    cq = np.arange(GRID_W)[:, None]
    ck = np.arange(GRID_W)[None, :]
    cstart = np.clip(cq - NA_WIN_COLS // 2, 0, GRID_W - NA_WIN_COLS)
    okc = (ck >= cstart) & (ck < cstart + NA_WIN_COLS)
    dc = np.clip(ck - cq + NA_WIN_COLS - 1, 0, ncol - 1)
    ohc = (dc[..., None] == np.arange(ncol)) & okc[..., None]
    dr = np.arange(NA_WIN_ROWS)[:, None] + np.arange(NA_WIN_ROWS)[None, :]
    ohr = dr[..., None] == np.arange(nr)
    bias = jnp.einsum('vjr,hrd,ckd->vhcjk', jnp.asarray(ohr, F32), rpb.astype(F32), jnp.asarray(ohc, F32),
                      precision=lax.Precision.HIGHEST)
    bias = jnp.where(jnp.asarray(okc)[None, None, :, None, :], bias, NEG)
    return bias.reshape(NA_WIN_ROWS, N_HEADS * GRID_W, NA_WIN_ROWS * GRID_W)


def _ctx_attn_kernel(*refs, diff, out_scale):
    if diff:
        q_ref, k_ref, v_ref, lam_ref, g_ref, o_ref = refs
    else:
        q_ref, k_ref, v_ref, o_ref = refs
    k = k_ref[...]
    v = v_ref[...]
    q32 = q_ref[...].astype(F32)
    lane = lax.broadcasted_iota(jnp.int32, q32.shape, 1)

    def softmax_pv(mk):
        s = _dot_nt(jnp.where(mk, q32, 0.0).astype(BF16), k)
        z = s - jnp.max(s, axis=-1, keepdims=True)
        e = jnp.exp2(z) if diff else jnp.exp(z)
        return _dot(e.astype(BF16), v) * (1.0 / jnp.sum(e, axis=-1, keepdims=True))

    out = jnp.zeros(q32.shape, F32)
    for h in range(N_HEADS):
        if diff:
            o = (softmax_pv(lane // (HEAD_DIM // 2) == 2 * h)
                 - lam_ref[...] * softmax_pv(lane // (HEAD_DIM // 2) == 2 * h + 1))
        else:
            o = softmax_pv(lane // HEAD_DIM == h)
        out = jnp.where(lane // HEAD_DIM == h, o, out)
    if diff:
        out = out * lax.rsqrt(_head_mean_sq(out) + EPS) * g_ref[...] * out_scale
    o_ref[...] = out.astype(BF16)


def _ctx_attn(proj_c, b, c, qcol, kcol, vcol, lam=None, g=None, out_scale=1.0):
    diff = lam is not None
    in_specs = [pl.BlockSpec((c, GW), lambda bi: (bi, qcol)),
                pl.BlockSpec((c, GW), lambda bi: (bi, kcol)),
                pl.BlockSpec((c, GW), lambda bi: (bi, vcol))]
    args = [proj_c, proj_c, proj_c]
    if diff:
        in_specs += [pl.BlockSpec((1, GW), lambda bi: (0, 0)), pl.BlockSpec((1, GW), lambda bi: (0, 0))]
        args += [lam, g]
    return pl.pallas_call(
        functools.partial(_ctx_attn_kernel, diff=diff, out_scale=out_scale),
        grid=(b,),
        in_specs=in_specs,
        out_specs=pl.BlockSpec((c, GW), lambda bi: (bi, 0)),
        out_shape=jax.ShapeDtypeStruct((b * c, GW), BF16),
        compiler_params=_cparams("arbitrary"),
        name="ctx_diff_attention" if diff else "ctx_attention",
    )(*args)


def _diff_key_tile(nk):
    return next(t for t in (768, 512, 384, 256, 128) if nk % t == 0)


V_AUG = HEAD_DIM + 16


DIFF_HEADROOM = 100.0
DIFF_MIN_SUM = 2.0 ** -100


def _diff_lat_kernel(q_ref, k_ref, vt_ref, kn_ref, lam_ref, g_ref, o_ref,
                     qm_scr, sh_scr, m_scr, acc_scr, *, tq, tk, nk, out_scale):
    nt = nk // tk
    q32 = q_ref[...].astype(F32)
    qt = q32.T
    rowg = lax.broadcasted_iota(jnp.int32, (GW, tq), 0) // MAP_DIM
    for u in range(N_MAPS):
        qm_scr[u] = jnp.where(rowg == u, qt, 0.0).astype(BF16)
    bound = jnp.sqrt(_group_sum_sq(q32 * jnp.sqrt(kn_ref[0, 0:1, :]), MAP_DIM)) * (1.0 + 2.0 ** -8)
    sh_scr[...] = bound.T - DIFF_HEADROOM
    acc_scr[...] = jnp.zeros(acc_scr.shape, F32)

    def v_tile(st, u):
        h = u // 2
        return vt_ref[0, h * V_AUG:(h + 1) * V_AUG, pl.ds(st, tk)]

    def fast(t, carry):
        st = pl.multiple_of(t * tk, tk)
        kt = k_ref[0, pl.ds(st, tk), :]
        for u in range(N_MAPS):
            s = _dot(kt, qm_scr[u])
            e = jnp.exp2(s - sh_scr[u * MAP_DIM:u * MAP_DIM + 1, :]).astype(BF16)
            acc_scr[u] = acc_scr[u] + _dot(v_tile(st, u), e)
        return carry

    lax.fori_loop(0, nt, fast, 0)

    lmin = acc_scr[0, HEAD_DIM:HEAD_DIM + 1, :]
    for u in range(1, N_MAPS):
        lmin = jnp.minimum(lmin, acc_scr[u, HEAD_DIM:HEAD_DIM + 1, :])

    @pl.when(jnp.logical_not(jnp.min(lmin) >= DIFF_MIN_SUM))
    def _():
        m_scr[...] = jnp.full(m_scr.shape, NEG, F32)
        acc_scr[...] = jnp.zeros(acc_scr.shape, F32)

        def exact(t, carry):
            st = pl.multiple_of(t * tk, tk)
            kt = k_ref[0, pl.ds(st, tk), :]
            for u in range(N_MAPS):
                s = _dot(kt, qm_scr[u])
                mo = m_scr[u:u + 1, :]
                mn = jnp.maximum(mo, jnp.max(s, axis=0, keepdims=True))
                m_scr[u:u + 1, :] = mn
                e = jnp.exp2(s - mn).astype(BF16)
                acc_scr[u] = jnp.exp2(mo - mn) * acc_scr[u] + _dot(v_tile(st, u), e)
            return carry

        lax.fori_loop(0, nt, exact, 0)

    lam = lam_ref[0:1, 0:1]
    outs = []
    for h in range(N_HEADS):
        a1 = acc_scr[2 * h]
        a2 = acc_scr[2 * h + 1]
        o1 = a1[:HEAD_DIM] * (1.0 / a1[HEAD_DIM:HEAD_DIM + 1])
        o2 = a2[:HEAD_DIM] * (1.0 / a2[HEAD_DIM:HEAD_DIM + 1])
        outs.append(o1 - lam * o2)
    out = jnp.concatenate(outs, axis=0).T
    out = out * lax.rsqrt(_head_mean_sq(out) + EPS) * g_ref[...] * out_scale
    o_ref[...] = out.astype(BF16)


def _diff_lat(proj_l, k_all, vt_aug, kn2, lam, g, b, n, tq, tk, out_scale):
    nt = n // tq
    nk = k_all.shape[1]
    assert nk % tk == 0
    return pl.pallas_call(
        functools.partial(_diff_lat_kernel, tq=tq, tk=tk, nk=nk, out_scale=out_scale),
        grid=(b, nt),
        in_specs=[pl.BlockSpec((tq, GW), lambda bi, i: (bi * nt + i, C_DF_Q)),
                  pl.BlockSpec((1, nk, GW), lambda bi, i: (bi, 0, 0)),
                  pl.BlockSpec((1, N_HEADS * V_AUG, nk), lambda bi, i: (bi, 0, 0)),
                  pl.BlockSpec((1, 8, GW), lambda bi, i: (bi, 0, 0)),
                  pl.BlockSpec((1, GW), lambda bi, i: (0, 0)),
                  pl.BlockSpec((1, GW), lambda bi, i: (0, 0))],
        out_specs=pl.BlockSpec((tq, GW), lambda bi, i: (bi * nt + i, 0)),
        out_shape=jax.ShapeDtypeStruct((b * n, GW), BF16),
        scratch_shapes=[pltpu.VMEM((N_MAPS, GW, tq), BF16), pltpu.VMEM((GW, tq), F32),
                        pltpu.VMEM((N_MAPS, tq), F32), pltpu.VMEM((N_MAPS, V_AUG, tq), F32)],
        compiler_params=_cparams("arbitrary", "arbitrary"),
        name="diff_attention_lat",
    )(proj_l, k_all, vt_aug, kn2, lam, g)


def _out_proj_kernel(x_ref, ya_ref, hf_ref, hb_ref, gl_ref, of_ref, ob_ref, gr_ref, yd_ref, w_ref, g_ref, gate_ref,
                     o_ref):
    gl = gl_ref[...].astype(F32)
    gelu = 0.5 * gl * (1.0 + jnp.tanh(math.sqrt(2.0 / math.pi) * (gl + 0.044715 * (gl * gl * gl))))
    yb = (hf_ref[...].astype(F32) + hb_ref[...].astype(F32)) * gelu
    r = of_ref[...].astype(F32) + ob_ref[...].astype(F32)
    gr = gr_ref[...].astype(F32)
    yc = r * lax.rsqrt(_head_mean_sq(r) + EPS) * (gr * _sigmoid(gr))
    y = (_dot(ya_ref[...], w_ref[0:GW, :]) + _dot(yb.astype(BF16), w_ref[GW:2 * GW, :])
         + _dot(yc.astype(BF16), w_ref[2 * GW:3 * GW, :]) + _dot(yd_ref[...], w_ref[3 * GW:4 * GW, :]))
    o_ref[...] = x_ref[...] + gate_ref[0] * _rms(y, g_ref[...])


def _out_proj(x2d, ya, hf, hb, of, ob, yd, proj, w, g, gate, b, n, tm):
    nt = n // tm
    tok = lambda col: pl.BlockSpec((tm, GW), lambda bi, i: (bi * nt + i, col))
    return pl.pallas_call(
        _out_proj_kernel,
        grid=(b, nt),
        in_specs=[pl.BlockSpec((tm, D_MODEL), lambda bi, i: (bi * nt + i, 0)),
                  tok(0), tok(0), tok(0), tok(C_LRU_G), tok(0), tok(0), tok(C_RET_G), tok(0),
                  pl.BlockSpec((D_MODEL, D_MODEL), lambda bi, i: (0, 0)),
                  pl.BlockSpec((1, D_MODEL), lambda bi, i: (0, 0)),
                  pl.BlockSpec((1, 1, D_MODEL), lambda bi, i: (bi, 0, 0))],
        out_specs=pl.BlockSpec((tm, D_MODEL), lambda bi, i: (bi * nt + i, 0)),
        out_shape=jax.ShapeDtypeStruct((b * n, D_MODEL), F32),
        compiler_params=_cparams("arbitrary", "arbitrary"),
        name="out_proj",
    )(x2d, ya, hf, hb, proj, of, ob, proj, yd, w, g, gate)


FFN_CHUNK = 256


def _ffn_kernel(x_ref, g1_ref, sh_ref, sc_ref, w1_ref, w2_ref, g2_ref, gate_ref, o_ref):
    x = x_ref[...]
    h = (_rms(x, g1_ref[...]) * (1.0 + sc_ref[0]) + sh_ref[0]).astype(BF16)
    acc = jnp.zeros(x.shape, F32)
    for c in range(D_FF // FFN_CHUNK):
        lo, hi = c * FFN_CHUNK, (c + 1) * FFN_CHUNK
        gt = _dot(h, w1_ref[:, lo:hi])
        up = _dot(h, w1_ref[:, D_FF + lo:D_FF + hi])
        act = (gt * _sigmoid(gt) * up).astype(BF16)
        acc = acc + _dot(act, w2_ref[lo:hi, :])
    o_ref[...] = x + gate_ref[0] * _rms(acc, g2_ref[...])


def _ffn(x2d, g1, shift, scale, w1, w2, g2, gate, b, n, tm):
    nt = n // tm
    vec = pl.BlockSpec((1, D_MODEL), lambda bi, i: (0, 0))
    per_b = pl.BlockSpec((1, 1, D_MODEL), lambda bi, i: (bi, 0, 0))
    return pl.pallas_call(
        _ffn_kernel,
        grid=(b, nt),
        in_specs=[pl.BlockSpec((tm, D_MODEL), lambda bi, i: (bi * nt + i, 0)), vec, per_b, per_b,
                  pl.BlockSpec((D_MODEL, 2 * D_FF), lambda bi, i: (0, 0)),
                  pl.BlockSpec((D_FF, D_MODEL), lambda bi, i: (0, 0)), vec, per_b],
        out_specs=pl.BlockSpec((tm, D_MODEL), lambda bi, i: (bi * nt + i, 0)),
        out_shape=jax.ShapeDtypeStruct((b * n, D_MODEL), F32),
        compiler_params=_cparams("arbitrary", "arbitrary"),
        name="ffn",
    )(x2d, g1, shift, scale, w1, w2, g2, gate)


def _blockdiag(w):
    nb, bs, _ = w.shape
    return jnp.einsum('kcd,kj->kcjd', w, jnp.eye(nb, dtype=w.dtype)).reshape(nb * bs, nb * bs)


def _lru_params(conv_w, conv_b, gate_w, gate_b, lam):
    wg = jnp.stack([jnp.concatenate([_blockdiag(gate_w[d, 0]), _blockdiag(gate_w[d, 1])], axis=1)
                    for d in range(2)]).astype(BF16)
    bg = jnp.stack([jnp.concatenate([gate_b[d, 0].reshape(1, GW), gate_b[d, 1].reshape(1, GW)], axis=1)
                    for d in range(2)])
    return conv_w, conv_b.reshape(1, GW), wg, bg, lam.reshape(2, 1, GW)


def kernel(x, c, ctx, c_ctx, w_mod, b_mod, g_pre_mix, g_post_mix, g_pre_ffn, g_post_ffn, w_in, na_rpb, lru_conv_w,
           lru_conv_b, lru_gate_w, lru_gate_b, lru_lambda, ret_decay, diff_lambda, diff_subln, w_out, w_ffn_in,
           w_ffn_out):
    b, n, _ = x.shape
    nc = ctx.shape[1]
    depth = w_mod.shape[0]
    assert n % 1024 == 0 and nc % RET_CHUNK == 0

    r = -(-(b + 1) // 8) * 8
    cc = jnp.zeros((r, D_MODEL), F32).at[:b].set(c).at[b].set(c_ctx)
    mod = _mod(cc, w_mod, b_mod)
    tabs = _rope_tables(n)

    xl = x.reshape(b * n, D_MODEL)
    xc = ctx.reshape(b * nc, D_MODEL)
    tm = 512
    vec = lambda a: a.reshape(1, D_MODEL)
    for l in range(depth):
        last = l == depth - 1
        lam_init = 0.8 - 0.6 * math.exp(-0.3 * l)
        ml = [mod[l, :b, k * D_MODEL:(k + 1) * D_MODEL].reshape(b, 1, D_MODEL) for k in range(6)]
        mc = [jnp.broadcast_to(mod[l, b, k * D_MODEL:(k + 1) * D_MODEL], (b, 1, D_MODEL)) for k in range(6)]
        w_in_l = w_in[l].astype(BF16)
        w_out_l = w_out[l].astype(BF16)
        w1_l = w_ffn_in[l].astype(BF16)
        w2_l = w_ffn_out[l].astype(BF16)

        pc, vt_c, kn_c = _in_proj(xc, vec(g_pre_mix[l]), mc[0], mc[1], w_in_l, None, b, nc, nc)
        pt, vt_l, kn_l = _in_proj(xl, vec(g_pre_mix[l]), ml[0], ml[1], w_in_l, tabs, b, n, tm)

        lru_p = _lru_params(lru_conv_w[l], lru_conv_b[l], lru_gate_w[l], lru_gate_b[l], lru_lambda[l])
        hf_c, hb_c, hfin = _lru(pc, b, nc, nc, *lru_p, jnp.zeros((b, 8, GW), F32))
        hf_l, hb_l, _ = _lru(pt, b, n, 1024, *lru_p, hfin)

        log_g = jax.nn.log_sigmoid(ret_decay[l].astype(F32))
        lgl = jnp.repeat(log_g, HEAD_DIM, axis=-1).reshape(2, 1, GW)
        lgh = jnp.broadcast_to(log_g.reshape(2 * N_HEADS, 1), (2 * N_HEADS, 128))
        of_c, ob_c, sfin = _ret(pc, b, nc, lgl, lgh, jnp.zeros((b, 2, GW, GW), F32))
        of_l, ob_l, _ = _ret(pt, b, n, lgl, lgh, sfin)

        ya_l = _na(pt, pc, _na_bias_tables(na_rpb[l]), b, n, nc)

        lq1, lk1, lq2, lk2 = diff_lambda[l].astype(F32)
        lam = jnp.exp(jnp.sum(lq1 * lk1)) - jnp.exp(jnp.sum(lq2 * lk2)) + lam_init
        lam_v = jnp.broadcast_to(lam, (1, GW)).astype(F32)
        g_sub = jnp.tile(diff_subln[l].astype(F32), N_HEADS).reshape(1, GW)
        k_all = jnp.concatenate([pc[:, C_DF_K * GW:(C_DF_K + 1) * GW].reshape(b, nc, GW),
                                 pt[:, C_DF_K * GW:(C_DF_K + 1) * GW].reshape(b, n, GW)], axis=1)
        vt_all = jnp.concatenate([vt_c, vt_l], axis=2).reshape(b, N_HEADS, HEAD_DIM, nc + n)
        vt_aug = jnp.concatenate([vt_all, jnp.ones((b, N_HEADS, V_AUG - HEAD_DIM, nc + n), BF16)],
                                 axis=2).reshape(b, N_HEADS * V_AUG, nc + n)
        yd_l = _diff_lat(pt, k_all, vt_aug, jnp.maximum(kn_c, kn_l), lam_v, g_sub, b, n, 512,
                         _diff_key_tile(nc + n), 1.0 - lam_init)

        x_mid = _out_proj(xl, ya_l, hf_l, hb_l, of_l, ob_l, yd_l, pt, w_out_l, vec(g_post_mix[l]), ml[2], b, n, tm)
        xl_new = _ffn(x_mid, vec(g_pre_ffn[l]), ml[3], ml[4], w1_l, w2_l, vec(g_post_ffn[l]), ml[5], b, n, tm)

        if not last:
            ya_c = _ctx_attn(pc, b, nc, C_NA_Q, C_NA_K, C_NA_V)
            yd_c = _ctx_attn(pc, b, nc, C_DF_Q, C_DF_K, C_DF_V, lam_v, g_sub, 1.0 - lam_init)
            xc_mid = _out_proj(xc, ya_c, hf_c, hb_c, of_c, ob_c, yd_c, pc, w_out_l, vec(g_post_mix[l]), mc[2],
                               b, nc, nc)
            xc = _ffn(xc_mid, vec(g_pre_ffn[l]), mc[3], mc[4], w1_l, w2_l, vec(g_post_ffn[l]), mc[5], b, nc, nc)
        xl = xl_new
    return xl.reshape(b, n, D_MODEL)
```

```python
import functools
import math

import numpy as np
import jax
import jax.numpy as jnp
from jax import lax
from jax.experimental import pallas as pl
from jax.experimental.pallas import tpu as pltpu

F32 = jnp.float32
BF16 = jnp.bfloat16

D_MODEL = 1024
GRID_W = 64
HEAD_DIM = 64
N_HEADS = 4
GW = N_HEADS * HEAD_DIM
N_PROJ = 12
N_MAPS = 2 * N_HEADS
MAP_DIM = HEAD_DIM // 2
D_FF = 2816
NA_WIN_ROWS = 8
NA_WIN_COLS = 16
LRU_C = 8.0
RET_CHUNK = 128
ROPE_BASE = 10000.0
EPS = 1e-6
NEG = -1e30

C_NA_Q, C_NA_K, C_NA_V, C_LRU_X, C_LRU_G, C_RET_Q, C_RET_K, C_RET_V, C_RET_G, C_DF_Q, C_DF_K, C_DF_V = range(12)
_COL_SCALE = {C_NA_Q: HEAD_DIM ** -0.5, C_RET_K: HEAD_DIM ** -0.5,
              C_DF_Q: (HEAD_DIM // 2) ** -0.5 * math.log2(math.e)}

LRU_HALO = 16
V7X_VMEM_LIMIT = 56 * 1024 * 1024


def _cparams(*sem):
    return pltpu.CompilerParams(dimension_semantics=sem, vmem_limit_bytes=V7X_VMEM_LIMIT)


def _dot(a, b):
    return jnp.dot(a, b, preferred_element_type=F32)


def _dot_nt(a, b):
    return lax.dot_general(a, b, (((1,), (1,)), ((), ())), preferred_element_type=F32)


def _rms(x, g):
    return x * lax.rsqrt(jnp.mean(x * x, axis=-1, keepdims=True) + EPS) * g


def _sigmoid(x):
    return 1.0 / (1.0 + jnp.exp(-x))


def _group_sum_sq(y, group):
    y2 = y * y
    hi = y2.astype(BF16)
    lo = (y2 - hi.astype(F32)).astype(BF16)
    r = lax.broadcasted_iota(jnp.int32, (GW, GW), 0) // group
    c = lax.broadcasted_iota(jnp.int32, (GW, GW), 1) // group
    bd = jnp.where(r == c, 1.0, 0.0).astype(BF16)
    return _dot(hi, bd) + _dot(lo, bd)


def _head_mean_sq(y):
    return _group_sum_sq(y, HEAD_DIM) * (1.0 / HEAD_DIM)


def _mod_kernel(c_ref, w_ref, b_ref, o_ref):
    c = c_ref[...]
    s = c * _sigmoid(c)
    o_ref[0] = jnp.dot(s, w_ref[0], preferred_element_type=F32, precision=lax.Precision.HIGHEST) + b_ref[0]


def _mod(cc, w_mod, b_mod):
    depth = w_mod.shape[0]
    r = cc.shape[0]
    tn = 1536
    return pl.pallas_call(
        _mod_kernel,
        grid=(depth, 6 * D_MODEL // tn),
        in_specs=[pl.BlockSpec((r, D_MODEL), lambda l, j: (0, 0)),
                  pl.BlockSpec((1, D_MODEL, tn), lambda l, j: (l, 0, j)),
                  pl.BlockSpec((1, 1, tn), lambda l, j: (l, 0, j))],
        out_specs=pl.BlockSpec((1, r, tn), lambda l, j: (l, 0, j)),
        out_shape=jax.ShapeDtypeStruct((depth, r, 6 * D_MODEL), F32),
        compiler_params=_cparams("arbitrary", "arbitrary"),
        name="adaln_mod",
    )(cc, w_mod, b_mod.reshape(depth, 1, 6 * D_MODEL))


def _rope(p, cos, sin_signed, half):
    outs = []
    for c in range(GW // 128):
        xs = p[:, c * 128:(c + 1) * 128]
        lane = lax.broadcasted_iota(jnp.int32, xs.shape, 1)
        first = (lane % (2 * half)) < half
        partner = jnp.where(first, pltpu.roll(xs, 128 - half, 1), pltpu.roll(xs, half, 1))
        outs.append(xs * cos[:, c * 128:(c + 1) * 128] + partner * sin_signed[:, c * 128:(c + 1) * 128])
    return jnp.concatenate(outs, axis=1)


def _in_proj_kernel(*refs, rope):
    if rope:
        x_ref, g_ref, sh_ref, sc_ref, w_ref, tab_ref, o_ref, vt_ref, kn_ref = refs
    else:
        x_ref, g_ref, sh_ref, sc_ref, w_ref, o_ref, vt_ref, kn_ref = refs

    @pl.when(pl.program_id(1) == 0)
    def _():
        kn_ref[...] = jnp.zeros(kn_ref.shape, F32)

    h = _rms(x_ref[...], g_ref[...]) * (1.0 + sc_ref[0]) + sh_ref[0]
    hb = h.astype(BF16)
    for j in range(N_PROJ):
        p = _dot(hb, w_ref[:, j * GW:(j + 1) * GW])
        if j in _COL_SCALE:
            p = p * _COL_SCALE[j]
        if rope and j in (C_RET_Q, C_RET_K):
            p = _rope(p, tab_ref[0], tab_ref[1], HEAD_DIM // 2)
        if rope and j in (C_DF_Q, C_DF_K):
            p = _rope(p, tab_ref[2], tab_ref[3], HEAD_DIM // 4)
        pb = p.astype(BF16)
        o_ref[:, j * GW:(j + 1) * GW] = pb
        if j == C_DF_K:
            kf = pb.astype(F32)
            kn_ref[0] = jnp.maximum(kn_ref[0], jnp.broadcast_to(jnp.max(kf * kf, axis=0, keepdims=True), (8, GW)))
        if j == C_DF_V:
            vt_ref[0] = p.T.astype(BF16)


def _in_proj(x2d, g, shift, scale, w, tabs, b, n, tm):
    nt = n // tm
    rope = tabs is not None
    in_specs = [pl.BlockSpec((tm, D_MODEL), lambda bi, i: (bi * nt + i, 0)),
                pl.BlockSpec((1, D_MODEL), lambda bi, i: (0, 0)),
                pl.BlockSpec((1, 1, D_MODEL), lambda bi, i: (bi, 0, 0)),
                pl.BlockSpec((1, 1, D_MODEL), lambda bi, i: (bi, 0, 0)),
                pl.BlockSpec((D_MODEL, N_PROJ * GW), lambda bi, i: (0, 0))]
    args = [x2d, g, shift, scale, w]
    if rope:
        in_specs.append(pl.BlockSpec((4, tm, GW), lambda bi, i: (0, i, 0)))
        args.append(tabs)
    return pl.pallas_call(
        functools.partial(_in_proj_kernel, rope=rope),
        grid=(b, nt),
        in_specs=in_specs,
        out_specs=[pl.BlockSpec((tm, N_PROJ * GW), lambda bi, i: (bi * nt + i, 0)),
                   pl.BlockSpec((1, GW, tm), lambda bi, i: (bi, 0, i)),
                   pl.BlockSpec((1, 8, GW), lambda bi, i: (bi, 0, 0))],
        out_shape=[jax.ShapeDtypeStruct((b * n, N_PROJ * GW), BF16), jax.ShapeDtypeStruct((b, GW, n), BF16),
                   jax.ShapeDtypeStruct((b, 8, GW), F32)],
        compiler_params=_cparams("arbitrary", "arbitrary"),
        name="in_proj_rope" if rope else "in_proj",
    )(*args)


def _rope_tables(n):
    t = jnp.arange(n)
    row = (t // GRID_W).astype(F32)
    col = (t % GRID_W).astype(F32)

    def tab(dim):
        nf = dim // 4
        inv = ROPE_BASE ** (-jnp.arange(nf, dtype=F32) / nf)
        ang = jnp.concatenate([row[:, None] * inv, col[:, None] * inv], axis=-1)
        cos, sin = jnp.cos(ang), jnp.sin(ang)
        reps = GW // dim
        return (jnp.tile(jnp.concatenate([cos, cos], axis=-1), (1, reps)),
                jnp.tile(jnp.concatenate([-sin, sin], axis=-1), (1, reps)))

    cr, sr = tab(HEAD_DIM)
    cd, sd = tab(HEAD_DIM // 2)
    return jnp.stack([cr, sr, cd, sd])


def _lru_kernel(xf_ref, xfp_ref, xfn_ref, xb_ref, xbp_ref, xbn_ref, cw_ref, cb_ref, wg_ref, bg_ref, lam_ref,
                h0_ref, hf_ref, hb_ref, hfin_ref, a_scr, u_scr, hc_scr, *, tn, nt):
    i = pl.program_id(1)

    @pl.when(i == 0)
    def _():
        hc_scr[...] = h0_ref[0]

    cw = cw_ref[...]
    row = lax.broadcasted_iota(jnp.int32, (tn, GW), 0)

    def coeffs(x_ref, xp_ref, xn_ref, tile, d):
        xm = x_ref[...].astype(F32)
        prev = jnp.where(tile > 0, xp_ref[LRU_HALO - 1:LRU_HALO, :].astype(F32), 0.0)
        nxt = jnp.where(tile < nt - 1, xn_ref[0:2, :].astype(F32), 0.0)
        xm1 = jnp.where(row == 0, prev, pltpu.roll(xm, 1, 0))
        xp1 = jnp.where(row == tn - 1, nxt[0:1], pltpu.roll(xm, tn - 1, 0))
        xp2 = pltpu.roll(xm, tn - 2, 0)
        xp2 = jnp.where(row == tn - 2, nxt[0:1], xp2)
        xp2 = jnp.where(row == tn - 1, nxt[1:2], xp2)
        xb = cw[0:1] * xm1 + cw[1:2] * xm + cw[2:3] * xp1 + cw[3:4] * xp2 + cb_ref[...]
        gates = _sigmoid(_dot(xb.astype(BF16), wg_ref[d]) + bg_ref[d])
        r = gates[:, :GW]
        ig = gates[:, GW:]
        nl = -lam_ref[d]
        softplus = jnp.maximum(nl, 0.0) + jnp.log1p(jnp.exp(-jnp.abs(nl)))
        log_a = -LRU_C * r * softplus
        a_scr[d] = jnp.exp(log_a)
        th = jnp.tanh(log_a)
        u_scr[d] = jnp.sqrt(-2.0 * th / (1.0 - th)) * (ig * xb)

    coeffs(xf_ref, xfp_ref, xfn_ref, i, 0)
    coeffs(xb_ref, xbp_ref, xbn_ref, nt - 1 - i, 1)

    gs = LRU_HALO
    ng = tn // gs

    def body(g, carry):
        hf, hb = carry
        fb = pl.multiple_of(g * gs, gs)
        bb = pl.multiple_of((ng - 1 - g) * gs, gs)
        af = a_scr[0, pl.ds(fb, gs), :]
        uf = u_scr[0, pl.ds(fb, gs), :]
        ab = a_scr[1, pl.ds(bb, gs), :]
        ub = u_scr[1, pl.ds(bb, gs), :]
        frows = []
        brows = [None] * gs
        for j in range(gs):
            hf = af[j:j + 1] * hf + uf[j:j + 1]
            frows.append(hf)
            jb = gs - 1 - j
            hb = ab[jb:jb + 1] * hb + ub[jb:jb + 1]
            brows[jb] = hb
        hf_ref[pl.ds(fb, gs), :] = jnp.concatenate(frows, axis=0).astype(BF16)
        hb_ref[pl.ds(bb, gs), :] = jnp.concatenate(brows, axis=0).astype(BF16)
        return hf, hb

    hf, hb = lax.fori_loop(0, ng, body, (hc_scr[0:1, :], hc_scr[1:2, :]))
    hc_scr[0:1, :] = hf
    hc_scr[1:2, :] = hb

    @pl.when(i == nt - 1)
    def _():
        hfin_ref[0] = jnp.concatenate([hf, hb, jnp.zeros((6, GW), F32)], axis=0)


def _lru(proj, b, n, tn, cw, cb, wg, bg, lam, h0):
    nt = n // tn
    hb8 = tn // LRU_HALO

    def main(rev):
        return pl.BlockSpec((tn, GW), lambda bi, i: (bi * nt + (nt - 1 - i if rev else i), C_LRU_X))

    def prev(rev):
        def im(bi, i):
            t = nt - 1 - i if rev else i
            return (jnp.maximum((bi * nt + t) * hb8 - 1, 0), C_LRU_X)
        return pl.BlockSpec((LRU_HALO, GW), im)

    def nxt(rev):
        def im(bi, i):
            t = nt - 1 - i if rev else i
            return (jnp.minimum((bi * nt + t + 1) * hb8, b * nt * hb8 - 1), C_LRU_X)
        return pl.BlockSpec((LRU_HALO, GW), im)

    const2 = lambda bi, i: (0, 0)
    const3 = lambda bi, i: (0, 0, 0)
    return pl.pallas_call(
        functools.partial(_lru_kernel, tn=tn, nt=nt),
        grid=(b, nt),
        in_specs=[main(False), prev(False), nxt(False), main(True), prev(True), nxt(True),
                  pl.BlockSpec((4, GW), const2), pl.BlockSpec((1, GW), const2),
                  pl.BlockSpec((2, GW, 2 * GW), const3), pl.BlockSpec((2, 1, 2 * GW), const3),
                  pl.BlockSpec((2, 1, GW), const3),
                  pl.BlockSpec((1, 8, GW), lambda bi, i: (bi, 0, 0))],
        out_specs=[pl.BlockSpec((tn, GW), lambda bi, i: (bi * nt + i, 0)),
                   pl.BlockSpec((tn, GW), lambda bi, i: (bi * nt + nt - 1 - i, 0)),
                   pl.BlockSpec((1, 8, GW), lambda bi, i: (bi, 0, 0))],
        out_shape=[jax.ShapeDtypeStruct((b * n, GW), BF16), jax.ShapeDtypeStruct((b * n, GW), BF16),
                   jax.ShapeDtypeStruct((b, 8, GW), F32)],
        scratch_shapes=[pltpu.VMEM((2, tn, GW), F32), pltpu.VMEM((2, tn, GW), F32), pltpu.VMEM((8, GW), F32)],
        compiler_params=_cparams("arbitrary", "arbitrary"),
        name="rglru",
    )(proj, proj, proj, proj, proj, proj, cw, cb, wg, bg, lam, h0)


def _ret_kernel(qf_ref, kf_ref, vf_ref, qb_ref, kb_ref, vb_ref, lgl_ref, lgh_ref, s0_ref,
                of_ref, ob_ref, sfin_ref, s_scr, dm_scr, qd_scr, kd_scr, *, nc, cps):
    c = pl.program_id(1)
    cs = RET_CHUNK

    @pl.when(c == 0)
    def _():
        s_scr[...] = s0_ref[0]
        ii = lax.broadcasted_iota(jnp.int32, (cs, cs), 0)
        jj = lax.broadcasted_iota(jnp.int32, (cs, cs), 1)
        t = lax.broadcasted_iota(jnp.int32, (cs, GW), 0).astype(F32)
        for d in range(2):
            rel = ii - jj if d == 0 else jj - ii - 1
            ok = rel >= 0
            relf = jnp.where(ok, rel, 0).astype(F32)
            for h in range(N_HEADS):
                lg = lgh_ref[d * N_HEADS + h:d * N_HEADS + h + 1, :]
                dm_scr[d, h] = jnp.where(ok, jnp.exp(lg * relf), 0.0)
            lgl = lgl_ref[d]
            if d == 0:
                qd_scr[d] = jnp.exp(lgl * (t + 1.0))
                kd_scr[d] = jnp.exp(lgl * (cs - 1.0 - t))
            else:
                qd_scr[d] = jnp.exp(lgl * (cs - 1.0 - t))
                kd_scr[d] = jnp.exp(lgl * t)

    lane = lax.broadcasted_iota(jnp.int32, (cs, GW), 1) // HEAD_DIM
    br = lax.broadcasted_iota(jnp.int32, (GW, GW), 0) // HEAD_DIM
    bc = lax.broadcasted_iota(jnp.int32, (GW, GW), 1) // HEAD_DIM

    def direction(d, q_ref, k_ref, v_ref, o_ref):
        s = s_scr[d]
        cdec = jnp.exp(lgl_ref[d] * float(cs))
        for step in range(cps):
            j = step if d == 0 else cps - 1 - step
            rs = slice(j * cs, (j + 1) * cs)
            k = k_ref[rs, :]
            v = v_ref[rs, :]
            q32 = q_ref[rs, :].astype(F32)
            v32 = v.astype(F32)
            parts = []
            vparts = []
            for h in range(N_HEADS):
                mk = lane == h
                qh = jnp.where(mk, q32, 0.0).astype(BF16)
                parts.append((_dot_nt(qh, k) * dm_scr[d, h]).astype(BF16))
                vparts.append(jnp.where(mk, v32, 0.0).astype(BF16))
            p = jnp.concatenate(parts, axis=1)
            vs = jnp.concatenate(vparts, axis=0)
            o = _dot(p, vs) + _dot((q32 * qd_scr[d]).astype(BF16), s.astype(BF16))
            o_ref[rs, :] = o.astype(BF16)
            kd = (k.astype(F32) * kd_scr[d]).T.astype(BF16)
            s = s * cdec + jnp.where(br == bc, _dot(kd, v), 0.0)
        s_scr[d] = s

    direction(0, qf_ref, kf_ref, vf_ref, of_ref)
    direction(1, qb_ref, kb_ref, vb_ref, ob_ref)

    @pl.when(c == nc - 1)
    def _():
        sfin_ref[0] = s_scr[...]


RET_CHUNKS_PER_STEP = 4


def _ret(proj, b, n, lgl, lgh, s0):
    cps = min(RET_CHUNKS_PER_STEP, n // RET_CHUNK)
    tb = cps * RET_CHUNK
    nc = n // tb

    def blk(col, rev):
        return pl.BlockSpec((tb, GW), lambda bi, c: (bi * nc + (nc - 1 - c if rev else c), col))

    return pl.pallas_call(
        functools.partial(_ret_kernel, nc=nc, cps=cps),
        grid=(b, nc),
        in_specs=[blk(C_RET_Q, False), blk(C_RET_K, False), blk(C_RET_V, False),
                  blk(C_RET_Q, True), blk(C_RET_K, True), blk(C_RET_V, True),
                  pl.BlockSpec((2, 1, GW), lambda bi, c: (0, 0, 0)),
                  pl.BlockSpec((2 * N_HEADS, 128), lambda bi, c: (0, 0)),
                  pl.BlockSpec((1, 2, GW, GW), lambda bi, c: (bi, 0, 0, 0))],
        out_specs=[pl.BlockSpec((tb, GW), lambda bi, c: (bi * nc + c, 0)),
                   pl.BlockSpec((tb, GW), lambda bi, c: (bi * nc + nc - 1 - c, 0)),
                   pl.BlockSpec((1, 2, GW, GW), lambda bi, c: (bi, 0, 0, 0))],
        out_shape=[jax.ShapeDtypeStruct((b * n, GW), BF16), jax.ShapeDtypeStruct((b * n, GW), BF16),
                   jax.ShapeDtypeStruct((b, 2, GW, GW), F32)],
        scratch_shapes=[pltpu.VMEM((2, GW, GW), F32), pltpu.VMEM((2, N_HEADS, RET_CHUNK, RET_CHUNK), F32),
                        pltpu.VMEM((2, RET_CHUNK, GW), F32), pltpu.VMEM((2, RET_CHUNK, GW), F32)],
        compiler_params=_cparams("arbitrary", "arbitrary"),
        name="retention",
    )(proj, proj, proj, proj, proj, proj, lgl, lgh, s0)


def _na_kernel(q_ref, k_ref, v_ref, kc_ref, vc_ref, bias_ref, o_ref, *, rows, rb):
    i0 = pl.program_id(1) * rb
    nwin = NA_WIN_ROWS * GRID_W
    kc = kc_ref[...]
    vc = vc_ref[...]
    lane = lax.broadcasted_iota(jnp.int32, (GRID_W, GW), 1) // HEAD_DIM

    def row(i, carry):
        r = i0 + i
        r0 = jnp.clip(r - NA_WIN_ROWS // 2, 0, rows - NA_WIN_ROWS)
        start = pl.multiple_of(r0 * GRID_W, GRID_W)
        qrow = pl.multiple_of(i * GRID_W, GRID_W)
        kw = k_ref[pl.ds(start, nwin), :]
        vw = v_ref[pl.ds(start, nwin), :]
        q32 = q_ref[pl.ds(qrow, GRID_W), :].astype(F32)
        qs = jnp.concatenate([jnp.where(lane == h, q32, 0.0) for h in range(N_HEADS)], axis=0).astype(BF16)
        sw = _dot_nt(qs, kw) + bias_ref[r0 - r + NA_WIN_ROWS - 1]
        sc = _dot_nt(qs, kc)
        m = jnp.maximum(jnp.max(sw, axis=-1, keepdims=True), jnp.max(sc, axis=-1, keepdims=True))
        ew = jnp.exp(sw - m)
        ec = jnp.exp(sc - m)
        l = jnp.sum(ew, axis=-1, keepdims=True) + jnp.sum(ec, axis=-1, keepdims=True)
        o = (_dot(ew.astype(BF16), vw) + _dot(ec.astype(BF16), vc)) * (1.0 / l)
        out = jnp.zeros((GRID_W, GW), F32)
        for h in range(N_HEADS):
            out = jnp.where(lane == h, o[h * GRID_W:(h + 1) * GRID_W], out)
        o_ref[pl.ds(qrow, GRID_W), :] = out.astype(BF16)
        return carry

    lax.fori_loop(0, rb, row, 0, unroll=NA_UNROLL)


NA_ROWS_PER_STEP = 16
NA_UNROLL = 8


def _na(proj_l, proj_c, bias, b, n, c):
    rows = n // GRID_W
    rb = NA_ROWS_PER_STEP
    nrb = rows // rb
    tq = rb * GRID_W
    return pl.pallas_call(
        functools.partial(_na_kernel, rows=rows, rb=rb),
        grid=(b, nrb),
        in_specs=[pl.BlockSpec((tq, GW), lambda bi, i: (bi * nrb + i, C_NA_Q)),
                  pl.BlockSpec((n, GW), lambda bi, i: (bi, C_NA_K)),
                  pl.BlockSpec((n, GW), lambda bi, i: (bi, C_NA_V)),
                  pl.BlockSpec((c, GW), lambda bi, i: (bi, C_NA_K)),
                  pl.BlockSpec((c, GW), lambda bi, i: (bi, C_NA_V)),
                  pl.BlockSpec((NA_WIN_ROWS, N_HEADS * GRID_W, NA_WIN_ROWS * GRID_W), lambda bi, i: (0, 0, 0))],
        out_specs=pl.BlockSpec((tq, GW), lambda bi, i: (bi * nrb + i, 0)),
        out_shape=jax.ShapeDtypeStruct((b * n, GW), BF16),
        compiler_params=_cparams("arbitrary", "arbitrary"),
        name="na_attention",
    )(proj_l, proj_l, proj_l, proj_c, proj_c, bias)


def _na_bias_tables(rpb):
    nr, ncol = 2 * NA_WIN_ROWS - 1, 2 * NA_WIN_COLS - 1
    cq = np.arange(GRID_W)[:, None]
    ck = np.arange(GRID_W)[None, :]
    cstart = np.clip(cq - NA_WIN_COLS // 2, 0, GRID_W - NA_WIN_COLS)
    okc = (ck >= cstart) & (ck < cstart + NA_WIN_COLS)
    dc = np.clip(ck - cq + NA_WIN_COLS - 1, 0, ncol - 1)
    ohc = (dc[..., None] == np.arange(ncol)) & okc[..., None]
    dr = np.arange(NA_WIN_ROWS)[:, None] + np.arange(NA_WIN_ROWS)[None, :]
    ohr = dr[..., None] == np.arange(nr)
    bias = jnp.einsum('vjr,hrd,ckd->vhcjk', jnp.asarray(ohr, F32), rpb.astype(F32), jnp.asarray(ohc, F32),
                      precision=lax.Precision.HIGHEST)
    bias = jnp.where(jnp.asarray(okc)[None, None, :, None, :], bias, NEG)
    return bias.reshape(NA_WIN_ROWS, N_HEADS * GRID_W, NA_WIN_ROWS * GRID_W)


def _ctx_attn_kernel(*refs, diff, out_scale):
    if diff:
        q_ref, k_ref, v_ref, lam_ref, g_ref, o_ref = refs
    else:
        q_ref, k_ref, v_ref, o_ref = refs
    k = k_ref[...]
    v = v_ref[...]
    q32 = q_ref[...].astype(F32)
    lane = lax.broadcasted_iota(jnp.int32, q32.shape, 1)

    def softmax_pv(mk):
        s = _dot_nt(jnp.where(mk, q32, 0.0).astype(BF16), k)
        z = s - jnp.max(s, axis=-1, keepdims=True)
        e = jnp.exp2(z) if diff else jnp.exp(z)
        return _dot(e.astype(BF16), v) * (1.0 / jnp.sum(e, axis=-1, keepdims=True))

    out = jnp.zeros(q32.shape, F32)
    for h in range(N_HEADS):
        if diff:
            o = (softmax_pv(lane // (HEAD_DIM // 2) == 2 * h)
                 - lam_ref[...] * softmax_pv(lane // (HEAD_DIM // 2) == 2 * h + 1))
        else:
            o = softmax_pv(lane // HEAD_DIM == h)
        out = jnp.where(lane // HEAD_DIM == h, o, out)
    if diff:
        out = out * lax.rsqrt(_head_mean_sq(out) + EPS) * g_ref[...] * out_scale
    o_ref[...] = out.astype(BF16)


def _ctx_attn(proj_c, b, c, qcol, kcol, vcol, lam=None, g=None, out_scale=1.0):
    diff = lam is not None
    in_specs = [pl.BlockSpec((c, GW), lambda bi: (bi, qcol)),
                pl.BlockSpec((c, GW), lambda bi: (bi, kcol)),
                pl.BlockSpec((c, GW), lambda bi: (bi, vcol))]
    args = [proj_c, proj_c, proj_c]
    if diff:
        in_specs += [pl.BlockSpec((1, GW), lambda bi: (0, 0)), pl.BlockSpec((1, GW), lambda bi: (0, 0))]
        args += [lam, g]
    return pl.pallas_call(
        functools.partial(_ctx_attn_kernel, diff=diff, out_scale=out_scale),
        grid=(b,),
        in_specs=in_specs,
        out_specs=pl.BlockSpec((c, GW), lambda bi: (bi, 0)),
        out_shape=jax.ShapeDtypeStruct((b * c, GW), BF16),
        compiler_params=_cparams("arbitrary"),
        name="ctx_diff_attention" if diff else "ctx_attention",
    )(*args)


def _diff_key_tile(nk):
    return next(t for t in (1408, 768, 512, 384, 256, 128) if nk % t == 0)


V_AUG = HEAD_DIM + 16


DIFF_HEADROOM = 100.0
DIFF_MIN_SUM = 2.0 ** -100


def _diff_lat_kernel(q_ref, k_ref, vt_ref, kn_ref, lam_ref, g_ref, o_ref,
                     qm_scr, sh_scr, m_scr, acc_scr, *, tq, tk, nk, out_scale):
    nt = nk // tk
    q32 = q_ref[...].astype(F32)
    qt = q32.T
    rowg = lax.broadcasted_iota(jnp.int32, (GW, tq), 0) // MAP_DIM
    for u in range(N_MAPS):
        qm_scr[u] = jnp.where(rowg == u, qt, 0.0).astype(BF16)
    kmax = jnp.sqrt(_group_sum_sq(jnp.sqrt(kn_ref[0]), MAP_DIM))[0:1, :]
    bound = jnp.sqrt(_group_sum_sq(q32 * kmax, MAP_DIM)) * (1.0 + 2.0 ** -8)
    sh_scr[...] = bound.T - DIFF_HEADROOM
    acc_scr[...] = jnp.zeros(acc_scr.shape, F32)

    def v_tile(st, u):
        h = u // 2
        return vt_ref[0, h * V_AUG:(h + 1) * V_AUG, pl.ds(st, tk)]

    def fast(t, carry):
        st = pl.multiple_of(t * tk, tk)
        kt = k_ref[0, pl.ds(st, tk), :]
        for u in range(N_MAPS):
            s = _dot(kt, qm_scr[u])
            e = jnp.exp2(s - sh_scr[u * MAP_DIM:u * MAP_DIM + 1, :]).astype(BF16)
            acc_scr[u] = acc_scr[u] + _dot(v_tile(st, u), e)
        return carry

    lax.fori_loop(0, nt, fast, 0)

    lmin = acc_scr[0, HEAD_DIM:HEAD_DIM + 1, :]
    for u in range(1, N_MAPS):
        lmin = jnp.minimum(lmin, acc_scr[u, HEAD_DIM:HEAD_DIM + 1, :])

    @pl.when(jnp.logical_not(jnp.min(lmin) >= DIFF_MIN_SUM))
    def _():
        m_scr[...] = jnp.full(m_scr.shape, NEG, F32)
        acc_scr[...] = jnp.zeros(acc_scr.shape, F32)

        def exact(t, carry):
            st = pl.multiple_of(t * tk, tk)
            kt = k_ref[0, pl.ds(st, tk), :]
            for u in range(N_MAPS):
                s = _dot(kt, qm_scr[u])
                mo = m_scr[u:u + 1, :]
                mn = jnp.maximum(mo, jnp.max(s, axis=0, keepdims=True))
                m_scr[u:u + 1, :] = mn
                e = jnp.exp2(s - mn).astype(BF16)
                acc_scr[u] = jnp.exp2(mo - mn) * acc_scr[u] + _dot(v_tile(st, u), e)
            return carry

        lax.fori_loop(0, nt, exact, 0)

    lam = lam_ref[0:1, 0:1]
    outs = []
    for h in range(N_HEADS):
        a1 = acc_scr[2 * h]
        a2 = acc_scr[2 * h + 1]
        o1 = a1[:HEAD_DIM] * (1.0 / a1[HEAD_DIM:HEAD_DIM + 1])
        o2 = a2[:HEAD_DIM] * (1.0 / a2[HEAD_DIM:HEAD_DIM + 1])
        outs.append(o1 - lam * o2)
    out = jnp.concatenate(outs, axis=0).T
    out = out * lax.rsqrt(_head_mean_sq(out) + EPS) * g_ref[...] * out_scale
    o_ref[...] = out.astype(BF16)


def _diff_lat(proj_l, k_all, vt_aug, kn2, lam, g, b, n, tq, tk, out_scale):
    nt = n // tq
    nk = k_all.shape[1]
    assert nk % tk == 0
    return pl.pallas_call(
        functools.partial(_diff_lat_kernel, tq=tq, tk=tk, nk=nk, out_scale=out_scale),
        grid=(b, nt),
        in_specs=[pl.BlockSpec((tq, GW), lambda bi, i: (bi * nt + i, C_DF_Q)),
                  pl.BlockSpec((1, nk, GW), lambda bi, i: (bi, 0, 0)),
                  pl.BlockSpec((1, N_HEADS * V_AUG, nk), lambda bi, i: (bi, 0, 0)),
                  pl.BlockSpec((1, 8, GW), lambda bi, i: (bi, 0, 0)),
                  pl.BlockSpec((1, GW), lambda bi, i: (0, 0)),
                  pl.BlockSpec((1, GW), lambda bi, i: (0, 0))],
        out_specs=pl.BlockSpec((tq, GW), lambda bi, i: (bi * nt + i, 0)),
        out_shape=jax.ShapeDtypeStruct((b * n, GW), BF16),
        scratch_shapes=[pltpu.VMEM((N_MAPS, GW, tq), BF16), pltpu.VMEM((GW, tq), F32),
                        pltpu.VMEM((N_MAPS, tq), F32), pltpu.VMEM((N_MAPS, V_AUG, tq), F32)],
        compiler_params=_cparams("arbitrary", "arbitrary"),
        name="diff_attention_lat",
    )(proj_l, k_all, vt_aug, kn2, lam, g)


def _out_proj_kernel(x_ref, ya_ref, hf_ref, hb_ref, gl_ref, of_ref, ob_ref, gr_ref, yd_ref, w_ref, g_ref, gate_ref,
                     o_ref):
    gl = gl_ref[...].astype(F32)
    gelu = 0.5 * gl * (1.0 + jnp.tanh(math.sqrt(2.0 / math.pi) * (gl + 0.044715 * (gl * gl * gl))))
    yb = (hf_ref[...].astype(F32) + hb_ref[...].astype(F32)) * gelu
    r = of_ref[...].astype(F32) + ob_ref[...].astype(F32)
    gr = gr_ref[...].astype(F32)
    yc = r * lax.rsqrt(_head_mean_sq(r) + EPS) * (gr * _sigmoid(gr))
    y = (_dot(ya_ref[...], w_ref[0:GW, :]) + _dot(yb.astype(BF16), w_ref[GW:2 * GW, :])
         + _dot(yc.astype(BF16), w_ref[2 * GW:3 * GW, :]) + _dot(yd_ref[...], w_ref[3 * GW:4 * GW, :]))
    o_ref[...] = x_ref[...] + gate_ref[0] * _rms(y, g_ref[...])


def _out_proj(x2d, ya, hf, hb, of, ob, yd, proj, w, g, gate, b, n, tm):
    nt = n // tm
    tok = lambda col: pl.BlockSpec((tm, GW), lambda bi, i: (bi * nt + i, col))
    return pl.pallas_call(
        _out_proj_kernel,
        grid=(b, nt),
        in_specs=[pl.BlockSpec((tm, D_MODEL), lambda bi, i: (bi * nt + i, 0)),
                  tok(0), tok(0), tok(0), tok(C_LRU_G), tok(0), tok(0), tok(C_RET_G), tok(0),
                  pl.BlockSpec((D_MODEL, D_MODEL), lambda bi, i: (0, 0)),
                  pl.BlockSpec((1, D_MODEL), lambda bi, i: (0, 0)),
                  pl.BlockSpec((1, 1, D_MODEL), lambda bi, i: (bi, 0, 0))],
        out_specs=pl.BlockSpec((tm, D_MODEL), lambda bi, i: (bi * nt + i, 0)),
        out_shape=jax.ShapeDtypeStruct((b * n, D_MODEL), F32),
        compiler_params=_cparams("arbitrary", "arbitrary"),
        name="out_proj",
    )(x2d, ya, hf, hb, proj, of, ob, proj, yd, w, g, gate)


FFN_CHUNK = 256


def _ffn_kernel(x_ref, g1_ref, sh_ref, sc_ref, w1_ref, w2_ref, g2_ref, gate_ref, o_ref):
    x = x_ref[...]
    h = (_rms(x, g1_ref[...]) * (1.0 + sc_ref[0]) + sh_ref[0]).astype(BF16)
    acc = jnp.zeros(x.shape, F32)
    for c in range(D_FF // FFN_CHUNK):
        lo, hi = c * FFN_CHUNK, (c + 1) * FFN_CHUNK
        gt = _dot(h, w1_ref[:, lo:hi])
        up = _dot(h, w1_ref[:, D_FF + lo:D_FF + hi])
        act = (gt * _sigmoid(gt) * up).astype(BF16)
        acc = acc + _dot(act, w2_ref[lo:hi, :])
    o_ref[...] = x + gate_ref[0] * _rms(acc, g2_ref[...])


def _ffn(x2d, g1, shift, scale, w1, w2, g2, gate, b, n, tm):
    nt = n // tm
    vec = pl.BlockSpec((1, D_MODEL), lambda bi, i: (0, 0))
    per_b = pl.BlockSpec((1, 1, D_MODEL), lambda bi, i: (bi, 0, 0))
    return pl.pallas_call(
        _ffn_kernel,
        grid=(b, nt),
        in_specs=[pl.BlockSpec((tm, D_MODEL), lambda bi, i: (bi * nt + i, 0)), vec, per_b, per_b,
                  pl.BlockSpec((D_MODEL, 2 * D_FF), lambda bi, i: (0, 0)),
                  pl.BlockSpec((D_FF, D_MODEL), lambda bi, i: (0, 0)), vec, per_b],
        out_specs=pl.BlockSpec((tm, D_MODEL), lambda bi, i: (bi * nt + i, 0)),
        out_shape=jax.ShapeDtypeStruct((b * n, D_MODEL), F32),
        compiler_params=_cparams("arbitrary", "arbitrary"),
        name="ffn",
    )(x2d, g1, shift, scale, w1, w2, g2, gate)


def _blockdiag(w):
    nb, bs, _ = w.shape
    return jnp.einsum('kcd,kj->kcjd', w, jnp.eye(nb, dtype=w.dtype)).reshape(nb * bs, nb * bs)


def _lru_params(conv_w, conv_b, gate_w, gate_b, lam):
    wg = jnp.stack([jnp.concatenate([_blockdiag(gate_w[d, 0]), _blockdiag(gate_w[d, 1])], axis=1)
                    for d in range(2)]).astype(BF16)
    bg = jnp.stack([jnp.concatenate([gate_b[d, 0].reshape(1, GW), gate_b[d, 1].reshape(1, GW)], axis=1)
                    for d in range(2)])
    return conv_w, conv_b.reshape(1, GW), wg, bg, lam.reshape(2, 1, GW)


def kernel(x, c, ctx, c_ctx, w_mod, b_mod, g_pre_mix, g_post_mix, g_pre_ffn, g_post_ffn, w_in, na_rpb, lru_conv_w,
           lru_conv_b, lru_gate_w, lru_gate_b, lru_lambda, ret_decay, diff_lambda, diff_subln, w_out, w_ffn_in,
           w_ffn_out):
    b, n, _ = x.shape
    nc = ctx.shape[1]
    depth = w_mod.shape[0]
    assert n % 1024 == 0 and nc % RET_CHUNK == 0

    r = -(-(b + 1) // 8) * 8
    cc = jnp.zeros((r, D_MODEL), F32).at[:b].set(c).at[b].set(c_ctx)
    mod = _mod(cc, w_mod, b_mod)
    tabs = _rope_tables(n)

    xl = x.reshape(b * n, D_MODEL)
    xc = ctx.reshape(b * nc, D_MODEL)
    tm = 512
    vec = lambda a: a.reshape(1, D_MODEL)
    for l in range(depth):
        last = l == depth - 1
        lam_init = 0.8 - 0.6 * math.exp(-0.3 * l)
        ml = [mod[l, :b, k * D_MODEL:(k + 1) * D_MODEL].reshape(b, 1, D_MODEL) for k in range(6)]
        mc = [jnp.broadcast_to(mod[l, b, k * D_MODEL:(k + 1) * D_MODEL], (b, 1, D_MODEL)) for k in range(6)]
        w_in_l = w_in[l].astype(BF16)
        w_out_l = w_out[l].astype(BF16)
        w1_l = w_ffn_in[l].astype(BF16)
        w2_l = w_ffn_out[l].astype(BF16)

        pc, vt_c, kn_c = _in_proj(xc, vec(g_pre_mix[l]), mc[0], mc[1], w_in_l, None, b, nc, nc)
        pt, vt_l, kn_l = _in_proj(xl, vec(g_pre_mix[l]), ml[0], ml[1], w_in_l, tabs, b, n, tm)

        lru_p = _lru_params(lru_conv_w[l], lru_conv_b[l], lru_gate_w[l], lru_gate_b[l], lru_lambda[l])
        hf_c, hb_c, hfin = _lru(pc, b, nc, nc, *lru_p, jnp.zeros((b, 8, GW), F32))
        hf_l, hb_l, _ = _lru(pt, b, n, 1024, *lru_p, hfin)

        log_g = jax.nn.log_sigmoid(ret_decay[l].astype(F32))
        lgl = jnp.repeat(log_g, HEAD_DIM, axis=-1).reshape(2, 1, GW)
        lgh = jnp.broadcast_to(log_g.reshape(2 * N_HEADS, 1), (2 * N_HEADS, 128))
        of_c, ob_c, sfin = _ret(pc, b, nc, lgl, lgh, jnp.zeros((b, 2, GW, GW), F32))
        of_l, ob_l, _ = _ret(pt, b, n, lgl, lgh, sfin)

        ya_l = _na(pt, pc, _na_bias_tables(na_rpb[l]), b, n, nc)

        lq1, lk1, lq2, lk2 = diff_lambda[l].astype(F32)
        lam = jnp.exp(jnp.sum(lq1 * lk1)) - jnp.exp(jnp.sum(lq2 * lk2)) + lam_init
        lam_v = jnp.broadcast_to(lam, (1, GW)).astype(F32)
        g_sub = jnp.tile(diff_subln[l].astype(F32), N_HEADS).reshape(1, GW)
        k_all = jnp.concatenate([pc[:, C_DF_K * GW:(C_DF_K + 1) * GW].reshape(b, nc, GW),
                                 pt[:, C_DF_K * GW:(C_DF_K + 1) * GW].reshape(b, n, GW)], axis=1)
        vt_all = jnp.concatenate([vt_c, vt_l], axis=2).reshape(b, N_HEADS, HEAD_DIM, nc + n)
        vt_aug = jnp.concatenate([vt_all, jnp.ones((b, N_HEADS, V_AUG - HEAD_DIM, nc + n), BF16)],
                                 axis=2).reshape(b, N_HEADS * V_AUG, nc + n)
        yd_l = _diff_lat(pt, k_all, vt_aug, jnp.maximum(kn_c, kn_l), lam_v, g_sub, b, n, 512,
                         _diff_key_tile(nc + n), 1.0 - lam_init)

        x_mid = _out_proj(xl, ya_l, hf_l, hb_l, of_l, ob_l, yd_l, pt, w_out_l, vec(g_post_mix[l]), ml[2], b, n, tm)
        xl_new = _ffn(x_mid, vec(g_pre_ffn[l]), ml[3], ml[4], w1_l, w2_l, vec(g_post_ffn[l]), ml[5], b, n, tm)

        if not last:
            ya_c = _ctx_attn(pc, b, nc, C_NA_Q, C_NA_K, C_NA_V)
            yd_c = _ctx_attn(pc, b, nc, C_DF_Q, C_DF_K, C_DF_V, lam_v, g_sub, 1.0 - lam_init)
            xc_mid = _out_proj(xc, ya_c, hf_c, hb_c, of_c, ob_c, yd_c, pc, w_out_l, vec(g_post_mix[l]), mc[2],
                               b, nc, nc)
            xc = _ffn(xc_mid, vec(g_pre_ffn[l]), mc[3], mc[4], w1_l, w2_l, vec(g_post_ffn[l]), mc[5], b, nc, nc)
        xl = xl_new
    return xl.reshape(b, n, D_MODEL)
```

```python
import functools
import math

import numpy as np
import jax
import jax.numpy as jnp
from jax import lax
from jax.experimental import pallas as pl
from jax.experimental.pallas import tpu as pltpu

F32 = jnp.float32
BF16 = jnp.bfloat16

D_MODEL = 1024
GRID_W = 64
HEAD_DIM = 64
N_HEADS = 4
GW = N_HEADS * HEAD_DIM
N_PROJ = 12
N_MAPS = 2 * N_HEADS
MAP_DIM = HEAD_DIM // 2
V_AUG = HEAD_DIM + 16
D_FF = 2816
NA_WIN_ROWS = 8
NA_WIN_COLS = 16
LRU_C = 8.0
RET_CHUNK = 128
ROPE_BASE = 10000.0
EPS = 1e-6
NEG = -1e30

C_NA_Q, C_NA_K, C_NA_V, C_LRU_X, C_LRU_G, C_RET_Q, C_RET_K, C_RET_V, C_RET_G, C_DF_Q, C_DF_K, C_DF_V = range(12)
_COL_SCALE = {C_NA_Q: HEAD_DIM ** -0.5, C_RET_K: HEAD_DIM ** -0.5,
              C_DF_Q: (HEAD_DIM // 2) ** -0.5 * math.log2(math.e)}

LRU_HALO = 16
V7X_VMEM_LIMIT = 56 * 1024 * 1024


def _cparams(*sem):
    return pltpu.CompilerParams(dimension_semantics=sem, vmem_limit_bytes=V7X_VMEM_LIMIT)


def _dot(a, b):
    return jnp.dot(a, b, preferred_element_type=F32)


def _dot_nt(a, b):
    return lax.dot_general(a, b, (((1,), (1,)), ((), ())), preferred_element_type=F32)


def _rms(x, g):
    return x * lax.rsqrt(jnp.mean(x * x, axis=-1, keepdims=True) + EPS) * g


def _sigmoid(x):
    return 1.0 / (1.0 + jnp.exp(-x))


def _group_sum_sq(y, group):
    y2 = y * y
    hi = y2.astype(BF16)
    lo = (y2 - hi.astype(F32)).astype(BF16)
    r = lax.broadcasted_iota(jnp.int32, (GW, GW), 0) // group
    c = lax.broadcasted_iota(jnp.int32, (GW, GW), 1) // group
    bd = jnp.where(r == c, 1.0, 0.0).astype(BF16)
    return _dot(hi, bd) + _dot(lo, bd)


def _head_mean_sq(y):
    return _group_sum_sq(y, HEAD_DIM) * (1.0 / HEAD_DIM)


def _mod_kernel(c_ref, w_ref, b_ref, o_ref):
    c = c_ref[...]
    s = c * _sigmoid(c)
    o_ref[0] = jnp.dot(s, w_ref[0], preferred_element_type=F32, precision=lax.Precision.HIGHEST) + b_ref[0]


def _mod(cc, w_mod, b_mod):
    depth = w_mod.shape[0]
    r = cc.shape[0]
    tn = 1536
    return pl.pallas_call(
        _mod_kernel,
        grid=(depth, 6 * D_MODEL // tn),
        in_specs=[pl.BlockSpec((r, D_MODEL), lambda l, j: (0, 0)),
                  pl.BlockSpec((1, D_MODEL, tn), lambda l, j: (l, 0, j)),
                  pl.BlockSpec((1, 1, tn), lambda l, j: (l, 0, j))],
        out_specs=pl.BlockSpec((1, r, tn), lambda l, j: (l, 0, j)),
        out_shape=jax.ShapeDtypeStruct((depth, r, 6 * D_MODEL), F32),
        compiler_params=_cparams("arbitrary", "arbitrary"),
        name="adaln_mod",
    )(cc, w_mod, b_mod.reshape(depth, 1, 6 * D_MODEL))


def _rope(p, cos, sin_signed, half):
    outs = []
    for c in range(GW // 128):
        xs = p[:, c * 128:(c + 1) * 128]
        lane = lax.broadcasted_iota(jnp.int32, xs.shape, 1)
        first = (lane % (2 * half)) < half
        partner = jnp.where(first, pltpu.roll(xs, 128 - half, 1), pltpu.roll(xs, half, 1))
        outs.append(xs * cos[:, c * 128:(c + 1) * 128] + partner * sin_signed[:, c * 128:(c + 1) * 128])
    return jnp.concatenate(outs, axis=1)


def _in_proj_kernel(*refs, rope):
    if rope:
        x_ref, g_ref, sh_ref, sc_ref, w_ref, tab_ref, o_ref, vt_ref, kn_ref = refs
    else:
        x_ref, g_ref, sh_ref, sc_ref, w_ref, o_ref, vt_ref, kn_ref = refs

    @pl.when(pl.program_id(1) == 0)
    def _():
        kn_ref[...] = jnp.zeros(kn_ref.shape, F32)

    h = _rms(x_ref[...], g_ref[...]) * (1.0 + sc_ref[0]) + sh_ref[0]
    hb = h.astype(BF16)
    for j in range(N_PROJ):
        p = _dot(hb, w_ref[:, j * GW:(j + 1) * GW])
        if j in _COL_SCALE:
            p = p * _COL_SCALE[j]
        if rope and j in (C_RET_Q, C_RET_K):
            p = _rope(p, tab_ref[0], tab_ref[1], HEAD_DIM // 2)
        if rope and j in (C_DF_Q, C_DF_K):
            p = _rope(p, tab_ref[2], tab_ref[3], HEAD_DIM // 4)
        pb = p.astype(BF16)
        o_ref[:, j * GW:(j + 1) * GW] = pb
        if j == C_DF_K:
            kf = pb.astype(F32)
            kn_ref[0] = jnp.maximum(kn_ref[0], jnp.broadcast_to(jnp.max(kf * kf, axis=0, keepdims=True), (8, GW)))
        if j == C_DF_V:
            vt = p.T.astype(BF16)
            ones = jnp.ones((V_AUG - HEAD_DIM, vt.shape[1]), BF16)
            for hd in range(N_HEADS):
                vt_ref[0, hd * V_AUG:hd * V_AUG + HEAD_DIM, :] = vt[hd * HEAD_DIM:(hd + 1) * HEAD_DIM]
                vt_ref[0, hd * V_AUG + HEAD_DIM:(hd + 1) * V_AUG, :] = ones


def _in_proj(x2d, g, shift, scale, w, tabs, b, n, tm):
    nt = n // tm
    rope = tabs is not None
    in_specs = [pl.BlockSpec((tm, D_MODEL), lambda bi, i: (bi * nt + i, 0)),
                pl.BlockSpec((1, D_MODEL), lambda bi, i: (0, 0)),
                pl.BlockSpec((1, 1, D_MODEL), lambda bi, i: (bi, 0, 0)),
                pl.BlockSpec((1, 1, D_MODEL), lambda bi, i: (bi, 0, 0)),
                pl.BlockSpec((D_MODEL, N_PROJ * GW), lambda bi, i: (0, 0))]
    args = [x2d, g, shift, scale, w]
    if rope:
        in_specs.append(pl.BlockSpec((4, tm, GW), lambda bi, i: (0, i, 0)))
        args.append(tabs)
    return pl.pallas_call(
        functools.partial(_in_proj_kernel, rope=rope),
        grid=(b, nt),
        in_specs=in_specs,
        out_specs=[pl.BlockSpec((tm, N_PROJ * GW), lambda bi, i: (bi * nt + i, 0)),
                   pl.BlockSpec((1, N_HEADS * V_AUG, tm), lambda bi, i: (bi, 0, i)),
                   pl.BlockSpec((1, 8, GW), lambda bi, i: (bi, 0, 0))],
        out_shape=[jax.ShapeDtypeStruct((b * n, N_PROJ * GW), BF16),
                   jax.ShapeDtypeStruct((b, N_HEADS * V_AUG, n), BF16),
                   jax.ShapeDtypeStruct((b, 8, GW), F32)],
        compiler_params=_cparams("arbitrary", "arbitrary"),
        name="in_proj_rope" if rope else "in_proj",
    )(*args)


def _rope_tables(n):
    t = jnp.arange(n)
    row = (t // GRID_W).astype(F32)
    col = (t % GRID_W).astype(F32)

    def tab(dim):
        nf = dim // 4
        inv = ROPE_BASE ** (-jnp.arange(nf, dtype=F32) / nf)
        ang = jnp.concatenate([row[:, None] * inv, col[:, None] * inv], axis=-1)
        cos, sin = jnp.cos(ang), jnp.sin(ang)
        reps = GW // dim
        return (jnp.tile(jnp.concatenate([cos, cos], axis=-1), (1, reps)),
                jnp.tile(jnp.concatenate([-sin, sin], axis=-1), (1, reps)))

    cr, sr = tab(HEAD_DIM)
    cd, sd = tab(HEAD_DIM // 2)
    return jnp.stack([cr, sr, cd, sd])


def _lru_kernel(xf_ref, xfp_ref, xfn_ref, xb_ref, xbp_ref, xbn_ref, cw_ref, cb_ref, wg_ref, bg_ref, lam_ref,
                h0_ref, hf_ref, hb_ref, hfin_ref, a_scr, u_scr, hc_scr, *, tn, nt):
    i = pl.program_id(1)

    @pl.when(i == 0)
    def _():
        hc_scr[...] = h0_ref[0]

    cw = cw_ref[...]
    row = lax.broadcasted_iota(jnp.int32, (tn, GW), 0)

    def coeffs(x_ref, xp_ref, xn_ref, tile, d):
        xm = x_ref[...].astype(F32)
        prev = jnp.where(tile > 0, xp_ref[LRU_HALO - 1:LRU_HALO, :].astype(F32), 0.0)
        nxt = jnp.where(tile < nt - 1, xn_ref[0:2, :].astype(F32), 0.0)
        xm1 = jnp.where(row == 0, prev, pltpu.roll(xm, 1, 0))
        xp1 = jnp.where(row == tn - 1, nxt[0:1], pltpu.roll(xm, tn - 1, 0))
        xp2 = pltpu.roll(xm, tn - 2, 0)
        xp2 = jnp.where(row == tn - 2, nxt[0:1], xp2)
        xp2 = jnp.where(row == tn - 1, nxt[1:2], xp2)
        xb = cw[0:1] * xm1 + cw[1:2] * xm + cw[2:3] * xp1 + cw[3:4] * xp2 + cb_ref[...]
        gates = _sigmoid(_dot(xb.astype(BF16), wg_ref[d]) + bg_ref[d])
        r = gates[:, :GW]
        ig = gates[:, GW:]
        nl = -lam_ref[d]
        softplus = jnp.maximum(nl, 0.0) + jnp.log1p(jnp.exp(-jnp.abs(nl)))
        log_a = -LRU_C * r * softplus
        a_scr[d] = jnp.exp(log_a)
        th = jnp.tanh(log_a)
        u_scr[d] = jnp.sqrt(-2.0 * th / (1.0 - th)) * (ig * xb)

    coeffs(xf_ref, xfp_ref, xfn_ref, i, 0)
    coeffs(xb_ref, xbp_ref, xbn_ref, nt - 1 - i, 1)

    gs = LRU_HALO
    ng = tn // gs

    def body(g, carry):
        hf, hb = carry
        fb = pl.multiple_of(g * gs, gs)
        bb = pl.multiple_of((ng - 1 - g) * gs, gs)
        af = a_scr[0, pl.ds(fb, gs), :]
        uf = u_scr[0, pl.ds(fb, gs), :]
        ab = a_scr[1, pl.ds(bb, gs), :]
        ub = u_scr[1, pl.ds(bb, gs), :]
        frows = []
        brows = [None] * gs
        for j in range(gs):
            hf = af[j:j + 1] * hf + uf[j:j + 1]
            frows.append(hf)
            jb = gs - 1 - j
            hb = ab[jb:jb + 1] * hb + ub[jb:jb + 1]
            brows[jb] = hb
        hf_ref[pl.ds(fb, gs), :] = jnp.concatenate(frows, axis=0).astype(BF16)
        hb_ref[pl.ds(bb, gs), :] = jnp.concatenate(brows, axis=0).astype(BF16)
        return hf, hb

    hf, hb = lax.fori_loop(0, ng, body, (hc_scr[0:1, :], hc_scr[1:2, :]))
    hc_scr[0:1, :] = hf
    hc_scr[1:2, :] = hb

    @pl.when(i == nt - 1)
    def _():
        hfin_ref[0] = jnp.concatenate([hf, hb, jnp.zeros((6, GW), F32)], axis=0)


def _lru(proj, b, n, tn, cw, cb, wg, bg, lam, h0):
    nt = n // tn
    hb8 = tn // LRU_HALO

    def main(rev):
        return pl.BlockSpec((tn, GW), lambda bi, i: (bi * nt + (nt - 1 - i if rev else i), C_LRU_X))

    def prev(rev):
        def im(bi, i):
            t = nt - 1 - i if rev else i
            return (jnp.maximum((bi * nt + t) * hb8 - 1, 0), C_LRU_X)
        return pl.BlockSpec((LRU_HALO, GW), im)

    def nxt(rev):
        def im(bi, i):
            t = nt - 1 - i if rev else i
            return (jnp.minimum((bi * nt + t + 1) * hb8, b * nt * hb8 - 1), C_LRU_X)
        return pl.BlockSpec((LRU_HALO, GW), im)

    const2 = lambda bi, i: (0, 0)
    const3 = lambda bi, i: (0, 0, 0)
    return pl.pallas_call(
        functools.partial(_lru_kernel, tn=tn, nt=nt),
        grid=(b, nt),
        in_specs=[main(False), prev(False), nxt(False), main(True), prev(True), nxt(True),
                  pl.BlockSpec((4, GW), const2), pl.BlockSpec((1, GW), const2),
                  pl.BlockSpec((2, GW, 2 * GW), const3), pl.BlockSpec((2, 1, 2 * GW), const3),
                  pl.BlockSpec((2, 1, GW), const3),
                  pl.BlockSpec((1, 8, GW), lambda bi, i: (bi, 0, 0))],
        out_specs=[pl.BlockSpec((tn, GW), lambda bi, i: (bi * nt + i, 0)),
                   pl.BlockSpec((tn, GW), lambda bi, i: (bi * nt + nt - 1 - i, 0)),
                   pl.BlockSpec((1, 8, GW), lambda bi, i: (bi, 0, 0))],
        out_shape=[jax.ShapeDtypeStruct((b * n, GW), BF16), jax.ShapeDtypeStruct((b * n, GW), BF16),
                   jax.ShapeDtypeStruct((b, 8, GW), F32)],
        scratch_shapes=[pltpu.VMEM((2, tn, GW), F32), pltpu.VMEM((2, tn, GW), F32), pltpu.VMEM((8, GW), F32)],
        compiler_params=_cparams("arbitrary", "arbitrary"),
        name="rglru",
    )(proj, proj, proj, proj, proj, proj, cw, cb, wg, bg, lam, h0)


def _ret_kernel(qf_ref, kf_ref, vf_ref, qb_ref, kb_ref, vb_ref, lgl_ref, lgh_ref, s0_ref,
                of_ref, ob_ref, sfin_ref, s_scr, dm_scr, qd_scr, kd_scr, *, nc, cps):
    c = pl.program_id(1)
    cs = RET_CHUNK

    @pl.when(c == 0)
    def _():
        s_scr[...] = s0_ref[0]
        ii = lax.broadcasted_iota(jnp.int32, (cs, cs), 0)
        jj = lax.broadcasted_iota(jnp.int32, (cs, cs), 1)
        t = lax.broadcasted_iota(jnp.int32, (cs, GW), 0).astype(F32)
        for d in range(2):
            rel = ii - jj if d == 0 else jj - ii - 1
            ok = rel >= 0
            relf = jnp.where(ok, rel, 0).astype(F32)
            for h in range(N_HEADS):
                lg = lgh_ref[d * N_HEADS + h:d * N_HEADS + h + 1, :]
                dm_scr[d, h] = jnp.where(ok, jnp.exp(lg * relf), 0.0)
            lgl = lgl_ref[d]
            if d == 0:
                qd_scr[d] = jnp.exp(lgl * (t + 1.0))
                kd_scr[d] = jnp.exp(lgl * (cs - 1.0 - t))
            else:
                qd_scr[d] = jnp.exp(lgl * (cs - 1.0 - t))
                kd_scr[d] = jnp.exp(lgl * t)

    lane = lax.broadcasted_iota(jnp.int32, (cs, GW), 1) // HEAD_DIM
    br = lax.broadcasted_iota(jnp.int32, (GW, GW), 0) // HEAD_DIM
    bc = lax.broadcasted_iota(jnp.int32, (GW, GW), 1) // HEAD_DIM

    def direction(d, q_ref, k_ref, v_ref, o_ref):
        s = s_scr[d]
        cdec = jnp.exp(lgl_ref[d] * float(cs))
        for step in range(cps):
            j = step if d == 0 else cps - 1 - step
            rs = slice(j * cs, (j + 1) * cs)
            k = k_ref[rs, :]
            v = v_ref[rs, :]
            q32 = q_ref[rs, :].astype(F32)
            v32 = v.astype(F32)
            parts = []
            vparts = []
            for h in range(N_HEADS):
                mk = lane == h
                qh = jnp.where(mk, q32, 0.0).astype(BF16)
                parts.append((_dot_nt(qh, k) * dm_scr[d, h]).astype(BF16))
                vparts.append(jnp.where(mk, v32, 0.0).astype(BF16))
            p = jnp.concatenate(parts, axis=1)
            vs = jnp.concatenate(vparts, axis=0)
            o = _dot(p, vs) + _dot((q32 * qd_scr[d]).astype(BF16), s.astype(BF16))
            o_ref[rs, :] = o.astype(BF16)
            kd = (k.astype(F32) * kd_scr[d]).T.astype(BF16)
            s = s * cdec + jnp.where(br == bc, _dot(kd, v), 0.0)
        s_scr[d] = s

    direction(0, qf_ref, kf_ref, vf_ref, of_ref)
    direction(1, qb_ref, kb_ref, vb_ref, ob_ref)

    @pl.when(c == nc - 1)
    def _():
        sfin_ref[0] = s_scr[...]


RET_CHUNKS_PER_STEP = 8


def _ret(proj, b, n, lgl, lgh, s0):
    cps = min(RET_CHUNKS_PER_STEP, n // RET_CHUNK)
    tb = cps * RET_CHUNK
    nc = n // tb

    def blk(col, rev):
        return pl.BlockSpec((tb, GW), lambda bi, c: (bi * nc + (nc - 1 - c if rev else c), col))

    return pl.pallas_call(
        functools.partial(_ret_kernel, nc=nc, cps=cps),
        grid=(b, nc),
        in_specs=[blk(C_RET_Q, False), blk(C_RET_K, False), blk(C_RET_V, False),
                  blk(C_RET_Q, True), blk(C_RET_K, True), blk(C_RET_V, True),
                  pl.BlockSpec((2, 1, GW), lambda bi, c: (0, 0, 0)),
                  pl.BlockSpec((2 * N_HEADS, 128), lambda bi, c: (0, 0)),
                  pl.BlockSpec((1, 2, GW, GW), lambda bi, c: (bi, 0, 0, 0))],
        out_specs=[pl.BlockSpec((tb, GW), lambda bi, c: (bi * nc + c, 0)),
                   pl.BlockSpec((tb, GW), lambda bi, c: (bi * nc + nc - 1 - c, 0)),
                   pl.BlockSpec((1, 2, GW, GW), lambda bi, c: (bi, 0, 0, 0))],
        out_shape=[jax.ShapeDtypeStruct((b * n, GW), BF16), jax.ShapeDtypeStruct((b * n, GW), BF16),
                   jax.ShapeDtypeStruct((b, 2, GW, GW), F32)],
        scratch_shapes=[pltpu.VMEM((2, GW, GW), F32), pltpu.VMEM((2, N_HEADS, RET_CHUNK, RET_CHUNK), F32),
                        pltpu.VMEM((2, RET_CHUNK, GW), F32), pltpu.VMEM((2, RET_CHUNK, GW), F32)],
        compiler_params=_cparams("arbitrary", "arbitrary"),
        name="retention",
    )(proj, proj, proj, proj, proj, proj, lgl, lgh, s0)


def _na_kernel(q_ref, k_ref, v_ref, kc_ref, vc_ref, bias_ref, o_ref, *, rows, rb):
    i0 = pl.program_id(1) * rb
    nwin = NA_WIN_ROWS * GRID_W
    kc = kc_ref[...]
    vc = vc_ref[...]
    lane = lax.broadcasted_iota(jnp.int32, (GRID_W, GW), 1) // HEAD_DIM

    def row(i, carry):
        r = i0 + i
        r0 = jnp.clip(r - NA_WIN_ROWS // 2, 0, rows - NA_WIN_ROWS)
        start = pl.multiple_of(r0 * GRID_W, GRID_W)
        qrow = pl.multiple_of(i * GRID_W, GRID_W)
        kw = k_ref[pl.ds(start, nwin), :]
        vw = v_ref[pl.ds(start, nwin), :]
        q32 = q_ref[pl.ds(qrow, GRID_W), :].astype(F32)
        qs = jnp.concatenate([jnp.where(lane == h, q32, 0.0) for h in range(N_HEADS)], axis=0).astype(BF16)
        sw = _dot_nt(qs, kw) + bias_ref[r0 - r + NA_WIN_ROWS - 1]
        sc = _dot_nt(qs, kc)
        m = jnp.maximum(jnp.max(sw, axis=-1, keepdims=True), jnp.max(sc, axis=-1, keepdims=True))
        ew = jnp.exp(sw - m)
        ec = jnp.exp(sc - m)
        l = jnp.sum(ew, axis=-1, keepdims=True) + jnp.sum(ec, axis=-1, keepdims=True)
        o = (_dot(ew.astype(BF16), vw) + _dot(ec.astype(BF16), vc)) * (1.0 / l)
        out = jnp.zeros((GRID_W, GW), F32)
        for h in range(N_HEADS):
            out = jnp.where(lane == h, o[h * GRID_W:(h + 1) * GRID_W], out)
        o_ref[pl.ds(qrow, GRID_W), :] = out.astype(BF16)
        return carry

    lax.fori_loop(0, rb, row, 0, unroll=NA_UNROLL)


NA_ROWS_PER_STEP = 16
NA_UNROLL = 16


def _na(proj_l, proj_c, bias, b, n, c):
    rows = n // GRID_W
    rb = NA_ROWS_PER_STEP
    nrb = rows // rb
    tq = rb * GRID_W
    return pl.pallas_call(
        functools.partial(_na_kernel, rows=rows, rb=rb),
        grid=(b, nrb),
        in_specs=[pl.BlockSpec((tq, GW), lambda bi, i: (bi * nrb + i, C_NA_Q)),
                  pl.BlockSpec((n, GW), lambda bi, i: (bi, C_NA_K)),
                  pl.BlockSpec((n, GW), lambda bi, i: (bi, C_NA_V)),
                  pl.BlockSpec((c, GW), lambda bi, i: (bi, C_NA_K)),
                  pl.BlockSpec((c, GW), lambda bi, i: (bi, C_NA_V)),
                  pl.BlockSpec((NA_WIN_ROWS, N_HEADS * GRID_W, NA_WIN_ROWS * GRID_W), lambda bi, i: (0, 0, 0))],
        out_specs=pl.BlockSpec((tq, GW), lambda bi, i: (bi * nrb + i, 0)),
        out_shape=jax.ShapeDtypeStruct((b * n, GW), BF16),
        compiler_params=_cparams("arbitrary", "arbitrary"),
        name="na_attention",
    )(proj_l, proj_l, proj_l, proj_c, proj_c, bias)


def _na_bias_tables(rpb):
    nr, ncol = 2 * NA_WIN_ROWS - 1, 2 * NA_WIN_COLS - 1
    cq = np.arange(GRID_W)[:, None]
    ck = np.arange(GRID_W)[None, :]
    cstart = np.clip(cq - NA_WIN_COLS // 2, 0, GRID_W - NA_WIN_COLS)
    okc = (ck >= cstart) & (ck < cstart + NA_WIN_COLS)
    dc = np.clip(ck - cq + NA_WIN_COLS - 1, 0, ncol - 1)
    ohc = (dc[..., None] == np.arange(ncol)) & okc[..., None]
    dr = np.arange(NA_WIN_ROWS)[:, None] + np.arange(NA_WIN_ROWS)[None, :]
    ohr = dr[..., None] == np.arange(nr)
    bias = jnp.einsum('vjr,hrd,ckd->vhcjk', jnp.asarray(ohr, F32), rpb.astype(F32), jnp.asarray(ohc, F32),
                      precision=lax.Precision.HIGHEST)
    bias = jnp.where(jnp.asarray(okc)[None, None, :, None, :], bias, NEG)
    return bias.reshape(NA_WIN_ROWS, N_HEADS * GRID_W, NA_WIN_ROWS * GRID_W)


def _ctx_attn_kernel(*refs, diff, out_scale):
    if diff:
        q_ref, k_ref, v_ref, lam_ref, g_ref, o_ref = refs
    else:
        q_ref, k_ref, v_ref, o_ref = refs
    k = k_ref[...]
    v = v_ref[...]
    q32 = q_ref[...].astype(F32)
    lane = lax.broadcasted_iota(jnp.int32, q32.shape, 1)

    def softmax_pv(mk):
        s = _dot_nt(jnp.where(mk, q32, 0.0).astype(BF16), k)
        z = s - jnp.max(s, axis=-1, keepdims=True)
        e = jnp.exp2(z) if diff else jnp.exp(z)
        return _dot(e.astype(BF16), v) * (1.0 / jnp.sum(e, axis=-1, keepdims=True))

    out = jnp.zeros(q32.shape, F32)
    for h in range(N_HEADS):
        if diff:
            o = (softmax_pv(lane // (HEAD_DIM // 2) == 2 * h)
                 - lam_ref[...] * softmax_pv(lane // (HEAD_DIM // 2) == 2 * h + 1))
        else:
            o = softmax_pv(lane // HEAD_DIM == h)
        out = jnp.where(lane // HEAD_DIM == h, o, out)
    if diff:
        out = out * lax.rsqrt(_head_mean_sq(out) + EPS) * g_ref[...] * out_scale
    o_ref[...] = out.astype(BF16)


def _ctx_attn(proj_c, b, c, qcol, kcol, vcol, lam=None, g=None, out_scale=1.0):
    diff = lam is not None
    in_specs = [pl.BlockSpec((c, GW), lambda bi: (bi, qcol)),
                pl.BlockSpec((c, GW), lambda bi: (bi, kcol)),
                pl.BlockSpec((c, GW), lambda bi: (bi, vcol))]
    args = [proj_c, proj_c, proj_c]
    if diff:
        in_specs += [pl.BlockSpec((1, GW), lambda bi: (0, 0)), pl.BlockSpec((1, GW), lambda bi: (0, 0))]
        args += [lam, g]
    return pl.pallas_call(
        functools.partial(_ctx_attn_kernel, diff=diff, out_scale=out_scale),
        grid=(b,),
        in_specs=in_specs,
        out_specs=pl.BlockSpec((c, GW), lambda bi: (bi, 0)),
        out_shape=jax.ShapeDtypeStruct((b * c, GW), BF16),
        compiler_params=_cparams("arbitrary"),
        name="ctx_diff_attention" if diff else "ctx_attention",
    )(*args)


DIFF_KEY_TILES = (2816, 768, 256)
DIFF_HEADROOM = 100.0
DIFF_MIN_SUM = 2.0 ** -100


def _diff_lat_kernel(q_ref, kc_ref, vtc_ref, kl_ref, vtl_ref, kn_ref, lam_ref, g_ref, o_ref,
                     k_scr, vt_scr, qm_scr, sh_scr, m_scr, acc_scr, *, tq, tk, n_ctx, n_lat, out_scale):
    nt = (n_ctx + n_lat) // tk

    @pl.when(pl.program_id(1) == 0)
    def _():
        k_scr[0:n_ctx, :] = kc_ref[...]
        k_scr[n_ctx:, :] = kl_ref[...]
        vt_scr[:, 0:n_ctx] = vtc_ref[0]
        vt_scr[:, n_ctx:] = vtl_ref[0]

    q32 = q_ref[...].astype(F32)
    qt = q32.T
    rowg = lax.broadcasted_iota(jnp.int32, (GW, tq), 0) // MAP_DIM
    for u in range(N_MAPS):
        qm_scr[u] = jnp.where(rowg == u, qt, 0.0).astype(BF16)
    kmax = jnp.sqrt(_group_sum_sq(jnp.sqrt(kn_ref[0]), MAP_DIM))[0:1, :]
    bound = jnp.sqrt(_group_sum_sq(q32 * kmax, MAP_DIM)) * (1.0 + 2.0 ** -8)
    sh_scr[...] = bound.T - DIFF_HEADROOM
    acc_scr[...] = jnp.zeros(acc_scr.shape, F32)

    def tile(t, u):
        st = pl.multiple_of(t * tk, tk)
        h = u // 2
        return k_scr[pl.ds(st, tk), :], vt_scr[h * V_AUG:(h + 1) * V_AUG, pl.ds(st, tk)]

    def fast(t, carry):
        for u in range(N_MAPS):
            kt, vt = tile(t, u)
            s = _dot(kt, qm_scr[u])
            e = jnp.exp2(s - sh_scr[u * MAP_DIM:u * MAP_DIM + 1, :]).astype(BF16)
            acc_scr[u] = acc_scr[u] + _dot(vt, e)
        return carry

    lax.fori_loop(0, nt, fast, 0)

    lmin = acc_scr[0, HEAD_DIM:HEAD_DIM + 1, :]
    for u in range(1, N_MAPS):
        lmin = jnp.minimum(lmin, acc_scr[u, HEAD_DIM:HEAD_DIM + 1, :])

    @pl.when(jnp.logical_not(jnp.min(lmin) >= DIFF_MIN_SUM))
    def _():
        m_scr[...] = jnp.full(m_scr.shape, NEG, F32)
        acc_scr[...] = jnp.zeros(acc_scr.shape, F32)

        def exact(t, carry):
            for u in range(N_MAPS):
                kt, vt = tile(t, u)
                s = _dot(kt, qm_scr[u])
                mo = m_scr[u:u + 1, :]
                mn = jnp.maximum(mo, jnp.max(s, axis=0, keepdims=True))
                m_scr[u:u + 1, :] = mn
                e = jnp.exp2(s - mn).astype(BF16)
                acc_scr[u] = jnp.exp2(mo - mn) * acc_scr[u] + _dot(vt, e)
            return carry

        lax.fori_loop(0, nt, exact, 0)

    lam = lam_ref[0:1, 0:1]
    outs = []
    for h in range(N_HEADS):
        a1 = acc_scr[2 * h]
        a2 = acc_scr[2 * h + 1]
        o1 = a1[:HEAD_DIM] * (1.0 / a1[HEAD_DIM:HEAD_DIM + 1])
        o2 = a2[:HEAD_DIM] * (1.0 / a2[HEAD_DIM:HEAD_DIM + 1])
        outs.append(o1 - lam * o2)
    out = jnp.concatenate(outs, axis=0).T
    out = out * lax.rsqrt(_head_mean_sq(out) + EPS) * g_ref[...] * out_scale
    o_ref[...] = out.astype(BF16)


def _diff_lat(proj_l, proj_c, vt_l, vt_c, kn2, lam, g, b, n, c, tq, out_scale):
    nt = n // tq
    nk = c + n
    tk = next(t for t in DIFF_KEY_TILES if nk % t == 0)
    once = pl.Buffered(1)
    return pl.pallas_call(
        functools.partial(_diff_lat_kernel, tq=tq, tk=tk, n_ctx=c, n_lat=n, out_scale=out_scale),
        grid=(b, nt),
        in_specs=[pl.BlockSpec((tq, GW), lambda bi, i: (bi * nt + i, C_DF_Q)),
                  pl.BlockSpec((c, GW), lambda bi, i: (bi, C_DF_K)),
                  pl.BlockSpec((1, N_HEADS * V_AUG, c), lambda bi, i: (bi, 0, 0)),
                  pl.BlockSpec((n, GW), lambda bi, i: (bi, C_DF_K), pipeline_mode=once),
                  pl.BlockSpec((1, N_HEADS * V_AUG, n), lambda bi, i: (bi, 0, 0), pipeline_mode=once),
                  pl.BlockSpec((1, 8, GW), lambda bi, i: (bi, 0, 0)),
                  pl.BlockSpec((1, GW), lambda bi, i: (0, 0)),
                  pl.BlockSpec((1, GW), lambda bi, i: (0, 0))],
        out_specs=pl.BlockSpec((tq, GW), lambda bi, i: (bi * nt + i, 0)),
        out_shape=jax.ShapeDtypeStruct((b * n, GW), BF16),
        scratch_shapes=[pltpu.VMEM((nk, GW), BF16), pltpu.VMEM((N_HEADS * V_AUG, nk), BF16),
                        pltpu.VMEM((N_MAPS, GW, tq), BF16), pltpu.VMEM((GW, tq), F32),
                        pltpu.VMEM((N_MAPS, tq), F32), pltpu.VMEM((N_MAPS, V_AUG, tq), F32)],
        compiler_params=_cparams("arbitrary", "arbitrary"),
        name="diff_attention_lat",
    )(proj_l, proj_c, vt_c, proj_l, vt_l, kn2, lam, g)


def _out_proj_kernel(x_ref, ya_ref, hf_ref, hb_ref, gl_ref, of_ref, ob_ref, gr_ref, yd_ref, w_ref, g_ref, gate_ref,
                     o_ref):
    gl = gl_ref[...].astype(F32)
    gelu = 0.5 * gl * (1.0 + jnp.tanh(math.sqrt(2.0 / math.pi) * (gl + 0.044715 * (gl * gl * gl))))
    yb = (hf_ref[...].astype(F32) + hb_ref[...].astype(F32)) * gelu
    r = of_ref[...].astype(F32) + ob_ref[...].astype(F32)
    gr = gr_ref[...].astype(F32)
    yc = r * lax.rsqrt(_head_mean_sq(r) + EPS) * (gr * _sigmoid(gr))
    y = (_dot(ya_ref[...], w_ref[0:GW, :]) + _dot(yb.astype(BF16), w_ref[GW:2 * GW, :])
         + _dot(yc.astype(BF16), w_ref[2 * GW:3 * GW, :]) + _dot(yd_ref[...], w_ref[3 * GW:4 * GW, :]))
    o_ref[...] = x_ref[...] + gate_ref[0] * _rms(y, g_ref[...])


def _out_proj(x2d, ya, hf, hb, of, ob, yd, proj, w, g, gate, b, n, tm):
    nt = n // tm
    tok = lambda col: pl.BlockSpec((tm, GW), lambda bi, i: (bi * nt + i, col))
    return pl.pallas_call(
        _out_proj_kernel,
        grid=(b, nt),
        in_specs=[pl.BlockSpec((tm, D_MODEL), lambda bi, i: (bi * nt + i, 0)),
                  tok(0), tok(0), tok(0), tok(C_LRU_G), tok(0), tok(0), tok(C_RET_G), tok(0),
                  pl.BlockSpec((D_MODEL, D_MODEL), lambda bi, i: (0, 0)),
                  pl.BlockSpec((1, D_MODEL), lambda bi, i: (0, 0)),
                  pl.BlockSpec((1, 1, D_MODEL), lambda bi, i: (bi, 0, 0))],
        out_specs=pl.BlockSpec((tm, D_MODEL), lambda bi, i: (bi * nt + i, 0)),
        out_shape=jax.ShapeDtypeStruct((b * n, D_MODEL), F32),
        compiler_params=_cparams("arbitrary", "arbitrary"),
        name="out_proj",
    )(x2d, ya, hf, hb, proj, of, ob, proj, yd, w, g, gate)


FFN_CHUNK = 256


def _ffn_kernel(x_ref, g1_ref, sh_ref, sc_ref, w1_ref, w2_ref, g2_ref, gate_ref, o_ref):
    x = x_ref[...]
    h = (_rms(x, g1_ref[...]) * (1.0 + sc_ref[0]) + sh_ref[0]).astype(BF16)
    acc = jnp.zeros(x.shape, F32)
    for c in range(D_FF // FFN_CHUNK):
        lo, hi = c * FFN_CHUNK, (c + 1) * FFN_CHUNK
        gt = _dot(h, w1_ref[:, lo:hi])
        up = _dot(h, w1_ref[:, D_FF + lo:D_FF + hi])
        act = (gt * _sigmoid(gt) * up).astype(BF16)
        acc = acc + _dot(act, w2_ref[lo:hi, :])
    o_ref[...] = x + gate_ref[0] * _rms(acc, g2_ref[...])


def _ffn(x2d, g1, shift, scale, w1, w2, g2, gate, b, n, tm):
    nt = n // tm
    vec = pl.BlockSpec((1, D_MODEL), lambda bi, i: (0, 0))
    per_b = pl.BlockSpec((1, 1, D_MODEL), lambda bi, i: (bi, 0, 0))
    return pl.pallas_call(
        _ffn_kernel,
        grid=(b, nt),
        in_specs=[pl.BlockSpec((tm, D_MODEL), lambda bi, i: (bi * nt + i, 0)), vec, per_b, per_b,
                  pl.BlockSpec((D_MODEL, 2 * D_FF), lambda bi, i: (0, 0)),
                  pl.BlockSpec((D_FF, D_MODEL), lambda bi, i: (0, 0)), vec, per_b],
        out_specs=pl.BlockSpec((tm, D_MODEL), lambda bi, i: (bi * nt + i, 0)),
        out_shape=jax.ShapeDtypeStruct((b * n, D_MODEL), F32),
        compiler_params=_cparams("arbitrary", "arbitrary"),
        name="ffn",
    )(x2d, g1, shift, scale, w1, w2, g2, gate)


def _blockdiag(w):
    nb, bs, _ = w.shape
    return jnp.einsum('kcd,kj->kcjd', w, jnp.eye(nb, dtype=w.dtype)).reshape(nb * bs, nb * bs)


def _lru_params(conv_w, conv_b, gate_w, gate_b, lam):
    wg = jnp.stack([jnp.concatenate([_blockdiag(gate_w[d, 0]), _blockdiag(gate_w[d, 1])], axis=1)
                    for d in range(2)]).astype(BF16)
    bg = jnp.stack([jnp.concatenate([gate_b[d, 0].reshape(1, GW), gate_b[d, 1].reshape(1, GW)], axis=1)
                    for d in range(2)])
    return conv_w, conv_b.reshape(1, GW), wg, bg, lam.reshape(2, 1, GW)


def kernel(x, c, ctx, c_ctx, w_mod, b_mod, g_pre_mix, g_post_mix, g_pre_ffn, g_post_ffn, w_in, na_rpb, lru_conv_w,
           lru_conv_b, lru_gate_w, lru_gate_b, lru_lambda, ret_decay, diff_lambda, diff_subln, w_out, w_ffn_in,
           w_ffn_out):
    b, n, _ = x.shape
    nc = ctx.shape[1]
    depth = w_mod.shape[0]
    assert n % 1024 == 0 and nc % RET_CHUNK == 0

    r = -(-(b + 1) // 8) * 8
    cc = jnp.zeros((r, D_MODEL), F32).at[:b].set(c).at[b].set(c_ctx)
    mod = _mod(cc, w_mod, b_mod)
    tabs = _rope_tables(n)

    xl = x.reshape(b * n, D_MODEL)
    xc = ctx.reshape(b * nc, D_MODEL)
    tm, tm_proj = 512, 1024
    vec = lambda a: a.reshape(1, D_MODEL)
    for l in range(depth):
        last = l == depth - 1
        lam_init = 0.8 - 0.6 * math.exp(-0.3 * l)
        ml = [mod[l, :b, k * D_MODEL:(k + 1) * D_MODEL].reshape(b, 1, D_MODEL) for k in range(6)]
        mc = [jnp.broadcast_to(mod[l, b, k * D_MODEL:(k + 1) * D_MODEL], (b, 1, D_MODEL)) for k in range(6)]
        w_in_l = w_in[l].astype(BF16)
        w_out_l = w_out[l].astype(BF16)
        w1_l = w_ffn_in[l].astype(BF16)
        w2_l = w_ffn_out[l].astype(BF16)

        pc, vt_c, kn_c = _in_proj(xc, vec(g_pre_mix[l]), mc[0], mc[1], w_in_l, None, b, nc, nc)
        pt, vt_l, kn_l = _in_proj(xl, vec(g_pre_mix[l]), ml[0], ml[1], w_in_l, tabs, b, n, tm_proj)

        lru_p = _lru_params(lru_conv_w[l], lru_conv_b[l], lru_gate_w[l], lru_gate_b[l], lru_lambda[l])
        hf_c, hb_c, hfin = _lru(pc, b, nc, nc, *lru_p, jnp.zeros((b, 8, GW), F32))
        hf_l, hb_l, _ = _lru(pt, b, n, 1024, *lru_p, hfin)

        log_g = jax.nn.log_sigmoid(ret_decay[l].astype(F32))
        lgl = jnp.repeat(log_g, HEAD_DIM, axis=-1).reshape(2, 1, GW)
        lgh = jnp.broadcast_to(log_g.reshape(2 * N_HEADS, 1), (2 * N_HEADS, 128))
        of_c, ob_c, sfin = _ret(pc, b, nc, lgl, lgh, jnp.zeros((b, 2, GW, GW), F32))
        of_l, ob_l, _ = _ret(pt, b, n, lgl, lgh, sfin)

        ya_l = _na(pt, pc, _na_bias_tables(na_rpb[l]), b, n, nc)

        lq1, lk1, lq2, lk2 = diff_lambda[l].astype(F32)
        lam = jnp.exp(jnp.sum(lq1 * lk1)) - jnp.exp(jnp.sum(lq2 * lk2)) + lam_init
        lam_v = jnp.broadcast_to(lam, (1, GW)).astype(F32)
        g_sub = jnp.tile(diff_subln[l].astype(F32), N_HEADS).reshape(1, GW)
        yd_l = _diff_lat(pt, pc, vt_l, vt_c, jnp.maximum(kn_c, kn_l), lam_v, g_sub, b, n, nc, 512, 1.0 - lam_init)

        x_mid = _out_proj(xl, ya_l, hf_l, hb_l, of_l, ob_l, yd_l, pt, w_out_l, vec(g_post_mix[l]), ml[2], b, n,
                          tm_proj)
        xl_new = _ffn(x_mid, vec(g_pre_ffn[l]), ml[3], ml[4], w1_l, w2_l, vec(g_post_ffn[l]), ml[5], b, n, tm)

        if not last:
            ya_c = _ctx_attn(pc, b, nc, C_NA_Q, C_NA_K, C_NA_V)
            yd_c = _ctx_attn(pc, b, nc, C_DF_Q, C_DF_K, C_DF_V, lam_v, g_sub, 1.0 - lam_init)
            xc_mid = _out_proj(xc, ya_c, hf_c, hb_c, of_c, ob_c, yd_c, pc, w_out_l, vec(g_post_mix[l]), mc[2],
                               b, nc, nc)
            xc = _ffn(xc_mid, vec(g_pre_ffn[l]), mc[3], mc[4], w1_l, w2_l, vec(g_post_ffn[l]), mc[5], b, nc, nc)
        xl = xl_new
    return xl.reshape(b, n, D_MODEL)
```

```python
import functools
import math

import numpy as np
import jax
import jax.numpy as jnp
from jax import lax
from jax.experimental import pallas as pl
from jax.experimental.pallas import tpu as pltpu

F32 = jnp.float32
BF16 = jnp.bfloat16

D_MODEL = 1024
GRID_W = 64
HEAD_DIM = 64
N_HEADS = 4
GW = N_HEADS * HEAD_DIM
N_PROJ = 12
N_MAPS = 2 * N_HEADS
MAP_DIM = HEAD_DIM // 2
V_AUG = HEAD_DIM + 16
D_FF = 2816
NA_WIN_ROWS = 8
NA_WIN_COLS = 16
LRU_C = 8.0
RET_CHUNK = 128
ROPE_BASE = 10000.0
EPS = 1e-6
NEG = -1e30

C_NA_Q, C_NA_K, C_NA_V, C_LRU_X, C_LRU_G, C_RET_Q, C_RET_K, C_RET_V, C_RET_G, C_DF_Q, C_DF_K, C_DF_V = range(12)
_COL_SCALE = {C_NA_Q: HEAD_DIM ** -0.5, C_RET_K: HEAD_DIM ** -0.5,
              C_DF_Q: (HEAD_DIM // 2) ** -0.5 * math.log2(math.e)}

LRU_HALO = 16
V7X_VMEM_LIMIT = 56 * 1024 * 1024


def _cparams(*sem):
    return pltpu.CompilerParams(dimension_semantics=sem, vmem_limit_bytes=V7X_VMEM_LIMIT)


def _dot(a, b):
    return jnp.dot(a, b, preferred_element_type=F32)


def _dot_nt(a, b):
    return lax.dot_general(a, b, (((1,), (1,)), ((), ())), preferred_element_type=F32)


def _rms(x, g):
    return x * lax.rsqrt(jnp.mean(x * x, axis=-1, keepdims=True) + EPS) * g


def _sigmoid(x):
    return 1.0 / (1.0 + jnp.exp(-x))


def _group_sum_sq(y, group):
    y2 = y * y
    hi = y2.astype(BF16)
    lo = (y2 - hi.astype(F32)).astype(BF16)
    r = lax.broadcasted_iota(jnp.int32, (GW, GW), 0) // group
    c = lax.broadcasted_iota(jnp.int32, (GW, GW), 1) // group
    bd = jnp.where(r == c, 1.0, 0.0).astype(BF16)
    return _dot(hi, bd) + _dot(lo, bd)


def _head_mean_sq(y):
    return _group_sum_sq(y, HEAD_DIM) * (1.0 / HEAD_DIM)


def _mod_kernel(c_ref, w_ref, b_ref, o_ref):
    c = c_ref[...]
    s = c * _sigmoid(c)
    o_ref[0] = jnp.dot(s, w_ref[0], preferred_element_type=F32, precision=lax.Precision.HIGHEST) + b_ref[0]


def _mod(cc, w_mod, b_mod):
    depth = w_mod.shape[0]
    r = cc.shape[0]
    tn = 1536
    return pl.pallas_call(
        _mod_kernel,
        grid=(depth, 6 * D_MODEL // tn),
        in_specs=[pl.BlockSpec((r, D_MODEL), lambda l, j: (0, 0)),
                  pl.BlockSpec((1, D_MODEL, tn), lambda l, j: (l, 0, j)),
                  pl.BlockSpec((1, 1, tn), lambda l, j: (l, 0, j))],
        out_specs=pl.BlockSpec((1, r, tn), lambda l, j: (l, 0, j)),
        out_shape=jax.ShapeDtypeStruct((depth, r, 6 * D_MODEL), F32),
        compiler_params=_cparams("arbitrary", "arbitrary"),
        name="adaln_mod",
    )(cc, w_mod, b_mod.reshape(depth, 1, 6 * D_MODEL))


def _rope(p, cos, sin_signed, half):
    outs = []
    for c in range(GW // 128):
        xs = p[:, c * 128:(c + 1) * 128]
        lane = lax.broadcasted_iota(jnp.int32, xs.shape, 1)
        first = (lane % (2 * half)) < half
        partner = jnp.where(first, pltpu.roll(xs, 128 - half, 1), pltpu.roll(xs, half, 1))
        outs.append(xs * cos[:, c * 128:(c + 1) * 128] + partner * sin_signed[:, c * 128:(c + 1) * 128])
    return jnp.concatenate(outs, axis=1)


def _in_proj_kernel(*refs, rope):
    if rope:
        x_ref, g_ref, sh_ref, sc_ref, w_ref, tab_ref, o_ref, vt_ref, kn_ref = refs
    else:
        x_ref, g_ref, sh_ref, sc_ref, w_ref, o_ref, vt_ref, kn_ref = refs

    @pl.when(pl.program_id(1) == 0)
    def _():
        kn_ref[...] = jnp.zeros(kn_ref.shape, F32)

    h = _rms(x_ref[...], g_ref[...]) * (1.0 + sc_ref[0]) + sh_ref[0]
    hb = h.astype(BF16)
    for j in range(N_PROJ):
        p = _dot(hb, w_ref[:, j * GW:(j + 1) * GW])
        if j in _COL_SCALE:
            p = p * _COL_SCALE[j]
        if rope and j in (C_RET_Q, C_RET_K):
            p = _rope(p, tab_ref[0], tab_ref[1], HEAD_DIM // 2)
        if rope and j in (C_DF_Q, C_DF_K):
            p = _rope(p, tab_ref[2], tab_ref[3], HEAD_DIM // 4)
        pb = p.astype(BF16)
        o_ref[:, j * GW:(j + 1) * GW] = pb
        if j == C_DF_K:
            kf = pb.astype(F32)
            kn_ref[0] = jnp.maximum(kn_ref[0], jnp.broadcast_to(jnp.max(kf * kf, axis=0, keepdims=True), (8, GW)))
        if j == C_DF_V:
            vt = p.T.astype(BF16)
            ones = jnp.ones((V_AUG - HEAD_DIM, vt.shape[1]), BF16)
            for hd in range(N_HEADS):
                vt_ref[0, hd * V_AUG:hd * V_AUG + HEAD_DIM, :] = vt[hd * HEAD_DIM:(hd + 1) * HEAD_DIM]
                vt_ref[0, hd * V_AUG + HEAD_DIM:(hd + 1) * V_AUG, :] = ones


def _in_proj(x2d, g, shift, scale, w, tabs, b, n, tm):
    nt = n // tm
    rope = tabs is not None
    in_specs = [pl.BlockSpec((tm, D_MODEL), lambda bi, i: (bi * nt + i, 0)),
                pl.BlockSpec((1, D_MODEL), lambda bi, i: (0, 0)),
                pl.BlockSpec((1, 1, D_MODEL), lambda bi, i: (bi, 0, 0)),
                pl.BlockSpec((1, 1, D_MODEL), lambda bi, i: (bi, 0, 0)),
                pl.BlockSpec((D_MODEL, N_PROJ * GW), lambda bi, i: (0, 0))]
    args = [x2d, g, shift, scale, w]
    if rope:
        in_specs.append(pl.BlockSpec((4, tm, GW), lambda bi, i: (0, i, 0)))
        args.append(tabs)
    return pl.pallas_call(
        functools.partial(_in_proj_kernel, rope=rope),
        grid=(b, nt),
        in_specs=in_specs,
        out_specs=[pl.BlockSpec((tm, N_PROJ * GW), lambda bi, i: (bi * nt + i, 0)),
                   pl.BlockSpec((1, N_HEADS * V_AUG, tm), lambda bi, i: (bi, 0, i)),
                   pl.BlockSpec((1, 8, GW), lambda bi, i: (bi, 0, 0))],
        out_shape=[jax.ShapeDtypeStruct((b * n, N_PROJ * GW), BF16),
                   jax.ShapeDtypeStruct((b, N_HEADS * V_AUG, n), BF16),
                   jax.ShapeDtypeStruct((b, 8, GW), F32)],
        compiler_params=_cparams("arbitrary", "arbitrary"),
        name="in_proj_rope" if rope else "in_proj",
    )(*args)


def _rope_tables(n):
    t = jnp.arange(n)
    row = (t // GRID_W).astype(F32)
    col = (t % GRID_W).astype(F32)

    def tab(dim):
        nf = dim // 4
        inv = ROPE_BASE ** (-jnp.arange(nf, dtype=F32) / nf)
        ang = jnp.concatenate([row[:, None] * inv, col[:, None] * inv], axis=-1)
        cos, sin = jnp.cos(ang), jnp.sin(ang)
        reps = GW // dim
        return (jnp.tile(jnp.concatenate([cos, cos], axis=-1), (1, reps)),
                jnp.tile(jnp.concatenate([-sin, sin], axis=-1), (1, reps)))

    cr, sr = tab(HEAD_DIM)
    cd, sd = tab(HEAD_DIM // 2)
    return jnp.stack([cr, sr, cd, sd])


def _lru_kernel(xf_ref, xfp_ref, xfn_ref, xb_ref, xbp_ref, xbn_ref, cw_ref, cb_ref, wg_ref, bg_ref, lam_ref,
                h0_ref, hf_ref, hb_ref, hfin_ref, a_scr, u_scr, hc_scr, *, tn, nt):
    i = pl.program_id(1)

    @pl.when(i == 0)
    def _():
        hc_scr[...] = h0_ref[0]

    cw = cw_ref[...]
    ng8 = tn // 8
    sub = lax.broadcasted_iota(jnp.int32, (ng8, 8, GW), 1)
    row = lax.broadcasted_iota(jnp.int32, (tn, GW), 0)
    row8 = row % 8

    def shifted(xm, k):
        rot = pltpu.roll(xm.reshape(ng8, 8, GW), (-k) % 8, 1).reshape(tn, GW)
        if k < 0:
            return jnp.where(row8 < -k, pltpu.roll(rot, 8, 0), rot)
        return jnp.where(row8 >= 8 - k, pltpu.roll(rot, tn - 8, 0), rot)

    def coeffs(x_ref, xp_ref, xn_ref, tile, d):
        xm = x_ref[...].astype(F32)
        prev = jnp.where(tile > 0, xp_ref[LRU_HALO - 1:LRU_HALO, :].astype(F32), 0.0)
        nxt = jnp.where(tile < nt - 1, xn_ref[0:2, :].astype(F32), 0.0)
        xm1 = jnp.where(row == 0, prev, shifted(xm, -1))
        xp1 = jnp.where(row == tn - 1, nxt[0:1], shifted(xm, 1))
        xp2 = jnp.where(row == tn - 2, nxt[0:1], shifted(xm, 2))
        xp2 = jnp.where(row == tn - 1, nxt[1:2], xp2)
        xb = cw[0:1] * xm1 + cw[1:2] * xm + cw[2:3] * xp1 + cw[3:4] * xp2 + cb_ref[...]
        tg = jnp.tanh(_dot(xb.astype(BF16), wg_ref[d]) + bg_ref[d]) + 1.0
        nl = -lam_ref[d]
        softplus = jnp.maximum(nl, 0.0) + jnp.log1p(jnp.exp(-jnp.abs(nl)))
        log_a = (-0.5 * LRU_C * softplus) * tg[:, :GW]
        a = jnp.exp(log_a)
        th = jnp.tanh(log_a)
        v = -0.5 * th
        u = v * lax.rsqrt(jnp.maximum(v, 1e-37)) * lax.rsqrt(1.0 - th) * (tg[:, GW:] * xb)

        a = a.reshape(ng8, 8, GW)
        u = u.reshape(ng8, 8, GW)
        for sft in (1, 2, 4):
            keep = sub >= sft if d == 0 else sub < 8 - sft
            rot = sft if d == 0 else 8 - sft
            a_s = jnp.where(keep, pltpu.roll(a, rot, 1), 1.0)
            u_s = jnp.where(keep, pltpu.roll(u, rot, 1), 0.0)
            u = a * u_s + u
            a = a * a_s
        a_scr[d] = a.reshape(tn, GW)
        u_scr[d] = u.reshape(tn, GW)

    coeffs(xf_ref, xfp_ref, xfn_ref, i, 0)
    coeffs(xb_ref, xbp_ref, xbn_ref, nt - 1 - i, 1)

    gs = LRU_HALO
    ng = tn // gs

    def body(g, carry):
        hf, hb = carry
        fb = pl.multiple_of(g * gs, gs)
        bb = pl.multiple_of((ng - 1 - g) * gs, gs)
        f1 = a_scr[0, pl.ds(fb, 8), :] * hf + u_scr[0, pl.ds(fb, 8), :]
        f2 = a_scr[0, pl.ds(fb + 8, 8), :] * f1[7:8] + u_scr[0, pl.ds(fb + 8, 8), :]
        b2 = a_scr[1, pl.ds(bb + 8, 8), :] * hb + u_scr[1, pl.ds(bb + 8, 8), :]
        b1 = a_scr[1, pl.ds(bb, 8), :] * b2[0:1] + u_scr[1, pl.ds(bb, 8), :]
        hf_ref[pl.ds(fb, gs), :] = jnp.concatenate([f1, f2], axis=0).astype(BF16)
        hb_ref[pl.ds(bb, gs), :] = jnp.concatenate([b1, b2], axis=0).astype(BF16)
        return f2[7:8], b1[0:1]

    hf, hb = lax.fori_loop(0, ng, body, (hc_scr[0:1, :], hc_scr[1:2, :]))
    hc_scr[0:1, :] = hf
    hc_scr[1:2, :] = hb

    @pl.when(i == nt - 1)
    def _():
        hfin_ref[0] = jnp.concatenate([hf, hb, jnp.zeros((6, GW), F32)], axis=0)


def _lru(proj, b, n, tn, cw, cb, wg, bg, lam, h0):
    nt = n // tn
    hb8 = tn // LRU_HALO

    def main(rev):
        return pl.BlockSpec((tn, GW), lambda bi, i: (bi * nt + (nt - 1 - i if rev else i), C_LRU_X))

    def prev(rev):
        def im(bi, i):
            t = nt - 1 - i if rev else i
            return (jnp.maximum((bi * nt + t) * hb8 - 1, 0), C_LRU_X)
        return pl.BlockSpec((LRU_HALO, GW), im)

    def nxt(rev):
        def im(bi, i):
            t = nt - 1 - i if rev else i
            return (jnp.minimum((bi * nt + t + 1) * hb8, b * nt * hb8 - 1), C_LRU_X)
        return pl.BlockSpec((LRU_HALO, GW), im)

    const2 = lambda bi, i: (0, 0)
    const3 = lambda bi, i: (0, 0, 0)
    return pl.pallas_call(
        functools.partial(_lru_kernel, tn=tn, nt=nt),
        grid=(b, nt),
        in_specs=[main(False), prev(False), nxt(False), main(True), prev(True), nxt(True),
                  pl.BlockSpec((4, GW), const2), pl.BlockSpec((1, GW), const2),
                  pl.BlockSpec((2, GW, 2 * GW), const3), pl.BlockSpec((2, 1, 2 * GW), const3),
                  pl.BlockSpec((2, 1, GW), const3),
                  pl.BlockSpec((1, 8, GW), lambda bi, i: (bi, 0, 0))],
        out_specs=[pl.BlockSpec((tn, GW), lambda bi, i: (bi * nt + i, 0)),
                   pl.BlockSpec((tn, GW), lambda bi, i: (bi * nt + nt - 1 - i, 0)),
                   pl.BlockSpec((1, 8, GW), lambda bi, i: (bi, 0, 0))],
        out_shape=[jax.ShapeDtypeStruct((b * n, GW), BF16), jax.ShapeDtypeStruct((b * n, GW), BF16),
                   jax.ShapeDtypeStruct((b, 8, GW), F32)],
        scratch_shapes=[pltpu.VMEM((2, tn, GW), F32), pltpu.VMEM((2, tn, GW), F32), pltpu.VMEM((8, GW), F32)],
        compiler_params=_cparams("arbitrary", "arbitrary"),
        name="rglru",
    )(proj, proj, proj, proj, proj, proj, cw, cb, wg, bg, lam, h0)


def _ret_kernel(qf_ref, kf_ref, vf_ref, qb_ref, kb_ref, vb_ref, lgl_ref, lgh_ref, s0_ref,
                of_ref, ob_ref, sfin_ref, s_scr, dm_scr, qd_scr, kd_scr, *, nc, cps):
    c = pl.program_id(1)
    cs = RET_CHUNK

    @pl.when(c == 0)
    def _():
        s_scr[...] = s0_ref[0]
        ii = lax.broadcasted_iota(jnp.int32, (cs, cs), 0)
        jj = lax.broadcasted_iota(jnp.int32, (cs, cs), 1)
        t = lax.broadcasted_iota(jnp.int32, (cs, GW), 0).astype(F32)
        for d in range(2):
            rel = ii - jj if d == 0 else jj - ii - 1
            ok = rel >= 0
            relf = jnp.where(ok, rel, 0).astype(F32)
            for h in range(N_HEADS):
                lg = lgh_ref[d * N_HEADS + h:d * N_HEADS + h + 1, :]
                dm_scr[d, h] = jnp.where(ok, jnp.exp(lg * relf), 0.0)
            lgl = lgl_ref[d]
            if d == 0:
                qd_scr[d] = jnp.exp(lgl * (t + 1.0))
                kd_scr[d] = jnp.exp(lgl * (cs - 1.0 - t))
            else:
                qd_scr[d] = jnp.exp(lgl * (cs - 1.0 - t))
                kd_scr[d] = jnp.exp(lgl * t)

    lane = lax.broadcasted_iota(jnp.int32, (cs, GW), 1) // HEAD_DIM
    br = lax.broadcasted_iota(jnp.int32, (GW, GW), 0) // HEAD_DIM
    bc = lax.broadcasted_iota(jnp.int32, (GW, GW), 1) // HEAD_DIM

    def direction(d, q_ref, k_ref, v_ref, o_ref):
        s = s_scr[d]
        cdec = jnp.exp(lgl_ref[d] * float(cs))
        for step in range(cps):
            j = step if d == 0 else cps - 1 - step
            rs = slice(j * cs, (j + 1) * cs)
            k = k_ref[rs, :]
            v = v_ref[rs, :]
            q32 = q_ref[rs, :].astype(F32)
            v32 = v.astype(F32)
            parts = []
            vparts = []
            for h in range(N_HEADS):
                mk = lane == h
                qh = jnp.where(mk, q32, 0.0).astype(BF16)
                parts.append((_dot_nt(qh, k) * dm_scr[d, h]).astype(BF16))
                vparts.append(jnp.where(mk, v32, 0.0).astype(BF16))
            p = jnp.concatenate(parts, axis=1)
            vs = jnp.concatenate(vparts, axis=0)
            o = _dot(p, vs) + _dot((q32 * qd_scr[d]).astype(BF16), s.astype(BF16))
            o_ref[rs, :] = o.astype(BF16)
            kd = (k.astype(F32) * kd_scr[d]).T.astype(BF16)
            s = s * cdec + jnp.where(br == bc, _dot(kd, v), 0.0)
        s_scr[d] = s

    direction(0, qf_ref, kf_ref, vf_ref, of_ref)
    direction(1, qb_ref, kb_ref, vb_ref, ob_ref)

    @pl.when(c == nc - 1)
    def _():
        sfin_ref[0] = s_scr[...]


RET_CHUNKS_PER_STEP = 8


def _ret(proj, b, n, lgl, lgh, s0):
    cps = min(RET_CHUNKS_PER_STEP, n // RET_CHUNK)
    tb = cps * RET_CHUNK
    nc = n // tb

    def blk(col, rev):
        return pl.BlockSpec((tb, GW), lambda bi, c: (bi * nc + (nc - 1 - c if rev else c), col))

    return pl.pallas_call(
        functools.partial(_ret_kernel, nc=nc, cps=cps),
        grid=(b, nc),
        in_specs=[blk(C_RET_Q, False), blk(C_RET_K, False), blk(C_RET_V, False),
                  blk(C_RET_Q, True), blk(C_RET_K, True), blk(C_RET_V, True),
                  pl.BlockSpec((2, 1, GW), lambda bi, c: (0, 0, 0)),
                  pl.BlockSpec((2 * N_HEADS, 128), lambda bi, c: (0, 0)),
                  pl.BlockSpec((1, 2, GW, GW), lambda bi, c: (bi, 0, 0, 0))],
        out_specs=[pl.BlockSpec((tb, GW), lambda bi, c: (bi * nc + c, 0)),
                   pl.BlockSpec((tb, GW), lambda bi, c: (bi * nc + nc - 1 - c, 0)),
                   pl.BlockSpec((1, 2, GW, GW), lambda bi, c: (bi, 0, 0, 0))],
        out_shape=[jax.ShapeDtypeStruct((b * n, GW), BF16), jax.ShapeDtypeStruct((b * n, GW), BF16),
                   jax.ShapeDtypeStruct((b, 2, GW, GW), F32)],
        scratch_shapes=[pltpu.VMEM((2, GW, GW), F32), pltpu.VMEM((2, N_HEADS, RET_CHUNK, RET_CHUNK), F32),
                        pltpu.VMEM((2, RET_CHUNK, GW), F32), pltpu.VMEM((2, RET_CHUNK, GW), F32)],
        compiler_params=_cparams("arbitrary", "arbitrary"),
        name="retention",
    )(proj, proj, proj, proj, proj, proj, lgl, lgh, s0)


def _na_kernel(q_ref, k_ref, v_ref, kc_ref, vc_ref, bias_ref, o_ref, *, rows, rb):
    i0 = pl.program_id(1) * rb
    nwin = NA_WIN_ROWS * GRID_W
    kc = kc_ref[...]
    vc = vc_ref[...]
    lane = lax.broadcasted_iota(jnp.int32, (GRID_W, GW), 1) // HEAD_DIM

    def row(i, carry):
        r = i0 + i
        r0 = jnp.clip(r - NA_WIN_ROWS // 2, 0, rows - NA_WIN_ROWS)
        start = pl.multiple_of(r0 * GRID_W, GRID_W)
        qrow = pl.multiple_of(i * GRID_W, GRID_W)
        kw = k_ref[pl.ds(start, nwin), :]
        vw = v_ref[pl.ds(start, nwin), :]
        q32 = q_ref[pl.ds(qrow, GRID_W), :].astype(F32)
        qs = jnp.concatenate([jnp.where(lane == h, q32, 0.0) for h in range(N_HEADS)], axis=0).astype(BF16)
        sw = _dot_nt(qs, kw) + bias_ref[r0 - r + NA_WIN_ROWS - 1]
        sc = _dot_nt(qs, kc)
        m = jnp.maximum(jnp.max(sw, axis=-1, keepdims=True), jnp.max(sc, axis=-1, keepdims=True))
        ew = jnp.exp(sw - m)
        ec = jnp.exp(sc - m)
        l = jnp.sum(ew, axis=-1, keepdims=True) + jnp.sum(ec, axis=-1, keepdims=True)
        o = (_dot(ew.astype(BF16), vw) + _dot(ec.astype(BF16), vc)) * (1.0 / l)
        out = jnp.zeros((GRID_W, GW), F32)
        for h in range(N_HEADS):
            out = jnp.where(lane == h, o[h * GRID_W:(h + 1) * GRID_W], out)
        o_ref[pl.ds(qrow, GRID_W), :] = out.astype(BF16)
        return carry

    lax.fori_loop(0, rb, row, 0, unroll=NA_UNROLL)


NA_ROWS_PER_STEP = 16
NA_UNROLL = 16


def _na(proj_l, proj_c, bias, b, n, c):
    rows = n // GRID_W
    rb = NA_ROWS_PER_STEP
    nrb = rows // rb
    tq = rb * GRID_W
    return pl.pallas_call(
        functools.partial(_na_kernel, rows=rows, rb=rb),
        grid=(b, nrb),
        in_specs=[pl.BlockSpec((tq, GW), lambda bi, i: (bi * nrb + i, C_NA_Q)),
                  pl.BlockSpec((n, GW), lambda bi, i: (bi, C_NA_K)),
                  pl.BlockSpec((n, GW), lambda bi, i: (bi, C_NA_V)),
                  pl.BlockSpec((c, GW), lambda bi, i: (bi, C_NA_K)),
                  pl.BlockSpec((c, GW), lambda bi, i: (bi, C_NA_V)),
                  pl.BlockSpec((NA_WIN_ROWS, N_HEADS * GRID_W, NA_WIN_ROWS * GRID_W), lambda bi, i: (0, 0, 0))],
        out_specs=pl.BlockSpec((tq, GW), lambda bi, i: (bi * nrb + i, 0)),
        out_shape=jax.ShapeDtypeStruct((b * n, GW), BF16),
        compiler_params=_cparams("arbitrary", "arbitrary"),
        name="na_attention",
    )(proj_l, proj_l, proj_l, proj_c, proj_c, bias)


def _na_bias_tables(rpb):
    nr, ncol = 2 * NA_WIN_ROWS - 1, 2 * NA_WIN_COLS - 1
    cq = np.arange(GRID_W)[:, None]
    ck = np.arange(GRID_W)[None, :]
    cstart = np.clip(cq - NA_WIN_COLS // 2, 0, GRID_W - NA_WIN_COLS)
    okc = (ck >= cstart) & (ck < cstart + NA_WIN_COLS)
    dc = np.clip(ck - cq + NA_WIN_COLS - 1, 0, ncol - 1)
    ohc = (dc[..., None] == np.arange(ncol)) & okc[..., None]
    dr = np.arange(NA_WIN_ROWS)[:, None] + np.arange(NA_WIN_ROWS)[None, :]
    ohr = dr[..., None] == np.arange(nr)
    bias = jnp.einsum('vjr,hrd,ckd->vhcjk', jnp.asarray(ohr, F32), rpb.astype(F32), jnp.asarray(ohc, F32),
                      precision=lax.Precision.HIGHEST)
    bias = jnp.where(jnp.asarray(okc)[None, None, :, None, :], bias, NEG)
    return bias.reshape(NA_WIN_ROWS, N_HEADS * GRID_W, NA_WIN_ROWS * GRID_W)


def _ctx_attn_kernel(*refs, diff, out_scale):
    if diff:
        q_ref, k_ref, v_ref, lam_ref, g_ref, o_ref = refs
    else:
        q_ref, k_ref, v_ref, o_ref = refs
    k = k_ref[...]
    v = v_ref[...]
    q32 = q_ref[...].astype(F32)
    lane = lax.broadcasted_iota(jnp.int32, q32.shape, 1)

    def softmax_pv(mk):
        s = _dot_nt(jnp.where(mk, q32, 0.0).astype(BF16), k)
        z = s - jnp.max(s, axis=-1, keepdims=True)
        e = jnp.exp2(z) if diff else jnp.exp(z)
        return _dot(e.astype(BF16), v) * (1.0 / jnp.sum(e, axis=-1, keepdims=True))

    out = jnp.zeros(q32.shape, F32)
    for h in range(N_HEADS):
        if diff:
            o = (softmax_pv(lane // (HEAD_DIM // 2) == 2 * h)
                 - lam_ref[...] * softmax_pv(lane // (HEAD_DIM // 2) == 2 * h + 1))
        else:
            o = softmax_pv(lane // HEAD_DIM == h)
        out = jnp.where(lane // HEAD_DIM == h, o, out)
    if diff:
        out = out * lax.rsqrt(_head_mean_sq(out) + EPS) * g_ref[...] * out_scale
    o_ref[...] = out.astype(BF16)


def _ctx_attn(proj_c, b, c, qcol, kcol, vcol, lam=None, g=None, out_scale=1.0):
    diff = lam is not None
    in_specs = [pl.BlockSpec((c, GW), lambda bi: (bi, qcol)),
                pl.BlockSpec((c, GW), lambda bi: (bi, kcol)),
                pl.BlockSpec((c, GW), lambda bi: (bi, vcol))]
    args = [proj_c, proj_c, proj_c]
    if diff:
        in_specs += [pl.BlockSpec((1, GW), lambda bi: (0, 0)), pl.BlockSpec((1, GW), lambda bi: (0, 0))]
        args += [lam, g]
    return pl.pallas_call(
        functools.partial(_ctx_attn_kernel, diff=diff, out_scale=out_scale),
        grid=(b,),
        in_specs=in_specs,
        out_specs=pl.BlockSpec((c, GW), lambda bi: (bi, 0)),
        out_shape=jax.ShapeDtypeStruct((b * c, GW), BF16),
        compiler_params=_cparams("arbitrary"),
        name="ctx_diff_attention" if diff else "ctx_attention",
    )(*args)


DIFF_KEY_TILES = (2816, 768, 256)
DIFF_HEADROOM = 100.0
DIFF_MIN_SUM = 2.0 ** -100


def _diff_lat_kernel(q_ref, kc_ref, vtc_ref, kl_ref, vtl_ref, kn_ref, lam_ref, g_ref, o_ref,
                     k_scr, vt_scr, qm_scr, sh_scr, m_scr, acc_scr, *, tq, tk, n_ctx, n_lat, out_scale):
    nt = (n_ctx + n_lat) // tk

    @pl.when(pl.program_id(1) == 0)
    def _():
        k_scr[0:n_ctx, :] = kc_ref[...]
        k_scr[n_ctx:, :] = kl_ref[...]
        vt_scr[:, 0:n_ctx] = vtc_ref[0]
        vt_scr[:, n_ctx:] = vtl_ref[0]

    q32 = q_ref[...].astype(F32)
    qt = q32.T
    rowg = lax.broadcasted_iota(jnp.int32, (GW, tq), 0) // MAP_DIM
    for u in range(N_MAPS):
        qm_scr[u] = jnp.where(rowg == u, qt, 0.0).astype(BF16)
    kmax = jnp.sqrt(_group_sum_sq(jnp.sqrt(kn_ref[0]), MAP_DIM))[0:1, :]
    bound = jnp.sqrt(_group_sum_sq(q32 * kmax, MAP_DIM)) * (1.0 + 2.0 ** -8)
    sh_scr[...] = bound.T - DIFF_HEADROOM
    acc_scr[...] = jnp.zeros(acc_scr.shape, F32)

    def tile(t, u):
        st = pl.multiple_of(t * tk, tk)
        h = u // 2
        return k_scr[pl.ds(st, tk), :], vt_scr[h * V_AUG:(h + 1) * V_AUG, pl.ds(st, tk)]

    def fast(t, carry):
        for u in range(N_MAPS):
            kt, vt = tile(t, u)
            s = _dot(kt, qm_scr[u])
            e = jnp.exp2(s - sh_scr[u * MAP_DIM:u * MAP_DIM + 1, :]).astype(BF16)
            acc_scr[u] = acc_scr[u] + _dot(vt, e)
        return carry

    lax.fori_loop(0, nt, fast, 0)

    lmin = acc_scr[0, HEAD_DIM:HEAD_DIM + 1, :]
    for u in range(1, N_MAPS):
        lmin = jnp.minimum(lmin, acc_scr[u, HEAD_DIM:HEAD_DIM + 1, :])

    @pl.when(jnp.logical_not(jnp.min(lmin) >= DIFF_MIN_SUM))
    def _():
        m_scr[...] = jnp.full(m_scr.shape, NEG, F32)
        acc_scr[...] = jnp.zeros(acc_scr.shape, F32)

        def exact(t, carry):
            for u in range(N_MAPS):
                kt, vt = tile(t, u)
                s = _dot(kt, qm_scr[u])
                mo = m_scr[u:u + 1, :]
                mn = jnp.maximum(mo, jnp.max(s, axis=0, keepdims=True))
                m_scr[u:u + 1, :] = mn
                e = jnp.exp2(s - mn).astype(BF16)
                acc_scr[u] = jnp.exp2(mo - mn) * acc_scr[u] + _dot(vt, e)
            return carry

        lax.fori_loop(0, nt, exact, 0)

    lam = lam_ref[0:1, 0:1]
    outs = []
    for h in range(N_HEADS):
        a1 = acc_scr[2 * h]
        a2 = acc_scr[2 * h + 1]
        o1 = a1[:HEAD_DIM] * (1.0 / a1[HEAD_DIM:HEAD_DIM + 1])
        o2 = a2[:HEAD_DIM] * (1.0 / a2[HEAD_DIM:HEAD_DIM + 1])
        outs.append(o1 - lam * o2)
    out = jnp.concatenate(outs, axis=0).T
    out = out * lax.rsqrt(_head_mean_sq(out) + EPS) * g_ref[...] * out_scale
    o_ref[...] = out.astype(BF16)


def _diff_lat(proj_l, proj_c, vt_l, vt_c, kn2, lam, g, b, n, c, tq, out_scale):
    nt = n // tq
    nk = c + n
    tk = next(t for t in DIFF_KEY_TILES if nk % t == 0)
    once = pl.Buffered(1)
    return pl.pallas_call(
        functools.partial(_diff_lat_kernel, tq=tq, tk=tk, n_ctx=c, n_lat=n, out_scale=out_scale),
        grid=(b, nt),
        in_specs=[pl.BlockSpec((tq, GW), lambda bi, i: (bi * nt + i, C_DF_Q)),
                  pl.BlockSpec((c, GW), lambda bi, i: (bi, C_DF_K)),
                  pl.BlockSpec((1, N_HEADS * V_AUG, c), lambda bi, i: (bi, 0, 0)),
                  pl.BlockSpec((n, GW), lambda bi, i: (bi, C_DF_K), pipeline_mode=once),
                  pl.BlockSpec((1, N_HEADS * V_AUG, n), lambda bi, i: (bi, 0, 0), pipeline_mode=once),
                  pl.BlockSpec((1, 8, GW), lambda bi, i: (bi, 0, 0)),
                  pl.BlockSpec((1, GW), lambda bi, i: (0, 0)),
                  pl.BlockSpec((1, GW), lambda bi, i: (0, 0))],
        out_specs=pl.BlockSpec((tq, GW), lambda bi, i: (bi * nt + i, 0)),
        out_shape=jax.ShapeDtypeStruct((b * n, GW), BF16),
        scratch_shapes=[pltpu.VMEM((nk, GW), BF16), pltpu.VMEM((N_HEADS * V_AUG, nk), BF16),
                        pltpu.VMEM((N_MAPS, GW, tq), BF16), pltpu.VMEM((GW, tq), F32),
                        pltpu.VMEM((N_MAPS, tq), F32), pltpu.VMEM((N_MAPS, V_AUG, tq), F32)],
        compiler_params=_cparams("arbitrary", "arbitrary"),
        name="diff_attention_lat",
    )(proj_l, proj_c, vt_c, proj_l, vt_l, kn2, lam, g)


def _out_proj_kernel(x_ref, ya_ref, hf_ref, hb_ref, gl_ref, of_ref, ob_ref, gr_ref, yd_ref, w_ref, g_ref, gate_ref,
                     o_ref):
    gl = gl_ref[...].astype(F32)
    gelu = 0.5 * gl * (1.0 + jnp.tanh(math.sqrt(2.0 / math.pi) * (gl + 0.044715 * (gl * gl * gl))))
    yb = (hf_ref[...].astype(F32) + hb_ref[...].astype(F32)) * gelu
    r = of_ref[...].astype(F32) + ob_ref[...].astype(F32)
    gr = gr_ref[...].astype(F32)
    yc = r * lax.rsqrt(_head_mean_sq(r) + EPS) * (gr * _sigmoid(gr))
    y = (_dot(ya_ref[...], w_ref[0:GW, :]) + _dot(yb.astype(BF16), w_ref[GW:2 * GW, :])
         + _dot(yc.astype(BF16), w_ref[2 * GW:3 * GW, :]) + _dot(yd_ref[...], w_ref[3 * GW:4 * GW, :]))
    o_ref[...] = x_ref[...] + gate_ref[0] * _rms(y, g_ref[...])


def _out_proj(x2d, ya, hf, hb, of, ob, yd, proj, w, g, gate, b, n, tm):
    nt = n // tm
    tok = lambda col: pl.BlockSpec((tm, GW), lambda bi, i: (bi * nt + i, col))
    return pl.pallas_call(
        _out_proj_kernel,
        grid=(b, nt),
        in_specs=[pl.BlockSpec((tm, D_MODEL), lambda bi, i: (bi * nt + i, 0)),
                  tok(0), tok(0), tok(0), tok(C_LRU_G), tok(0), tok(0), tok(C_RET_G), tok(0),
                  pl.BlockSpec((D_MODEL, D_MODEL), lambda bi, i: (0, 0)),
                  pl.BlockSpec((1, D_MODEL), lambda bi, i: (0, 0)),
                  pl.BlockSpec((1, 1, D_MODEL), lambda bi, i: (bi, 0, 0))],
        out_specs=pl.BlockSpec((tm, D_MODEL), lambda bi, i: (bi * nt + i, 0)),
        out_shape=jax.ShapeDtypeStruct((b * n, D_MODEL), F32),
        compiler_params=_cparams("arbitrary", "arbitrary"),
        name="out_proj",
    )(x2d, ya, hf, hb, proj, of, ob, proj, yd, w, g, gate)


FFN_CHUNK = 256


def _ffn_kernel(x_ref, g1_ref, sh_ref, sc_ref, w1_ref, w2_ref, g2_ref, gate_ref, o_ref):
    x = x_ref[...]
    h = (_rms(x, g1_ref[...]) * (1.0 + sc_ref[0]) + sh_ref[0]).astype(BF16)
    acc = jnp.zeros(x.shape, F32)
    for c in range(D_FF // FFN_CHUNK):
        lo, hi = c * FFN_CHUNK, (c + 1) * FFN_CHUNK
        gt = _dot(h, w1_ref[:, lo:hi])
        up = _dot(h, w1_ref[:, D_FF + lo:D_FF + hi])
        act = (gt * _sigmoid(gt) * up).astype(BF16)
        acc = acc + _dot(act, w2_ref[lo:hi, :])
    o_ref[...] = x + gate_ref[0] * _rms(acc, g2_ref[...])


def _ffn(x2d, g1, shift, scale, w1, w2, g2, gate, b, n, tm):
    nt = n // tm
    vec = pl.BlockSpec((1, D_MODEL), lambda bi, i: (0, 0))
    per_b = pl.BlockSpec((1, 1, D_MODEL), lambda bi, i: (bi, 0, 0))
    return pl.pallas_call(
        _ffn_kernel,
        grid=(b, nt),
        in_specs=[pl.BlockSpec((tm, D_MODEL), lambda bi, i: (bi * nt + i, 0)), vec, per_b, per_b,
                  pl.BlockSpec((D_MODEL, 2 * D_FF), lambda bi, i: (0, 0)),
                  pl.BlockSpec((D_FF, D_MODEL), lambda bi, i: (0, 0)), vec, per_b],
        out_specs=pl.BlockSpec((tm, D_MODEL), lambda bi, i: (bi * nt + i, 0)),
        out_shape=jax.ShapeDtypeStruct((b * n, D_MODEL), F32),
        compiler_params=_cparams("arbitrary", "arbitrary"),
        name="ffn",
    )(x2d, g1, shift, scale, w1, w2, g2, gate)


def _blockdiag(w):
    nb, bs, _ = w.shape
    return jnp.einsum('kcd,kj->kcjd', w, jnp.eye(nb, dtype=w.dtype)).reshape(nb * bs, nb * bs)


def _lru_params(conv_w, conv_b, gate_w, gate_b, lam):
    wg = jnp.stack([jnp.concatenate([_blockdiag(gate_w[d, 0]), _blockdiag(gate_w[d, 1])], axis=1)
                    for d in range(2)])
    bg = jnp.stack([jnp.concatenate([gate_b[d, 0].reshape(1, GW), gate_b[d, 1].reshape(1, GW)], axis=1)
                    for d in range(2)])
    return conv_w, conv_b.reshape(1, GW), (0.5 * wg).astype(BF16), 0.5 * bg, lam.reshape(2, 1, GW)


def kernel(x, c, ctx, c_ctx, w_mod, b_mod, g_pre_mix, g_post_mix, g_pre_ffn, g_post_ffn, w_in, na_rpb, lru_conv_w,
           lru_conv_b, lru_gate_w, lru_gate_b, lru_lambda, ret_decay, diff_lambda, diff_subln, w_out, w_ffn_in,
           w_ffn_out):
    b, n, _ = x.shape
    nc = ctx.shape[1]
    depth = w_mod.shape[0]
    assert n % 1024 == 0 and nc % RET_CHUNK == 0

    r = -(-(b + 1) // 8) * 8
    cc = jnp.zeros((r, D_MODEL), F32).at[:b].set(c).at[b].set(c_ctx)
    mod = _mod(cc, w_mod, b_mod)
    tabs = _rope_tables(n)

    xl = x.reshape(b * n, D_MODEL)
    xc = ctx.reshape(b * nc, D_MODEL)
    tm, tm_proj = 512, 1024
    vec = lambda a: a.reshape(1, D_MODEL)
    for l in range(depth):
        last = l == depth - 1
        lam_init = 0.8 - 0.6 * math.exp(-0.3 * l)
        ml = [mod[l, :b, k * D_MODEL:(k + 1) * D_MODEL].reshape(b, 1, D_MODEL) for k in range(6)]
        mc = [jnp.broadcast_to(mod[l, b, k * D_MODEL:(k + 1) * D_MODEL], (b, 1, D_MODEL)) for k in range(6)]
        w_in_l = w_in[l].astype(BF16)
        w_out_l = w_out[l].astype(BF16)
        w1_l = w_ffn_in[l].astype(BF16)
        w2_l = w_ffn_out[l].astype(BF16)

        pc, vt_c, kn_c = _in_proj(xc, vec(g_pre_mix[l]), mc[0], mc[1], w_in_l, None, b, nc, nc)
        pt, vt_l, kn_l = _in_proj(xl, vec(g_pre_mix[l]), ml[0], ml[1], w_in_l, tabs, b, n, tm_proj)

        lru_p = _lru_params(lru_conv_w[l], lru_conv_b[l], lru_gate_w[l], lru_gate_b[l], lru_lambda[l])
        hf_c, hb_c, hfin = _lru(pc, b, nc, nc, *lru_p, jnp.zeros((b, 8, GW), F32))
        hf_l, hb_l, _ = _lru(pt, b, n, 1024, *lru_p, hfin)

        log_g = jax.nn.log_sigmoid(ret_decay[l].astype(F32))
        lgl = jnp.repeat(log_g, HEAD_DIM, axis=-1).reshape(2, 1, GW)
        lgh = jnp.broadcast_to(log_g.reshape(2 * N_HEADS, 1), (2 * N_HEADS, 128))
        of_c, ob_c, sfin = _ret(pc, b, nc, lgl, lgh, jnp.zeros((b, 2, GW, GW), F32))
        of_l, ob_l, _ = _ret(pt, b, n, lgl, lgh, sfin)

        ya_l = _na(pt, pc, _na_bias_tables(na_rpb[l]), b, n, nc)

        lq1, lk1, lq2, lk2 = diff_lambda[l].astype(F32)
        lam = jnp.exp(jnp.sum(lq1 * lk1)) - jnp.exp(jnp.sum(lq2 * lk2)) + lam_init
        lam_v = jnp.broadcast_to(lam, (1, GW)).astype(F32)
        g_sub = jnp.tile(diff_subln[l].astype(F32), N_HEADS).reshape(1, GW)
        yd_l = _diff_lat(pt, pc, vt_l, vt_c, jnp.maximum(kn_c, kn_l), lam_v, g_sub, b, n, nc, tm_proj, 1.0 - lam_init)

        x_mid = _out_proj(xl, ya_l, hf_l, hb_l, of_l, ob_l, yd_l, pt, w_out_l, vec(g_post_mix[l]), ml[2], b, n,
                          tm_proj)
        xl_new = _ffn(x_mid, vec(g_pre_ffn[l]), ml[3], ml[4], w1_l, w2_l, vec(g_post_ffn[l]), ml[5], b, n, tm)

        if not last:
            ya_c = _ctx_attn(pc, b, nc, C_NA_Q, C_NA_K, C_NA_V)
            yd_c = _ctx_attn(pc, b, nc, C_DF_Q, C_DF_K, C_DF_V, lam_v, g_sub, 1.0 - lam_init)
            xc_mid = _out_proj(xc, ya_c, hf_c, hb_c, of_c, ob_c, yd_c, pc, w_out_l, vec(g_post_mix[l]), mc[2],
                               b, nc, nc)
            xc = _ffn(xc_mid, vec(g_pre_ffn[l]), mc[3], mc[4], w1_l, w2_l, vec(g_post_ffn[l]), mc[5], b, nc, nc)
        xl = xl_new
    return xl.reshape(b, n, D_MODEL)
```

```python
import functools
import math

import numpy as np
import jax
import jax.numpy as jnp
from jax import lax
from jax.experimental import pallas as pl
from jax.experimental.pallas import tpu as pltpu

F32 = jnp.float32
BF16 = jnp.bfloat16

D_MODEL = 1024
GRID_W = 64
HEAD_DIM = 64
N_HEADS = 4
GW = N_HEADS * HEAD_DIM
N_PROJ = 12
N_MAPS = 2 * N_HEADS
MAP_DIM = HEAD_DIM // 2
V_AUG = HEAD_DIM + 16
D_FF = 2816
NA_WIN_ROWS = 8
NA_WIN_COLS = 16
LRU_C = 8.0
RET_CHUNK = 128
ROPE_BASE = 10000.0
EPS = 1e-6
NEG = -1e30

C_NA_Q, C_NA_K, C_NA_V, C_LRU_X, C_LRU_G, C_RET_Q, C_RET_K, C_RET_V, C_RET_G, C_DF_Q, C_DF_K, C_DF_V = range(12)
_COL_SCALE = {C_NA_Q: HEAD_DIM ** -0.5, C_RET_K: HEAD_DIM ** -0.5,
              C_DF_Q: (HEAD_DIM // 2) ** -0.5 * math.log2(math.e)}

LANES = 128
SUBLANES = 8
BF16_ROWS = 2 * SUBLANES
V7X_VMEM_LIMIT = 56 * 1024 * 1024

TILE_PROJ = 1024
TILE_FFN = 512
TILE_LRU = 1024
MOD_COL_TILE = 1536
LRU_HALO = BF16_ROWS


def _cparams(*sem):
    return pltpu.CompilerParams(dimension_semantics=sem, vmem_limit_bytes=V7X_VMEM_LIMIT)


def _dot(a, b):
    return jnp.dot(a, b, preferred_element_type=F32)


def _dot_nt(a, b):
    return lax.dot_general(a, b, (((1,), (1,)), ((), ())), preferred_element_type=F32)


def _rms(x, g):
    return x * lax.rsqrt(jnp.mean(x * x, axis=-1, keepdims=True) + EPS) * g


def _sigmoid(x):
    return 1.0 / (1.0 + jnp.exp(-x))


def _group_sum_sq(y, group):
    y2 = y * y
    hi = y2.astype(BF16)
    lo = (y2 - hi.astype(F32)).astype(BF16)
    r = lax.broadcasted_iota(jnp.int32, (GW, GW), 0) // group
    c = lax.broadcasted_iota(jnp.int32, (GW, GW), 1) // group
    bd = jnp.where(r == c, 1.0, 0.0).astype(BF16)
    return _dot(hi, bd) + _dot(lo, bd)


def _head_mean_sq(y):
    return _group_sum_sq(y, HEAD_DIM) * (1.0 / HEAD_DIM)


def _mod_kernel(c_ref, w_ref, b_ref, o_ref):
    c = c_ref[...]
    s = c * _sigmoid(c)
    o_ref[0] = jnp.dot(s, w_ref[0], preferred_element_type=F32, precision=lax.Precision.HIGHEST) + b_ref[0]


def _mod(cc, w_mod, b_mod):
    depth = w_mod.shape[0]
    r = cc.shape[0]
    tn = MOD_COL_TILE
    return pl.pallas_call(
        _mod_kernel,
        grid=(depth, 6 * D_MODEL // tn),
        in_specs=[pl.BlockSpec((r, D_MODEL), lambda l, j: (0, 0)),
                  pl.BlockSpec((1, D_MODEL, tn), lambda l, j: (l, 0, j)),
                  pl.BlockSpec((1, 1, tn), lambda l, j: (l, 0, j))],
        out_specs=pl.BlockSpec((1, r, tn), lambda l, j: (l, 0, j)),
        out_shape=jax.ShapeDtypeStruct((depth, r, 6 * D_MODEL), F32),
        compiler_params=_cparams("arbitrary", "arbitrary"),
        name="adaln_mod",
    )(cc, w_mod, b_mod.reshape(depth, 1, 6 * D_MODEL))


def _rope(p, cos, sin_signed, half):
    outs = []
    for c in range(GW // LANES):
        cols = slice(c * LANES, (c + 1) * LANES)
        xs = p[:, cols]
        lane = lax.broadcasted_iota(jnp.int32, xs.shape, 1)
        first = (lane % (2 * half)) < half
        partner = jnp.where(first, pltpu.roll(xs, LANES - half, 1), pltpu.roll(xs, half, 1))
        outs.append(xs * cos[:, cols] + partner * sin_signed[:, cols])
    return jnp.concatenate(outs, axis=1)


def _in_proj_kernel(*refs, rope):
    if rope:
        x_ref, g_ref, sh_ref, sc_ref, w_ref, tab_ref, o_ref, vt_ref, kn_ref = refs
    else:
        x_ref, g_ref, sh_ref, sc_ref, w_ref, o_ref, vt_ref, kn_ref = refs

    @pl.when(pl.program_id(1) == 0)
    def _():
        kn_ref[...] = jnp.zeros(kn_ref.shape, F32)

    h = _rms(x_ref[...], g_ref[...]) * (1.0 + sc_ref[0]) + sh_ref[0]
    hb = h.astype(BF16)
    for j in range(N_PROJ):
        p = _dot(hb, w_ref[:, j * GW:(j + 1) * GW])
        if j in _COL_SCALE:
            p = p * _COL_SCALE[j]
        if rope and j in (C_RET_Q, C_RET_K):
            p = _rope(p, tab_ref[0], tab_ref[1], HEAD_DIM // 2)
        if rope and j in (C_DF_Q, C_DF_K):
            p = _rope(p, tab_ref[2], tab_ref[3], HEAD_DIM // 4)
        pb = p.astype(BF16)
        o_ref[:, j * GW:(j + 1) * GW] = pb
        if j == C_DF_K:
            kf = pb.astype(F32)
            kn_ref[0] = jnp.maximum(kn_ref[0],
                                    jnp.broadcast_to(jnp.max(kf * kf, axis=0, keepdims=True), (SUBLANES, GW)))
        if j == C_DF_V:
            vt = p.T.astype(BF16)
            ones = jnp.ones((V_AUG - HEAD_DIM, vt.shape[1]), BF16)
            for hd in range(N_HEADS):
                vt_ref[0, hd * V_AUG:hd * V_AUG + HEAD_DIM, :] = vt[hd * HEAD_DIM:(hd + 1) * HEAD_DIM]
                vt_ref[0, hd * V_AUG + HEAD_DIM:(hd + 1) * V_AUG, :] = ones


def _in_proj(x2d, g, shift, scale, w, tabs, b, n, tm):
    nt = n // tm
    rope = tabs is not None
    in_specs = [pl.BlockSpec((tm, D_MODEL), lambda bi, i: (bi * nt + i, 0)),
                pl.BlockSpec((1, D_MODEL), lambda bi, i: (0, 0)),
                pl.BlockSpec((1, 1, D_MODEL), lambda bi, i: (bi, 0, 0)),
                pl.BlockSpec((1, 1, D_MODEL), lambda bi, i: (bi, 0, 0)),
                pl.BlockSpec((D_MODEL, N_PROJ * GW), lambda bi, i: (0, 0))]
    args = [x2d, g, shift, scale, w]
    if rope:
        in_specs.append(pl.BlockSpec((4, tm, GW), lambda bi, i: (0, i, 0)))
        args.append(tabs)
    return pl.pallas_call(
        functools.partial(_in_proj_kernel, rope=rope),
        grid=(b, nt),
        in_specs=in_specs,
        out_specs=[pl.BlockSpec((tm, N_PROJ * GW), lambda bi, i: (bi * nt + i, 0)),
                   pl.BlockSpec((1, N_HEADS * V_AUG, tm), lambda bi, i: (bi, 0, i)),
                   pl.BlockSpec((1, SUBLANES, GW), lambda bi, i: (bi, 0, 0))],
        out_shape=[jax.ShapeDtypeStruct((b * n, N_PROJ * GW), BF16),
                   jax.ShapeDtypeStruct((b, N_HEADS * V_AUG, n), BF16),
                   jax.ShapeDtypeStruct((b, SUBLANES, GW), F32)],
        compiler_params=_cparams("arbitrary", "arbitrary"),
        name="in_proj_rope" if rope else "in_proj",
    )(*args)


def _rope_tables(n):
    t = jnp.arange(n)
    row = (t // GRID_W).astype(F32)
    col = (t % GRID_W).astype(F32)

    def tab(dim):
        nf = dim // 4
        inv = ROPE_BASE ** (-jnp.arange(nf, dtype=F32) / nf)
        ang = jnp.concatenate([row[:, None] * inv, col[:, None] * inv], axis=-1)
        cos, sin = jnp.cos(ang), jnp.sin(ang)
        reps = GW // dim
        return (jnp.tile(jnp.concatenate([cos, cos], axis=-1), (1, reps)),
                jnp.tile(jnp.concatenate([-sin, sin], axis=-1), (1, reps)))

    cr, sr = tab(HEAD_DIM)
    cd, sd = tab(HEAD_DIM // 2)
    return jnp.stack([cr, sr, cd, sd])


def _lru_kernel(xf_ref, xfp_ref, xfn_ref, xb_ref, xbp_ref, xbn_ref, cw_ref, cb_ref, wg_ref, bg_ref, lam_ref,
                h0_ref, hf_ref, hb_ref, hfin_ref, a_scr, u_scr, hc_scr, *, tn, nt):
    i = pl.program_id(1)

    @pl.when(i == 0)
    def _():
        hc_scr[...] = h0_ref[0]

    cw = cw_ref[...]
    s8 = SUBLANES
    ng8 = tn // s8
    sub = lax.broadcasted_iota(jnp.int32, (ng8, s8, GW), 1)
    row = lax.broadcasted_iota(jnp.int32, (tn, GW), 0)
    row8 = row % s8

    def shifted(xm, k):
        rot = pltpu.roll(xm.reshape(ng8, s8, GW), (-k) % s8, 1).reshape(tn, GW)
        if k < 0:
            return jnp.where(row8 < -k, pltpu.roll(rot, s8, 0), rot)
        return jnp.where(row8 >= s8 - k, pltpu.roll(rot, tn - s8, 0), rot)

    def coeffs(x_ref, xp_ref, xn_ref, tile, d):
        xm = x_ref[...].astype(F32)
        prev = jnp.where(tile > 0, xp_ref[LRU_HALO - 1:LRU_HALO, :].astype(F32), 0.0)
        nxt = jnp.where(tile < nt - 1, xn_ref[0:2, :].astype(F32), 0.0)
        xm1 = jnp.where(row == 0, prev, shifted(xm, -1))
        xp1 = jnp.where(row == tn - 1, nxt[0:1], shifted(xm, 1))
        xp2 = jnp.where(row == tn - 2, nxt[0:1], shifted(xm, 2))
        xp2 = jnp.where(row == tn - 1, nxt[1:2], xp2)
        xb = cw[0:1] * xm1 + cw[1:2] * xm + cw[2:3] * xp1 + cw[3:4] * xp2 + cb_ref[...]
        tg = jnp.tanh(_dot(xb.astype(BF16), wg_ref[d]) + bg_ref[d]) + 1.0
        nl = -lam_ref[d]
        softplus = jnp.maximum(nl, 0.0) + jnp.log1p(jnp.exp(-jnp.abs(nl)))
        log_a = (-0.5 * LRU_C * softplus) * tg[:, :GW]
        a = jnp.exp(log_a)
        th = jnp.tanh(log_a)
        v = -0.5 * th
        u = v * lax.rsqrt(jnp.maximum(v, 1e-37)) * lax.rsqrt(1.0 - th) * (tg[:, GW:] * xb)

        a = a.reshape(ng8, s8, GW)
        u = u.reshape(ng8, s8, GW)
        for sft in (1, 2, 4):
            keep = sub >= sft if d == 0 else sub < s8 - sft
            rot = sft if d == 0 else s8 - sft
            a_s = jnp.where(keep, pltpu.roll(a, rot, 1), 1.0)
            u_s = jnp.where(keep, pltpu.roll(u, rot, 1), 0.0)
            u = a * u_s + u
            a = a * a_s
        a_scr[d] = a.reshape(tn, GW)
        u_scr[d] = u.reshape(tn, GW)

    coeffs(xf_ref, xfp_ref, xfn_ref, i, 0)
    coeffs(xb_ref, xbp_ref, xbn_ref, nt - 1 - i, 1)

    gs = LRU_HALO
    ng = tn // gs

    def body(g, carry):
        hf, hb = carry
        fb = pl.multiple_of(g * gs, gs)
        bb = pl.multiple_of((ng - 1 - g) * gs, gs)
        f1 = a_scr[0, pl.ds(fb, s8), :] * hf + u_scr[0, pl.ds(fb, s8), :]
        f2 = a_scr[0, pl.ds(fb + s8, s8), :] * f1[s8 - 1:s8] + u_scr[0, pl.ds(fb + s8, s8), :]
        b2 = a_scr[1, pl.ds(bb + s8, s8), :] * hb + u_scr[1, pl.ds(bb + s8, s8), :]
        b1 = a_scr[1, pl.ds(bb, s8), :] * b2[0:1] + u_scr[1, pl.ds(bb, s8), :]
        hf_ref[pl.ds(fb, gs), :] = jnp.concatenate([f1, f2], axis=0).astype(BF16)
        hb_ref[pl.ds(bb, gs), :] = jnp.concatenate([b1, b2], axis=0).astype(BF16)
        return f2[s8 - 1:s8], b1[0:1]

    hf, hb = lax.fori_loop(0, ng, body, (hc_scr[0:1, :], hc_scr[1:2, :]))
    hc_scr[0:1, :] = hf
    hc_scr[1:2, :] = hb

    @pl.when(i == nt - 1)
    def _():
        hfin_ref[0] = jnp.concatenate([hf, hb, jnp.zeros((s8 - 2, GW), F32)], axis=0)


def _lru(proj, b, n, tn, cw, cb, wg, bg, lam, h0):
    nt = n // tn
    hb8 = tn // LRU_HALO

    def main(rev):
        return pl.BlockSpec((tn, GW), lambda bi, i: (bi * nt + (nt - 1 - i if rev else i), C_LRU_X))

    def prev(rev):
        def im(bi, i):
            t = nt - 1 - i if rev else i
            return (jnp.maximum((bi * nt + t) * hb8 - 1, 0), C_LRU_X)
        return pl.BlockSpec((LRU_HALO, GW), im)

    def nxt(rev):
        def im(bi, i):
            t = nt - 1 - i if rev else i
            return (jnp.minimum((bi * nt + t + 1) * hb8, b * nt * hb8 - 1), C_LRU_X)
        return pl.BlockSpec((LRU_HALO, GW), im)

    const2 = lambda bi, i: (0, 0)
    const3 = lambda bi, i: (0, 0, 0)
    return pl.pallas_call(
        functools.partial(_lru_kernel, tn=tn, nt=nt),
        grid=(b, nt),
        in_specs=[main(False), prev(False), nxt(False), main(True), prev(True), nxt(True),
                  pl.BlockSpec((4, GW), const2), pl.BlockSpec((1, GW), const2),
                  pl.BlockSpec((2, GW, 2 * GW), const3), pl.BlockSpec((2, 1, 2 * GW), const3),
                  pl.BlockSpec((2, 1, GW), const3),
                  pl.BlockSpec((1, SUBLANES, GW), lambda bi, i: (bi, 0, 0))],
        out_specs=[pl.BlockSpec((tn, GW), lambda bi, i: (bi * nt + i, 0)),
                   pl.BlockSpec((tn, GW), lambda bi, i: (bi * nt + nt - 1 - i, 0)),
                   pl.BlockSpec((1, SUBLANES, GW), lambda bi, i: (bi, 0, 0))],
        out_shape=[jax.ShapeDtypeStruct((b * n, GW), BF16), jax.ShapeDtypeStruct((b * n, GW), BF16),
                   jax.ShapeDtypeStruct((b, SUBLANES, GW), F32)],
        scratch_shapes=[pltpu.VMEM((2, tn, GW), F32), pltpu.VMEM((2, tn, GW), F32),
                        pltpu.VMEM((SUBLANES, GW), F32)],
        compiler_params=_cparams("arbitrary", "arbitrary"),
        name="rglru",
    )(proj, proj, proj, proj, proj, proj, cw, cb, wg, bg, lam, h0)


def _ret_kernel(qf_ref, kf_ref, vf_ref, qb_ref, kb_ref, vb_ref, lgl_ref, lgh_ref, s0_ref,
                of_ref, ob_ref, sfin_ref, s_scr, dm_scr, qd_scr, kd_scr, *, nc, cps):
    c = pl.program_id(1)
    cs = RET_CHUNK

    @pl.when(c == 0)
    def _():
        s_scr[...] = s0_ref[0]
        ii = lax.broadcasted_iota(jnp.int32, (cs, cs), 0)
        jj = lax.broadcasted_iota(jnp.int32, (cs, cs), 1)
        t = lax.broadcasted_iota(jnp.int32, (cs, GW), 0).astype(F32)
        for d in range(2):
            rel = ii - jj if d == 0 else jj - ii - 1
            ok = rel >= 0
            relf = jnp.where(ok, rel, 0).astype(F32)
            for h in range(N_HEADS):
                lg = lgh_ref[d * N_HEADS + h:d * N_HEADS + h + 1, :]
                dm_scr[d, h] = jnp.where(ok, jnp.exp(lg * relf), 0.0)
            lgl = lgl_ref[d]
            if d == 0:
                qd_scr[d] = jnp.exp(lgl * (t + 1.0))
                kd_scr[d] = jnp.exp(lgl * (cs - 1.0 - t))
            else:
                qd_scr[d] = jnp.exp(lgl * (cs - 1.0 - t))
                kd_scr[d] = jnp.exp(lgl * t)

    lane = lax.broadcasted_iota(jnp.int32, (cs, GW), 1) // HEAD_DIM
    br = lax.broadcasted_iota(jnp.int32, (GW, GW), 0) // HEAD_DIM
    bc = lax.broadcasted_iota(jnp.int32, (GW, GW), 1) // HEAD_DIM

    def direction(d, q_ref, k_ref, v_ref, o_ref):
        s = s_scr[d]
        cdec = jnp.exp(lgl_ref[d] * float(cs))
        for step in range(cps):
            j = step if d == 0 else cps - 1 - step
            rs = slice(j * cs, (j + 1) * cs)
            k = k_ref[rs, :]
            v = v_ref[rs, :]
            q32 = q_ref[rs, :].astype(F32)
            v32 = v.astype(F32)
            parts = []
            vparts = []
            for h in range(N_HEADS):
                mk = lane == h
                qh = jnp.where(mk, q32, 0.0).astype(BF16)
                parts.append((_dot_nt(qh, k) * dm_scr[d, h]).astype(BF16))
                vparts.append(jnp.where(mk, v32, 0.0).astype(BF16))
            p = jnp.concatenate(parts, axis=1)
            vs = jnp.concatenate(vparts, axis=0)
            o = _dot(p, vs) + _dot((q32 * qd_scr[d]).astype(BF16), s.astype(BF16))
            o_ref[rs, :] = o.astype(BF16)
            kd = (k.astype(F32) * kd_scr[d]).T.astype(BF16)
            s = s * cdec + jnp.where(br == bc, _dot(kd, v), 0.0)
        s_scr[d] = s

    direction(0, qf_ref, kf_ref, vf_ref, of_ref)
    direction(1, qb_ref, kb_ref, vb_ref, ob_ref)

    @pl.when(c == nc - 1)
    def _():
        sfin_ref[0] = s_scr[...]


RET_CHUNKS_PER_STEP = 8


def _ret(proj, b, n, lgl, lgh, s0):
    cps = min(RET_CHUNKS_PER_STEP, n // RET_CHUNK)
    tb = cps * RET_CHUNK
    nc = n // tb

    def blk(col, rev):
        return pl.BlockSpec((tb, GW), lambda bi, c: (bi * nc + (nc - 1 - c if rev else c), col))

    return pl.pallas_call(
        functools.partial(_ret_kernel, nc=nc, cps=cps),
        grid=(b, nc),
        in_specs=[blk(C_RET_Q, False), blk(C_RET_K, False), blk(C_RET_V, False),
                  blk(C_RET_Q, True), blk(C_RET_K, True), blk(C_RET_V, True),
                  pl.BlockSpec((2, 1, GW), lambda bi, c: (0, 0, 0)),
                  pl.BlockSpec((2 * N_HEADS, LANES), lambda bi, c: (0, 0)),
                  pl.BlockSpec((1, 2, GW, GW), lambda bi, c: (bi, 0, 0, 0))],
        out_specs=[pl.BlockSpec((tb, GW), lambda bi, c: (bi * nc + c, 0)),
                   pl.BlockSpec((tb, GW), lambda bi, c: (bi * nc + nc - 1 - c, 0)),
                   pl.BlockSpec((1, 2, GW, GW), lambda bi, c: (bi, 0, 0, 0))],
        out_shape=[jax.ShapeDtypeStruct((b * n, GW), BF16), jax.ShapeDtypeStruct((b * n, GW), BF16),
                   jax.ShapeDtypeStruct((b, 2, GW, GW), F32)],
        scratch_shapes=[pltpu.VMEM((2, GW, GW), F32), pltpu.VMEM((2, N_HEADS, RET_CHUNK, RET_CHUNK), F32),
                        pltpu.VMEM((2, RET_CHUNK, GW), F32), pltpu.VMEM((2, RET_CHUNK, GW), F32)],
        compiler_params=_cparams("arbitrary", "arbitrary"),
        name="retention",
    )(proj, proj, proj, proj, proj, proj, lgl, lgh, s0)


def _na_kernel(q_ref, k_ref, v_ref, kc_ref, vc_ref, bias_ref, o_ref, *, rows, rb):
    i0 = pl.program_id(1) * rb
    nwin = NA_WIN_ROWS * GRID_W
    kc = kc_ref[...]
    vc = vc_ref[...]
    lane = lax.broadcasted_iota(jnp.int32, (GRID_W, GW), 1) // HEAD_DIM

    def row(i, carry):
        r = i0 + i
        r0 = jnp.clip(r - NA_WIN_ROWS // 2, 0, rows - NA_WIN_ROWS)
        start = pl.multiple_of(r0 * GRID_W, GRID_W)
        qrow = pl.multiple_of(i * GRID_W, GRID_W)
        kw = k_ref[pl.ds(start, nwin), :]
        vw = v_ref[pl.ds(start, nwin), :]
        q32 = q_ref[pl.ds(qrow, GRID_W), :].astype(F32)
        qs = jnp.concatenate([jnp.where(lane == h, q32, 0.0) for h in range(N_HEADS)], axis=0).astype(BF16)
        sw = _dot_nt(qs, kw) + bias_ref[r0 - r + NA_WIN_ROWS - 1]
        sc = _dot_nt(qs, kc)
        m = jnp.maximum(jnp.max(sw, axis=-1, keepdims=True), jnp.max(sc, axis=-1, keepdims=True))
        ew = jnp.exp(sw - m)
        ec = jnp.exp(sc - m)
        l = jnp.sum(ew, axis=-1, keepdims=True) + jnp.sum(ec, axis=-1, keepdims=True)
        o = (_dot(ew.astype(BF16), vw) + _dot(ec.astype(BF16), vc)) * (1.0 / l)
        out = jnp.zeros((GRID_W, GW), F32)
        for h in range(N_HEADS):
            out = jnp.where(lane == h, o[h * GRID_W:(h + 1) * GRID_W], out)
        o_ref[pl.ds(qrow, GRID_W), :] = out.astype(BF16)
        return carry

    lax.fori_loop(0, rb, row, 0, unroll=NA_UNROLL)


NA_ROWS_PER_STEP = 16
NA_UNROLL = 16


def _na(proj_l, proj_c, bias, b, n, c):
    rows = n // GRID_W
    rb = NA_ROWS_PER_STEP
    nrb = rows // rb
    tq = rb * GRID_W
    return pl.pallas_call(
        functools.partial(_na_kernel, rows=rows, rb=rb),
        grid=(b, nrb),
        in_specs=[pl.BlockSpec((tq, GW), lambda bi, i: (bi * nrb + i, C_NA_Q)),
                  pl.BlockSpec((n, GW), lambda bi, i: (bi, C_NA_K)),
                  pl.BlockSpec((n, GW), lambda bi, i: (bi, C_NA_V)),
                  pl.BlockSpec((c, GW), lambda bi, i: (bi, C_NA_K)),
                  pl.BlockSpec((c, GW), lambda bi, i: (bi, C_NA_V)),
                  pl.BlockSpec((NA_WIN_ROWS, N_HEADS * GRID_W, NA_WIN_ROWS * GRID_W), lambda bi, i: (0, 0, 0))],
        out_specs=pl.BlockSpec((tq, GW), lambda bi, i: (bi * nrb + i, 0)),
        out_shape=jax.ShapeDtypeStruct((b * n, GW), BF16),
        compiler_params=_cparams("arbitrary", "arbitrary"),
        name="na_attention",
    )(proj_l, proj_l, proj_l, proj_c, proj_c, bias)


def _na_bias_tables(rpb):
    nr, ncol = 2 * NA_WIN_ROWS - 1, 2 * NA_WIN_COLS - 1
    cq = np.arange(GRID_W)[:, None]
    ck = np.arange(GRID_W)[None, :]
    cstart = np.clip(cq - NA_WIN_COLS // 2, 0, GRID_W - NA_WIN_COLS)
    okc = (ck >= cstart) & (ck < cstart + NA_WIN_COLS)
    dc = np.clip(ck - cq + NA_WIN_COLS - 1, 0, ncol - 1)
    ohc = (dc[..., None] == np.arange(ncol)) & okc[..., None]
    dr = np.arange(NA_WIN_ROWS)[:, None] + np.arange(NA_WIN_ROWS)[None, :]
    ohr = dr[..., None] == np.arange(nr)
    bias = jnp.einsum('vjr,hrd,ckd->vhcjk', jnp.asarray(ohr, F32), rpb.astype(F32), jnp.asarray(ohc, F32),
                      precision=lax.Precision.HIGHEST)
    bias = jnp.where(jnp.asarray(okc)[None, None, :, None, :], bias, NEG)
    return bias.reshape(NA_WIN_ROWS, N_HEADS * GRID_W, NA_WIN_ROWS * GRID_W)


def _ctx_attn_kernel(*refs, diff, out_scale):
    if diff:
        q_ref, k_ref, v_ref, lam_ref, g_ref, o_ref = refs
    else:
        q_ref, k_ref, v_ref, o_ref = refs
    k = k_ref[...]
    v = v_ref[...]
    q32 = q_ref[...].astype(F32)
    lane = lax.broadcasted_iota(jnp.int32, q32.shape, 1)

    def softmax_pv(mk):
        s = _dot_nt(jnp.where(mk, q32, 0.0).astype(BF16), k)
        z = s - jnp.max(s, axis=-1, keepdims=True)
        e = jnp.exp2(z) if diff else jnp.exp(z)
        return _dot(e.astype(BF16), v) * (1.0 / jnp.sum(e, axis=-1, keepdims=True))

    out = jnp.zeros(q32.shape, F32)
    for h in range(N_HEADS):
        if diff:
            o = (softmax_pv(lane // (HEAD_DIM // 2) == 2 * h)
                 - lam_ref[...] * softmax_pv(lane // (HEAD_DIM // 2) == 2 * h + 1))
        else:
            o = softmax_pv(lane // HEAD_DIM == h)
        out = jnp.where(lane // HEAD_DIM == h, o, out)
    if diff:
        out = out * lax.rsqrt(_head_mean_sq(out) + EPS) * g_ref[...] * out_scale
    o_ref[...] = out.astype(BF16)


def _ctx_attn(proj_c, b, c, qcol, kcol, vcol, lam=None, g=None, out_scale=1.0):
    diff = lam is not None
    in_specs = [pl.BlockSpec((c, GW), lambda bi: (bi, qcol)),
                pl.BlockSpec((c, GW), lambda bi: (bi, kcol)),
                pl.BlockSpec((c, GW), lambda bi: (bi, vcol))]
    args = [proj_c, proj_c, proj_c]
    if diff:
        in_specs += [pl.BlockSpec((1, GW), lambda bi: (0, 0)), pl.BlockSpec((1, GW), lambda bi: (0, 0))]
        args += [lam, g]
    return pl.pallas_call(
        functools.partial(_ctx_attn_kernel, diff=diff, out_scale=out_scale),
        grid=(b,),
        in_specs=in_specs,
        out_specs=pl.BlockSpec((c, GW), lambda bi: (bi, 0)),
        out_shape=jax.ShapeDtypeStruct((b * c, GW), BF16),
        compiler_params=_cparams("arbitrary"),
        name="ctx_diff_attention" if diff else "ctx_attention",
    )(*args)


DIFF_KEY_TILES = (2816, 768, 256)
DIFF_HEADROOM = 100.0
DIFF_MIN_SUM = 2.0 ** -100


def _diff_lat_kernel(q_ref, kc_ref, vtc_ref, kl_ref, vtl_ref, kn_ref, lam_ref, g_ref, o_ref,
                     k_scr, vt_scr, qm_scr, sh_scr, m_scr, acc_scr, *, tq, tk, n_ctx, n_lat, out_scale):
    nt = (n_ctx + n_lat) // tk

    @pl.when(pl.program_id(1) == 0)
    def _():
        k_scr[0:n_ctx, :] = kc_ref[...]
        k_scr[n_ctx:, :] = kl_ref[...]
        vt_scr[:, 0:n_ctx] = vtc_ref[0]
        vt_scr[:, n_ctx:] = vtl_ref[0]

    q32 = q_ref[...].astype(F32)
    qt = q32.T
    rowg = lax.broadcasted_iota(jnp.int32, (GW, tq), 0) // MAP_DIM
    for u in range(N_MAPS):
        qm_scr[u] = jnp.where(rowg == u, qt, 0.0).astype(BF16)
    kmax = jnp.sqrt(_group_sum_sq(jnp.sqrt(kn_ref[0]), MAP_DIM))[0:1, :]
    bound = jnp.sqrt(_group_sum_sq(q32 * kmax, MAP_DIM)) * (1.0 + 2.0 ** -8)
    sh_scr[...] = bound.T - DIFF_HEADROOM
    acc_scr[...] = jnp.zeros(acc_scr.shape, F32)

    def tile(t, u):
        st = pl.multiple_of(t * tk, tk)
        h = u // 2
        return k_scr[pl.ds(st, tk), :], vt_scr[h * V_AUG:(h + 1) * V_AUG, pl.ds(st, tk)]

    def fast(t, carry):
        for u in range(N_MAPS):
            kt, vt = tile(t, u)
            s = _dot(kt, qm_scr[u])
            e = jnp.exp2(s - sh_scr[u * MAP_DIM:u * MAP_DIM + 1, :]).astype(BF16)
            acc_scr[u] = acc_scr[u] + _dot(vt, e)
        return carry

    lax.fori_loop(0, nt, fast, 0)

    lmin = acc_scr[0, HEAD_DIM:HEAD_DIM + 1, :]
    for u in range(1, N_MAPS):
        lmin = jnp.minimum(lmin, acc_scr[u, HEAD_DIM:HEAD_DIM + 1, :])

    @pl.when(jnp.logical_not(jnp.min(lmin) >= DIFF_MIN_SUM))
    def _():
        m_scr[...] = jnp.full(m_scr.shape, NEG, F32)
        acc_scr[...] = jnp.zeros(acc_scr.shape, F32)

        def exact(t, carry):
            for u in range(N_MAPS):
                kt, vt = tile(t, u)
                s = _dot(kt, qm_scr[u])
                mo = m_scr[u:u + 1, :]
                mn = jnp.maximum(mo, jnp.max(s, axis=0, keepdims=True))
                m_scr[u:u + 1, :] = mn
                e = jnp.exp2(s - mn).astype(BF16)
                acc_scr[u] = jnp.exp2(mo - mn) * acc_scr[u] + _dot(vt, e)
            return carry

        lax.fori_loop(0, nt, exact, 0)

    lam = lam_ref[0:1, 0:1]
    outs = []
    for h in range(N_HEADS):
        a1 = acc_scr[2 * h]
        a2 = acc_scr[2 * h + 1]
        o1 = a1[:HEAD_DIM] * (1.0 / a1[HEAD_DIM:HEAD_DIM + 1])
        o2 = a2[:HEAD_DIM] * (1.0 / a2[HEAD_DIM:HEAD_DIM + 1])
        outs.append(o1 - lam * o2)
    out = jnp.concatenate(outs, axis=0).T
    out = out * lax.rsqrt(_head_mean_sq(out) + EPS) * g_ref[...] * out_scale
    o_ref[...] = out.astype(BF16)


def _diff_lat(proj_l, proj_c, vt_l, vt_c, kn2, lam, g, b, n, c, tq, out_scale):
    nt = n // tq
    nk = c + n
    tk = next(t for t in DIFF_KEY_TILES if nk % t == 0)
    once = pl.Buffered(1)
    return pl.pallas_call(
        functools.partial(_diff_lat_kernel, tq=tq, tk=tk, n_ctx=c, n_lat=n, out_scale=out_scale),
        grid=(b, nt),
        in_specs=[pl.BlockSpec((tq, GW), lambda bi, i: (bi * nt + i, C_DF_Q)),
                  pl.BlockSpec((c, GW), lambda bi, i: (bi, C_DF_K)),
                  pl.BlockSpec((1, N_HEADS * V_AUG, c), lambda bi, i: (bi, 0, 0)),
                  pl.BlockSpec((n, GW), lambda bi, i: (bi, C_DF_K), pipeline_mode=once),
                  pl.BlockSpec((1, N_HEADS * V_AUG, n), lambda bi, i: (bi, 0, 0), pipeline_mode=once),
                  pl.BlockSpec((1, SUBLANES, GW), lambda bi, i: (bi, 0, 0)),
                  pl.BlockSpec((1, GW), lambda bi, i: (0, 0)),
                  pl.BlockSpec((1, GW), lambda bi, i: (0, 0))],
        out_specs=pl.BlockSpec((tq, GW), lambda bi, i: (bi * nt + i, 0)),
        out_shape=jax.ShapeDtypeStruct((b * n, GW), BF16),
        scratch_shapes=[pltpu.VMEM((nk, GW), BF16), pltpu.VMEM((N_HEADS * V_AUG, nk), BF16),
                        pltpu.VMEM((N_MAPS, GW, tq), BF16), pltpu.VMEM((GW, tq), F32),
                        pltpu.VMEM((N_MAPS, tq), F32), pltpu.VMEM((N_MAPS, V_AUG, tq), F32)],
        compiler_params=_cparams("arbitrary", "arbitrary"),
        name="diff_attention_lat",
    )(proj_l, proj_c, vt_c, proj_l, vt_l, kn2, lam, g)


def _out_proj_kernel(x_ref, ya_ref, hf_ref, hb_ref, gl_ref, of_ref, ob_ref, gr_ref, yd_ref, w_ref, g_ref, gate_ref,
                     o_ref):
    gl = gl_ref[...].astype(F32)
    gelu = 0.5 * gl * (1.0 + jnp.tanh(math.sqrt(2.0 / math.pi) * (gl + 0.044715 * (gl * gl * gl))))
    yb = (hf_ref[...].astype(F32) + hb_ref[...].astype(F32)) * gelu
    r = of_ref[...].astype(F32) + ob_ref[...].astype(F32)
    gr = gr_ref[...].astype(F32)
    yc = r * lax.rsqrt(_head_mean_sq(r) + EPS) * (gr * _sigmoid(gr))
    y = (_dot(ya_ref[...], w_ref[0:GW, :]) + _dot(yb.astype(BF16), w_ref[GW:2 * GW, :])
         + _dot(yc.astype(BF16), w_ref[2 * GW:3 * GW, :]) + _dot(yd_ref[...], w_ref[3 * GW:4 * GW, :]))
    o_ref[...] = x_ref[...] + gate_ref[0] * _rms(y, g_ref[...])


def _out_proj(x2d, ya, hf, hb, of, ob, yd, proj, w, g, gate, b, n, tm):
    nt = n // tm
    tok = lambda col: pl.BlockSpec((tm, GW), lambda bi, i: (bi * nt + i, col))
    return pl.pallas_call(
        _out_proj_kernel,
        grid=(b, nt),
        in_specs=[pl.BlockSpec((tm, D_MODEL), lambda bi, i: (bi * nt + i, 0)),
                  tok(0), tok(0), tok(0), tok(C_LRU_G), tok(0), tok(0), tok(C_RET_G), tok(0),
                  pl.BlockSpec((D_MODEL, D_MODEL), lambda bi, i: (0, 0)),
                  pl.BlockSpec((1, D_MODEL), lambda bi, i: (0, 0)),
                  pl.BlockSpec((1, 1, D_MODEL), lambda bi, i: (bi, 0, 0))],
        out_specs=pl.BlockSpec((tm, D_MODEL), lambda bi, i: (bi * nt + i, 0)),
        out_shape=jax.ShapeDtypeStruct((b * n, D_MODEL), F32),
        compiler_params=_cparams("arbitrary", "arbitrary"),
        name="out_proj",
    )(x2d, ya, hf, hb, proj, of, ob, proj, yd, w, g, gate)


FFN_CHUNK = 256


def _ffn_kernel(x_ref, g1_ref, sh_ref, sc_ref, w1_ref, w2_ref, g2_ref, gate_ref, o_ref):
    x = x_ref[...]
    h = (_rms(x, g1_ref[...]) * (1.0 + sc_ref[0]) + sh_ref[0]).astype(BF16)
    acc = jnp.zeros(x.shape, F32)
    for c in range(D_FF // FFN_CHUNK):
        lo, hi = c * FFN_CHUNK, (c + 1) * FFN_CHUNK
        gt = _dot(h, w1_ref[:, lo:hi])
        up = _dot(h, w1_ref[:, D_FF + lo:D_FF + hi])
        act = (gt * _sigmoid(gt) * up).astype(BF16)
        acc = acc + _dot(act, w2_ref[lo:hi, :])
    o_ref[...] = x + gate_ref[0] * _rms(acc, g2_ref[...])


def _ffn(x2d, g1, shift, scale, w1, w2, g2, gate, b, n, tm):
    nt = n // tm
    vec = pl.BlockSpec((1, D_MODEL), lambda bi, i: (0, 0))
    per_b = pl.BlockSpec((1, 1, D_MODEL), lambda bi, i: (bi, 0, 0))
    return pl.pallas_call(
        _ffn_kernel,
        grid=(b, nt),
        in_specs=[pl.BlockSpec((tm, D_MODEL), lambda bi, i: (bi * nt + i, 0)), vec, per_b, per_b,
                  pl.BlockSpec((D_MODEL, 2 * D_FF), lambda bi, i: (0, 0)),
                  pl.BlockSpec((D_FF, D_MODEL), lambda bi, i: (0, 0)), vec, per_b],
        out_specs=pl.BlockSpec((tm, D_MODEL), lambda bi, i: (bi * nt + i, 0)),
        out_shape=jax.ShapeDtypeStruct((b * n, D_MODEL), F32),
        compiler_params=_cparams("arbitrary", "arbitrary"),
        name="ffn",
    )(x2d, g1, shift, scale, w1, w2, g2, gate)


def _blockdiag(w):
    nb, bs, _ = w.shape
    return jnp.einsum('kcd,kj->kcjd', w, jnp.eye(nb, dtype=w.dtype)).reshape(nb * bs, nb * bs)


def _lru_params(conv_w, conv_b, gate_w, gate_b, lam):
    wg = jnp.stack([jnp.concatenate([_blockdiag(gate_w[d, 0]), _blockdiag(gate_w[d, 1])], axis=1)
                    for d in range(2)])
    bg = jnp.stack([jnp.concatenate([gate_b[d, 0].reshape(1, GW), gate_b[d, 1].reshape(1, GW)], axis=1)
                    for d in range(2)])
    return conv_w, conv_b.reshape(1, GW), (0.5 * wg).astype(BF16), 0.5 * bg, lam.reshape(2, 1, GW)


def kernel(x, c, ctx, c_ctx, w_mod, b_mod, g_pre_mix, g_post_mix, g_pre_ffn, g_post_ffn, w_in, na_rpb, lru_conv_w,
           lru_conv_b, lru_gate_w, lru_gate_b, lru_lambda, ret_decay, diff_lambda, diff_subln, w_out, w_ffn_in,
           w_ffn_out):
    b, n, _ = x.shape
    nc = ctx.shape[1]
    depth = w_mod.shape[0]
    assert n % TILE_PROJ == 0 and n % TILE_LRU == 0 and nc % RET_CHUNK == 0

    r = -(-(b + 1) // SUBLANES) * SUBLANES
    cc = jnp.zeros((r, D_MODEL), F32).at[:b].set(c).at[b].set(c_ctx)
    mod = _mod(cc, w_mod, b_mod)
    tabs = _rope_tables(n)

    xl = x.reshape(b * n, D_MODEL)
    xc = ctx.reshape(b * nc, D_MODEL)
    tm, tm_proj = TILE_FFN, TILE_PROJ
    vec = lambda a: a.reshape(1, D_MODEL)
    for l in range(depth):
        last = l == depth - 1
        lam_init = 0.8 - 0.6 * math.exp(-0.3 * l)
        ml = [mod[l, :b, k * D_MODEL:(k + 1) * D_MODEL].reshape(b, 1, D_MODEL) for k in range(6)]
        mc = [jnp.broadcast_to(mod[l, b, k * D_MODEL:(k + 1) * D_MODEL], (b, 1, D_MODEL)) for k in range(6)]
        w_in_l = w_in[l].astype(BF16)
        w_out_l = w_out[l].astype(BF16)
        w1_l = w_ffn_in[l].astype(BF16)
        w2_l = w_ffn_out[l].astype(BF16)

        pc, vt_c, kn_c = _in_proj(xc, vec(g_pre_mix[l]), mc[0], mc[1], w_in_l, None, b, nc, nc)
        pt, vt_l, kn_l = _in_proj(xl, vec(g_pre_mix[l]), ml[0], ml[1], w_in_l, tabs, b, n, tm_proj)

        lru_p = _lru_params(lru_conv_w[l], lru_conv_b[l], lru_gate_w[l], lru_gate_b[l], lru_lambda[l])
        hf_c, hb_c, hfin = _lru(pc, b, nc, nc, *lru_p, jnp.zeros((b, SUBLANES, GW), F32))
        hf_l, hb_l, _ = _lru(pt, b, n, TILE_LRU, *lru_p, hfin)

        log_g = jax.nn.log_sigmoid(ret_decay[l].astype(F32))
        lgl = jnp.repeat(log_g, HEAD_DIM, axis=-1).reshape(2, 1, GW)
        lgh = jnp.broadcast_to(log_g.reshape(2 * N_HEADS, 1), (2 * N_HEADS, LANES))
        of_c, ob_c, sfin = _ret(pc, b, nc, lgl, lgh, jnp.zeros((b, 2, GW, GW), F32))
        of_l, ob_l, _ = _ret(pt, b, n, lgl, lgh, sfin)

        ya_l = _na(pt, pc, _na_bias_tables(na_rpb[l]), b, n, nc)

        lq1, lk1, lq2, lk2 = diff_lambda[l].astype(F32)
        lam = jnp.exp(jnp.sum(lq1 * lk1)) - jnp.exp(jnp.sum(lq2 * lk2)) + lam_init
        lam_v = jnp.broadcast_to(lam, (1, GW)).astype(F32)
        g_sub = jnp.tile(diff_subln[l].astype(F32), N_HEADS).reshape(1, GW)
        yd_l = _diff_lat(pt, pc, vt_l, vt_c, jnp.maximum(kn_c, kn_l), lam_v, g_sub, b, n, nc, tm_proj, 1.0 - lam_init)

        x_mid = _out_proj(xl, ya_l, hf_l, hb_l, of_l, ob_l, yd_l, pt, w_out_l, vec(g_post_mix[l]), ml[2], b, n,
                          tm_proj)
        xl_new = _ffn(x_mid, vec(g_pre_ffn[l]), ml[3], ml[4], w1_l, w2_l, vec(g_post_ffn[l]), ml[5], b, n, tm)

        if not last:
            ya_c = _ctx_attn(pc, b, nc, C_NA_Q, C_NA_K, C_NA_V)
            yd_c = _ctx_attn(pc, b, nc, C_DF_Q, C_DF_K, C_DF_V, lam_v, g_sub, 1.0 - lam_init)
            xc_mid = _out_proj(xc, ya_c, hf_c, hb_c, of_c, ob_c, yd_c, pc, w_out_l, vec(g_post_mix[l]), mc[2],
                               b, nc, nc)
            xc = _ffn(xc_mid, vec(g_pre_ffn[l]), mc[3], mc[4], w1_l, w2_l, vec(g_post_ffn[l]), mc[5], b, nc, nc)
        xl = xl_new
    return xl.reshape(b, n, D_MODEL)
```

```python
import functools
import math

import numpy as np
import jax
import jax.numpy as jnp
from jax import lax
from jax.experimental import pallas as pl
from jax.experimental.pallas import tpu as pltpu

F32 = jnp.float32
BF16 = jnp.bfloat16

D_MODEL = 1024
GRID_W = 64
HEAD_DIM = 64
N_HEADS = 4
GW = N_HEADS * HEAD_DIM
N_PROJ = 12
N_MAPS = 2 * N_HEADS
MAP_DIM = HEAD_DIM // 2
V_AUG = HEAD_DIM + 16
D_FF = 2816
NA_WIN_ROWS = 8
NA_WIN_COLS = 16
LRU_C = 8.0
RET_CHUNK = 128
ROPE_BASE = 10000.0
EPS = 1e-6
NEG = -1e30

C_NA_Q, C_NA_K, C_NA_V, C_LRU_X, C_LRU_G, C_RET_Q, C_RET_K, C_RET_V, C_RET_G, C_DF_Q, C_DF_K, C_DF_V = range(12)
_COL_SCALE = {C_NA_Q: HEAD_DIM ** -0.5, C_RET_K: HEAD_DIM ** -0.5,
              C_DF_Q: (HEAD_DIM // 2) ** -0.5 * math.log2(math.e)}

LANES = 128
SUBLANES = 8
BF16_ROWS = 2 * SUBLANES
V7X_VMEM_LIMIT = 56 * 1024 * 1024

TILE_PROJ = 1024
TILE_FFN = 512
TILE_LRU = 1024
MOD_COL_TILE = 1536
LRU_HALO = BF16_ROWS


def _cparams(*sem):
    return pltpu.CompilerParams(dimension_semantics=sem, vmem_limit_bytes=V7X_VMEM_LIMIT)


def _dot(a, b):
    return jnp.dot(a, b, preferred_element_type=F32)


def _dot_nt(a, b):
    return lax.dot_general(a, b, (((1,), (1,)), ((), ())), preferred_element_type=F32)


def _rms(x, g):
    return x * lax.rsqrt(jnp.mean(x * x, axis=-1, keepdims=True) + EPS) * g


def _sigmoid(x):
    return 1.0 / (1.0 + jnp.exp(-x))


def _group_sum_sq(y, group):
    y2 = y * y
    hi = y2.astype(BF16)
    lo = (y2 - hi.astype(F32)).astype(BF16)
    r = lax.broadcasted_iota(jnp.int32, (GW, GW), 0) // group
    c = lax.broadcasted_iota(jnp.int32, (GW, GW), 1) // group
    bd = jnp.where(r == c, 1.0, 0.0).astype(BF16)
    return _dot(hi, bd) + _dot(lo, bd)


def _head_mean_sq(y):
    return _group_sum_sq(y, HEAD_DIM) * (1.0 / HEAD_DIM)


def _mod_kernel(c_ref, w_ref, b_ref, o_ref):
    c = c_ref[...]
    s = c * _sigmoid(c)
    o_ref[0] = jnp.dot(s, w_ref[0], preferred_element_type=F32, precision=lax.Precision.HIGHEST) + b_ref[0]


def _mod(cc, w_mod, b_mod):
    depth = w_mod.shape[0]
    r = cc.shape[0]
    tn = MOD_COL_TILE
    return pl.pallas_call(
        _mod_kernel,
        grid=(depth, 6 * D_MODEL // tn),
        in_specs=[pl.BlockSpec((r, D_MODEL), lambda l, j: (0, 0)),
                  pl.BlockSpec((1, D_MODEL, tn), lambda l, j: (l, 0, j)),
                  pl.BlockSpec((1, 1, tn), lambda l, j: (l, 0, j))],
        out_specs=pl.BlockSpec((1, r, tn), lambda l, j: (l, 0, j)),
        out_shape=jax.ShapeDtypeStruct((depth, r, 6 * D_MODEL), F32),
        compiler_params=_cparams("arbitrary", "arbitrary"),
        name="adaln_mod",
    )(cc, w_mod, b_mod.reshape(depth, 1, 6 * D_MODEL))


def _rope(p, cos, sin_signed, half):
    outs = []
    for c in range(GW // LANES):
        xs = p[:, c * LANES:(c + 1) * LANES]
        lane = lax.broadcasted_iota(jnp.int32, xs.shape, 1)
        first = (lane % (2 * half)) < half
        partner = jnp.where(first, pltpu.roll(xs, LANES - half, 1), pltpu.roll(xs, half, 1))
        outs.append(xs * cos + partner * sin_signed)
    return jnp.concatenate(outs, axis=1)


def _in_proj_kernel(*refs, rope):
    if rope:
        x_ref, g_ref, sh_ref, sc_ref, w_ref, tab_ref, o_ref, vt_ref, kn_ref = refs
    else:
        x_ref, g_ref, sh_ref, sc_ref, w_ref, o_ref, vt_ref, kn_ref = refs

    @pl.when(pl.program_id(1) == 0)
    def _():
        kn_ref[...] = jnp.zeros(kn_ref.shape, F32)

    h = _rms(x_ref[...], g_ref[...]) * (1.0 + sc_ref[0]) + sh_ref[0]
    hb = h.astype(BF16)
    for j in range(N_PROJ):
        p = _dot(hb, w_ref[:, j * GW:(j + 1) * GW])
        if j in _COL_SCALE:
            p = p * _COL_SCALE[j]
        if rope and j in (C_RET_Q, C_RET_K):
            p = _rope(p, tab_ref[0], tab_ref[1], HEAD_DIM // 2)
        if rope and j in (C_DF_Q, C_DF_K):
            p = _rope(p, tab_ref[2], tab_ref[3], HEAD_DIM // 4)
        pb = p.astype(BF16)
        o_ref[:, j * GW:(j + 1) * GW] = pb
        if j == C_DF_K:
            kf = pb.astype(F32)
            kn_ref[0] = jnp.maximum(kn_ref[0],
                                    jnp.broadcast_to(jnp.max(kf * kf, axis=0, keepdims=True), (SUBLANES, GW)))
        if j == C_DF_V:
            vt = p.T.astype(BF16)
            ones = jnp.ones((V_AUG - HEAD_DIM, vt.shape[1]), BF16)
            for hd in range(N_HEADS):
                vt_ref[0, hd * V_AUG:hd * V_AUG + HEAD_DIM, :] = vt[hd * HEAD_DIM:(hd + 1) * HEAD_DIM]
                vt_ref[0, hd * V_AUG + HEAD_DIM:(hd + 1) * V_AUG, :] = ones


def _in_proj(x2d, g, shift, scale, w, layer, tabs, b, n, tm):
    nt = n // tm
    rope = tabs is not None
    in_specs = [pl.BlockSpec((tm, D_MODEL), lambda bi, i: (bi * nt + i, 0)),
                pl.BlockSpec((1, D_MODEL), lambda bi, i: (0, 0)),
                pl.BlockSpec((1, 1, D_MODEL), lambda bi, i: (bi, 0, 0)),
                pl.BlockSpec((1, 1, D_MODEL), lambda bi, i: (bi, 0, 0)),
                pl.BlockSpec((None, D_MODEL, N_PROJ * GW), lambda bi, i: (layer, 0, 0))]
    args = [x2d, g, shift, scale, w]
    if rope:
        in_specs.append(pl.BlockSpec((4, tm, LANES), lambda bi, i: (0, i, 0)))
        args.append(tabs)
    return pl.pallas_call(
        functools.partial(_in_proj_kernel, rope=rope),
        grid=(b, nt),
        in_specs=in_specs,
        out_specs=[pl.BlockSpec((tm, N_PROJ * GW), lambda bi, i: (bi * nt + i, 0)),
                   pl.BlockSpec((1, N_HEADS * V_AUG, tm), lambda bi, i: (bi, 0, i)),
                   pl.BlockSpec((1, SUBLANES, GW), lambda bi, i: (bi, 0, 0))],
        out_shape=[jax.ShapeDtypeStruct((b * n, N_PROJ * GW), BF16),
                   jax.ShapeDtypeStruct((b, N_HEADS * V_AUG, n), BF16),
                   jax.ShapeDtypeStruct((b, SUBLANES, GW), F32)],
        compiler_params=_cparams("arbitrary", "arbitrary"),
        name="in_proj_rope" if rope else "in_proj",
    )(*args)


def _rope_tables(n):
    t = jnp.arange(n)
    row = (t // GRID_W).astype(F32)
    col = (t % GRID_W).astype(F32)

    def tab(dim):
        nf = dim // 4
        inv = ROPE_BASE ** (-jnp.arange(nf, dtype=F32) / nf)
        ang = jnp.concatenate([row[:, None] * inv, col[:, None] * inv], axis=-1)
        cos, sin = jnp.cos(ang), jnp.sin(ang)
        reps = LANES // dim
        return (jnp.tile(jnp.concatenate([cos, cos], axis=-1), (1, reps)),
                jnp.tile(jnp.concatenate([-sin, sin], axis=-1), (1, reps)))

    cr, sr = tab(HEAD_DIM)
    cd, sd = tab(HEAD_DIM // 2)
    return jnp.stack([cr, sr, cd, sd])


def _lru_kernel(xf_ref, xfp_ref, xfn_ref, xb_ref, xbp_ref, xbn_ref, cw_ref, cb_ref, wg_ref, bg_ref, lam_ref,
                h0_ref, hf_ref, hb_ref, hfin_ref, a_scr, u_scr, hc_scr, *, tn, nt):
    i = pl.program_id(1)

    @pl.when(i == 0)
    def _():
        hc_scr[...] = h0_ref[0]

    cw = cw_ref[...]
    s8 = SUBLANES
    ng8 = tn // s8
    sub = lax.broadcasted_iota(jnp.int32, (ng8, s8, GW), 1)
    row = lax.broadcasted_iota(jnp.int32, (tn, GW), 0)
    row8 = row % s8

    def shifted(xm, k):
        rot = pltpu.roll(xm.reshape(ng8, s8, GW), (-k) % s8, 1).reshape(tn, GW)
        if k < 0:
            return jnp.where(row8 < -k, pltpu.roll(rot, s8, 0), rot)
        return jnp.where(row8 >= s8 - k, pltpu.roll(rot, tn - s8, 0), rot)

    def coeffs(x_ref, xp_ref, xn_ref, tile, d):
        xm = x_ref[...].astype(F32)
        prev = jnp.where(tile > 0, xp_ref[LRU_HALO - 1:LRU_HALO, :].astype(F32), 0.0)
        nxt = jnp.where(tile < nt - 1, xn_ref[0:2, :].astype(F32), 0.0)
        xm1 = jnp.where(row == 0, prev, shifted(xm, -1))
        xp1 = jnp.where(row == tn - 1, nxt[0:1], shifted(xm, 1))
        xp2 = jnp.where(row == tn - 2, nxt[0:1], shifted(xm, 2))
        xp2 = jnp.where(row == tn - 1, nxt[1:2], xp2)
        xb = cw[0:1] * xm1 + cw[1:2] * xm + cw[2:3] * xp1 + cw[3:4] * xp2 + cb_ref[...]
        tg = jnp.tanh(_dot(xb.astype(BF16), wg_ref[d]) + bg_ref[d]) + 1.0
        nl = -lam_ref[d]
        softplus = jnp.maximum(nl, 0.0) + jnp.log1p(jnp.exp(-jnp.abs(nl)))
        log_a = (-0.5 * LRU_C * softplus) * tg[:, :GW]
        a = jnp.exp(log_a)
        th = jnp.tanh(log_a)
        v = -0.5 * th
        u = v * lax.rsqrt(jnp.maximum(v, 1e-37)) * lax.rsqrt(1.0 - th) * (tg[:, GW:] * xb)

        a = a.reshape(ng8, s8, GW)
        u = u.reshape(ng8, s8, GW)
        for sft in (1, 2, 4):
            keep = sub >= sft if d == 0 else sub < s8 - sft
            rot = sft if d == 0 else s8 - sft
            a_s = jnp.where(keep, pltpu.roll(a, rot, 1), 1.0)
            u_s = jnp.where(keep, pltpu.roll(u, rot, 1), 0.0)
            u = a * u_s + u
            a = a * a_s
        a_scr[d] = a.reshape(tn, GW)
        u_scr[d] = u.reshape(tn, GW)

    coeffs(xf_ref, xfp_ref, xfn_ref, i, 0)
    coeffs(xb_ref, xbp_ref, xbn_ref, nt - 1 - i, 1)

    gs = LRU_HALO
    ng = tn // gs

    def body(g, carry):
        hf, hb = carry
        fb = pl.multiple_of(g * gs, gs)
        bb = pl.multiple_of((ng - 1 - g) * gs, gs)
        f1 = a_scr[0, pl.ds(fb, s8), :] * hf + u_scr[0, pl.ds(fb, s8), :]
        f2 = a_scr[0, pl.ds(fb + s8, s8), :] * f1[s8 - 1:s8] + u_scr[0, pl.ds(fb + s8, s8), :]
        b2 = a_scr[1, pl.ds(bb + s8, s8), :] * hb + u_scr[1, pl.ds(bb + s8, s8), :]
        b1 = a_scr[1, pl.ds(bb, s8), :] * b2[0:1] + u_scr[1, pl.ds(bb, s8), :]
        hf_ref[pl.ds(fb, gs), :] = jnp.concatenate([f1, f2], axis=0).astype(BF16)
        hb_ref[pl.ds(bb, gs), :] = jnp.concatenate([b1, b2], axis=0).astype(BF16)
        return f2[s8 - 1:s8], b1[0:1]

    hf, hb = lax.fori_loop(0, ng, body, (hc_scr[0:1, :], hc_scr[1:2, :]))
    hc_scr[0:1, :] = hf
    hc_scr[1:2, :] = hb

    @pl.when(i == nt - 1)
    def _():
        hfin_ref[0] = jnp.concatenate([hf, hb, jnp.zeros((s8 - 2, GW), F32)], axis=0)


def _lru(proj, b, n, tn, cw, cb, wg, bg, lam, h0):
    nt = n // tn
    hb8 = tn // LRU_HALO

    def main(rev):
        return pl.BlockSpec((tn, GW), lambda bi, i: (bi * nt + (nt - 1 - i if rev else i), C_LRU_X))

    def prev(rev):
        def im(bi, i):
            t = nt - 1 - i if rev else i
            return (jnp.maximum((bi * nt + t) * hb8 - 1, 0), C_LRU_X)
        return pl.BlockSpec((LRU_HALO, GW), im)

    def nxt(rev):
        def im(bi, i):
            t = nt - 1 - i if rev else i
            return (jnp.minimum((bi * nt + t + 1) * hb8, b * nt * hb8 - 1), C_LRU_X)
        return pl.BlockSpec((LRU_HALO, GW), im)

    const2 = lambda bi, i: (0, 0)
    const3 = lambda bi, i: (0, 0, 0)
    return pl.pallas_call(
        functools.partial(_lru_kernel, tn=tn, nt=nt),
        grid=(b, nt),
        in_specs=[main(False), prev(False), nxt(False), main(True), prev(True), nxt(True),
                  pl.BlockSpec((4, GW), const2), pl.BlockSpec((1, GW), const2),
                  pl.BlockSpec((2, GW, 2 * GW), const3), pl.BlockSpec((2, 1, 2 * GW), const3),
                  pl.BlockSpec((2, 1, GW), const3),
                  pl.BlockSpec((1, SUBLANES, GW), lambda bi, i: (bi, 0, 0))],
        out_specs=[pl.BlockSpec((tn, GW), lambda bi, i: (bi * nt + i, 0)),
                   pl.BlockSpec((tn, GW), lambda bi, i: (bi * nt + nt - 1 - i, 0)),
                   pl.BlockSpec((1, SUBLANES, GW), lambda bi, i: (bi, 0, 0))],
        out_shape=[jax.ShapeDtypeStruct((b * n, GW), BF16), jax.ShapeDtypeStruct((b * n, GW), BF16),
                   jax.ShapeDtypeStruct((b, SUBLANES, GW), F32)],
        scratch_shapes=[pltpu.VMEM((2, tn, GW), F32), pltpu.VMEM((2, tn, GW), F32),
                        pltpu.VMEM((SUBLANES, GW), F32)],
        compiler_params=_cparams("arbitrary", "arbitrary"),
        name="rglru",
    )(proj, proj, proj, proj, proj, proj, cw, cb, wg, bg, lam, h0)


def _ret_kernel(qf_ref, kf_ref, vf_ref, qb_ref, kb_ref, vb_ref, lgl_ref, lgh_ref, s0_ref,
                of_ref, ob_ref, sfin_ref, s_scr, dm_scr, qd_scr, kd_scr, *, nc, cps):
    c = pl.program_id(1)
    cs = RET_CHUNK

    @pl.when(c == 0)
    def _():
        s_scr[...] = s0_ref[0]
        ii = lax.broadcasted_iota(jnp.int32, (cs, cs), 0)
        jj = lax.broadcasted_iota(jnp.int32, (cs, cs), 1)
        t = lax.broadcasted_iota(jnp.int32, (cs, GW), 0).astype(F32)
        for d in range(2):
            rel = ii - jj if d == 0 else jj - ii - 1
            ok = rel >= 0
            relf = jnp.where(ok, rel, 0).astype(F32)
            for h in range(N_HEADS):
                lg = lgh_ref[d * N_HEADS + h:d * N_HEADS + h + 1, :]
                dm_scr[d, h] = jnp.where(ok, jnp.exp(lg * relf), 0.0)
            lgl = lgl_ref[d]
            if d == 0:
                qd_scr[d] = jnp.exp(lgl * (t + 1.0))
                kd_scr[d] = jnp.exp(lgl * (cs - 1.0 - t))
            else:
                qd_scr[d] = jnp.exp(lgl * (cs - 1.0 - t))
                kd_scr[d] = jnp.exp(lgl * t)

    lane = lax.broadcasted_iota(jnp.int32, (cs, GW), 1) // HEAD_DIM
    br = lax.broadcasted_iota(jnp.int32, (GW, GW), 0) // HEAD_DIM
    bc = lax.broadcasted_iota(jnp.int32, (GW, GW), 1) // HEAD_DIM

    def direction(d, q_ref, k_ref, v_ref, o_ref):
        s = s_scr[d]
        cdec = jnp.exp(lgl_ref[d] * float(cs))
        for step in range(cps):
            j = step if d == 0 else cps - 1 - step
            rs = slice(j * cs, (j + 1) * cs)
            k = k_ref[rs, :]
            v = v_ref[rs, :]
            q32 = q_ref[rs, :].astype(F32)
            v32 = v.astype(F32)
            parts = []
            vparts = []
            for h in range(N_HEADS):
                mk = lane == h
                qh = jnp.where(mk, q32, 0.0).astype(BF16)
                parts.append((_dot_nt(qh, k) * dm_scr[d, h]).astype(BF16))
                vparts.append(jnp.where(mk, v32, 0.0).astype(BF16))
            p = jnp.concatenate(parts, axis=1)
            vs = jnp.concatenate(vparts, axis=0)
            o = _dot(p, vs) + _dot((q32 * qd_scr[d]).astype(BF16), s.astype(BF16))
            o_ref[rs, :] = o.astype(BF16)
            kd = (k.astype(F32) * kd_scr[d]).T.astype(BF16)
            s = s * cdec + jnp.where(br == bc, _dot(kd, v), 0.0)
        s_scr[d] = s

    direction(0, qf_ref, kf_ref, vf_ref, of_ref)
    direction(1, qb_ref, kb_ref, vb_ref, ob_ref)

    @pl.when(c == nc - 1)
    def _():
        sfin_ref[0] = s_scr[...]


RET_CHUNKS_PER_STEP = 8


def _ret(proj, b, n, lgl, lgh, s0):
    cps = min(RET_CHUNKS_PER_STEP, n // RET_CHUNK)
    tb = cps * RET_CHUNK
    nc = n // tb

    def blk(col, rev):
        return pl.BlockSpec((tb, GW), lambda bi, c: (bi * nc + (nc - 1 - c if rev else c), col))

    return pl.pallas_call(
        functools.partial(_ret_kernel, nc=nc, cps=cps),
        grid=(b, nc),
        in_specs=[blk(C_RET_Q, False), blk(C_RET_K, False), blk(C_RET_V, False),
                  blk(C_RET_Q, True), blk(C_RET_K, True), blk(C_RET_V, True),
                  pl.BlockSpec((2, 1, GW), lambda bi, c: (0, 0, 0)),
                  pl.BlockSpec((2 * N_HEADS, LANES), lambda bi, c: (0, 0)),
                  pl.BlockSpec((1, 2, GW, GW), lambda bi, c: (bi, 0, 0, 0))],
        out_specs=[pl.BlockSpec((tb, GW), lambda bi, c: (bi * nc + c, 0)),
                   pl.BlockSpec((tb, GW), lambda bi, c: (bi * nc + nc - 1 - c, 0)),
                   pl.BlockSpec((1, 2, GW, GW), lambda bi, c: (bi, 0, 0, 0))],
        out_shape=[jax.ShapeDtypeStruct((b * n, GW), BF16), jax.ShapeDtypeStruct((b * n, GW), BF16),
                   jax.ShapeDtypeStruct((b, 2, GW, GW), F32)],
        scratch_shapes=[pltpu.VMEM((2, GW, GW), F32), pltpu.VMEM((2, N_HEADS, RET_CHUNK, RET_CHUNK), F32),
                        pltpu.VMEM((2, RET_CHUNK, GW), F32), pltpu.VMEM((2, RET_CHUNK, GW), F32)],
        compiler_params=_cparams("arbitrary", "arbitrary"),
        name="retention",
    )(proj, proj, proj, proj, proj, proj, lgl, lgh, s0)


def _na_kernel(q_ref, k_ref, v_ref, kc_ref, vc_ref, bias_ref, o_ref, *, rows, rb):
    i0 = pl.program_id(1) * rb
    nwin = NA_WIN_ROWS * GRID_W
    kc = kc_ref[...]
    vc = vc_ref[...]
    lane = lax.broadcasted_iota(jnp.int32, (GRID_W, GW), 1) // HEAD_DIM

    def row(i, carry):
        r = i0 + i
        r0 = jnp.clip(r - NA_WIN_ROWS // 2, 0, rows - NA_WIN_ROWS)
        start = pl.multiple_of(r0 * GRID_W, GRID_W)
        qrow = pl.multiple_of(i * GRID_W, GRID_W)
        kw = k_ref[pl.ds(start, nwin), :]
        vw = v_ref[pl.ds(start, nwin), :]
        q32 = q_ref[pl.ds(qrow, GRID_W), :].astype(F32)
        qs = jnp.concatenate([jnp.where(lane == h, q32, 0.0) for h in range(N_HEADS)], axis=0).astype(BF16)
        sw = _dot_nt(qs, kw) + bias_ref[r0 - r + NA_WIN_ROWS - 1]
        sc = _dot_nt(qs, kc)
        m = jnp.maximum(jnp.max(sw, axis=-1, keepdims=True), jnp.max(sc, axis=-1, keepdims=True))
        ew = jnp.exp(sw - m)
        ec = jnp.exp(sc - m)
        l = jnp.sum(ew, axis=-1, keepdims=True) + jnp.sum(ec, axis=-1, keepdims=True)
        o = (_dot(ew.astype(BF16), vw) + _dot(ec.astype(BF16), vc)) * (1.0 / l)
        out = jnp.zeros((GRID_W, GW), F32)
        for h in range(N_HEADS):
            out = jnp.where(lane == h, o[h * GRID_W:(h + 1) * GRID_W], out)
        o_ref[pl.ds(qrow, GRID_W), :] = out.astype(BF16)
        return carry

    lax.fori_loop(0, rb, row, 0, unroll=NA_UNROLL)


NA_ROWS_PER_STEP = 16
NA_UNROLL = 16


def _na(proj_l, proj_c, bias, b, n, c):
    rows = n // GRID_W
    rb = NA_ROWS_PER_STEP
    nrb = rows // rb
    tq = rb * GRID_W
    return pl.pallas_call(
        functools.partial(_na_kernel, rows=rows, rb=rb),
        grid=(b, nrb),
        in_specs=[pl.BlockSpec((tq, GW), lambda bi, i: (bi * nrb + i, C_NA_Q)),
                  pl.BlockSpec((n, GW), lambda bi, i: (bi, C_NA_K)),
                  pl.BlockSpec((n, GW), lambda bi, i: (bi, C_NA_V)),
                  pl.BlockSpec((c, GW), lambda bi, i: (bi, C_NA_K)),
                  pl.BlockSpec((c, GW), lambda bi, i: (bi, C_NA_V)),
                  pl.BlockSpec((NA_WIN_ROWS, N_HEADS * GRID_W, NA_WIN_ROWS * GRID_W), lambda bi, i: (0, 0, 0))],
        out_specs=pl.BlockSpec((tq, GW), lambda bi, i: (bi * nrb + i, 0)),
        out_shape=jax.ShapeDtypeStruct((b * n, GW), BF16),
        compiler_params=_cparams("arbitrary", "arbitrary"),
        name="na_attention",
    )(proj_l, proj_l, proj_l, proj_c, proj_c, bias)


def _na_bias_tables(rpb):
    nr, ncol = 2 * NA_WIN_ROWS - 1, 2 * NA_WIN_COLS - 1
    cq = np.arange(GRID_W)[:, None]
    ck = np.arange(GRID_W)[None, :]
    cstart = np.clip(cq - NA_WIN_COLS // 2, 0, GRID_W - NA_WIN_COLS)
    okc = (ck >= cstart) & (ck < cstart + NA_WIN_COLS)
    dc = np.clip(ck - cq + NA_WIN_COLS - 1, 0, ncol - 1)
    ohc = (dc[..., None] == np.arange(ncol)) & okc[..., None]
    dr = np.arange(NA_WIN_ROWS)[:, None] + np.arange(NA_WIN_ROWS)[None, :]
    ohr = dr[..., None] == np.arange(nr)
    bias = jnp.einsum('vjr,hrd,ckd->vhcjk', jnp.asarray(ohr, F32), rpb.astype(F32), jnp.asarray(ohc, F32),
                      precision=lax.Precision.HIGHEST)
    bias = jnp.where(jnp.asarray(okc)[None, None, :, None, :], bias, NEG)
    return bias.reshape(NA_WIN_ROWS, N_HEADS * GRID_W, NA_WIN_ROWS * GRID_W)


def _ctx_attn_kernel(*refs, diff, out_scale):
    if diff:
        q_ref, k_ref, v_ref, lam_ref, g_ref, o_ref = refs
    else:
        q_ref, k_ref, v_ref, o_ref = refs
    k = k_ref[...]
    v = v_ref[...]
    q32 = q_ref[...].astype(F32)
    lane = lax.broadcasted_iota(jnp.int32, q32.shape, 1)

    def softmax_pv(mk):
        s = _dot_nt(jnp.where(mk, q32, 0.0).astype(BF16), k)
        z = s - jnp.max(s, axis=-1, keepdims=True)
        e = jnp.exp2(z) if diff else jnp.exp(z)
        return _dot(e.astype(BF16), v) * (1.0 / jnp.sum(e, axis=-1, keepdims=True))

    out = jnp.zeros(q32.shape, F32)
    for h in range(N_HEADS):
        if diff:
            o = (softmax_pv(lane // (HEAD_DIM // 2) == 2 * h)
                 - lam_ref[...] * softmax_pv(lane // (HEAD_DIM // 2) == 2 * h + 1))
        else:
            o = softmax_pv(lane // HEAD_DIM == h)
        out = jnp.where(lane // HEAD_DIM == h, o, out)
    if diff:
        out = out * lax.rsqrt(_head_mean_sq(out) + EPS) * g_ref[...] * out_scale
    o_ref[...] = out.astype(BF16)


def _ctx_attn(proj_c, b, c, qcol, kcol, vcol, lam=None, g=None, out_scale=1.0):
    diff = lam is not None
    in_specs = [pl.BlockSpec((c, GW), lambda bi: (bi, qcol)),
                pl.BlockSpec((c, GW), lambda bi: (bi, kcol)),
                pl.BlockSpec((c, GW), lambda bi: (bi, vcol))]
    args = [proj_c, proj_c, proj_c]
    if diff:
        in_specs += [pl.BlockSpec((1, GW), lambda bi: (0, 0)), pl.BlockSpec((1, GW), lambda bi: (0, 0))]
        args += [lam, g]
    return pl.pallas_call(
        functools.partial(_ctx_attn_kernel, diff=diff, out_scale=out_scale),
        grid=(b,),
        in_specs=in_specs,
        out_specs=pl.BlockSpec((c, GW), lambda bi: (bi, 0)),
        out_shape=jax.ShapeDtypeStruct((b * c, GW), BF16),
        compiler_params=_cparams("arbitrary"),
        name="ctx_diff_attention" if diff else "ctx_attention",
    )(*args)


DIFF_KEY_TILES = (2816, 768, 256)
DIFF_HEADROOM = 100.0
DIFF_MIN_SUM = 2.0 ** -100


def _diff_lat_kernel(q_ref, kc_ref, vtc_ref, kl_ref, vtl_ref, kn_ref, lam_ref, g_ref, o_ref,
                     k_scr, vt_scr, qm_scr, sh_scr, m_scr, acc_scr, *, tq, tk, n_ctx, n_lat, out_scale):
    nt = (n_ctx + n_lat) // tk

    @pl.when(pl.program_id(1) == 0)
    def _():
        k_scr[0:n_ctx, :] = kc_ref[...]
        k_scr[n_ctx:, :] = kl_ref[...]
        vt_scr[:, 0:n_ctx] = vtc_ref[0]
        vt_scr[:, n_ctx:] = vtl_ref[0]

    q32 = q_ref[...].astype(F32)
    qt = q32.T
    rowg = lax.broadcasted_iota(jnp.int32, (GW, tq), 0) // MAP_DIM
    for u in range(N_MAPS):
        qm_scr[u] = jnp.where(rowg == u, qt, 0.0).astype(BF16)
    kmax = jnp.sqrt(_group_sum_sq(jnp.sqrt(kn_ref[0]), MAP_DIM))[0:1, :]
    bound = jnp.sqrt(_group_sum_sq(q32 * kmax, MAP_DIM)) * (1.0 + 2.0 ** -8)
    sh_scr[...] = bound.T - DIFF_HEADROOM
    acc_scr[...] = jnp.zeros(acc_scr.shape, F32)

    def tile(t, u):
        st = pl.multiple_of(t * tk, tk)
        h = u // 2
        return k_scr[pl.ds(st, tk), :], vt_scr[h * V_AUG:(h + 1) * V_AUG, pl.ds(st, tk)]

    def fast(t, carry):
        for u in range(N_MAPS):
            kt, vt = tile(t, u)
            s = _dot(kt, qm_scr[u])
            e = jnp.exp2(s - sh_scr[u * MAP_DIM:u * MAP_DIM + 1, :]).astype(BF16)
            acc_scr[u] = acc_scr[u] + _dot(vt, e)
        return carry

    lax.fori_loop(0, nt, fast, 0)

    lmin = acc_scr[0, HEAD_DIM:HEAD_DIM + 1, :]
    for u in range(1, N_MAPS):
        lmin = jnp.minimum(lmin, acc_scr[u, HEAD_DIM:HEAD_DIM + 1, :])

    @pl.when(jnp.logical_not(jnp.min(lmin) >= DIFF_MIN_SUM))
    def _():
        m_scr[...] = jnp.full(m_scr.shape, NEG, F32)
        acc_scr[...] = jnp.zeros(acc_scr.shape, F32)

        def exact(t, carry):
            for u in range(N_MAPS):
                kt, vt = tile(t, u)
                s = _dot(kt, qm_scr[u])
                mo = m_scr[u:u + 1, :]
                mn = jnp.maximum(mo, jnp.max(s, axis=0, keepdims=True))
                m_scr[u:u + 1, :] = mn
                e = jnp.exp2(s - mn).astype(BF16)
                acc_scr[u] = jnp.exp2(mo - mn) * acc_scr[u] + _dot(vt, e)
            return carry

        lax.fori_loop(0, nt, exact, 0)

    lam = lam_ref[0:1, 0:1]
    outs = []
    for h in range(N_HEADS):
        a1 = acc_scr[2 * h]
        a2 = acc_scr[2 * h + 1]
        o1 = a1[:HEAD_DIM] * (1.0 / a1[HEAD_DIM:HEAD_DIM + 1])
        o2 = a2[:HEAD_DIM] * (1.0 / a2[HEAD_DIM:HEAD_DIM + 1])
        outs.append(o1 - lam * o2)
    out = jnp.concatenate(outs, axis=0).T
    out = out * lax.rsqrt(_head_mean_sq(out) + EPS) * g_ref[...] * out_scale
    o_ref[...] = out.astype(BF16)


def _diff_lat(proj_l, proj_c, vt_l, vt_c, kn2, lam, g, b, n, c, tq, out_scale):
    nt = n // tq
    nk = c + n
    tk = next(t for t in DIFF_KEY_TILES if nk % t == 0)
    once = pl.Buffered(1)
    return pl.pallas_call(
        functools.partial(_diff_lat_kernel, tq=tq, tk=tk, n_ctx=c, n_lat=n, out_scale=out_scale),
        grid=(b, nt),
        in_specs=[pl.BlockSpec((tq, GW), lambda bi, i: (bi * nt + i, C_DF_Q)),
                  pl.BlockSpec((c, GW), lambda bi, i: (bi, C_DF_K)),
                  pl.BlockSpec((1, N_HEADS * V_AUG, c), lambda bi, i: (bi, 0, 0)),
                  pl.BlockSpec((n, GW), lambda bi, i: (bi, C_DF_K), pipeline_mode=once),
                  pl.BlockSpec((1, N_HEADS * V_AUG, n), lambda bi, i: (bi, 0, 0), pipeline_mode=once),
                  pl.BlockSpec((1, SUBLANES, GW), lambda bi, i: (bi, 0, 0)),
                  pl.BlockSpec((1, GW), lambda bi, i: (0, 0)),
                  pl.BlockSpec((1, GW), lambda bi, i: (0, 0))],
        out_specs=pl.BlockSpec((tq, GW), lambda bi, i: (bi * nt + i, 0)),
        out_shape=jax.ShapeDtypeStruct((b * n, GW), BF16),
        scratch_shapes=[pltpu.VMEM((nk, GW), BF16), pltpu.VMEM((N_HEADS * V_AUG, nk), BF16),
                        pltpu.VMEM((N_MAPS, GW, tq), BF16), pltpu.VMEM((GW, tq), F32),
                        pltpu.VMEM((N_MAPS, tq), F32), pltpu.VMEM((N_MAPS, V_AUG, tq), F32)],
        compiler_params=_cparams("arbitrary", "arbitrary"),
        name="diff_attention_lat",
    )(proj_l, proj_c, vt_c, proj_l, vt_l, kn2, lam, g)


def _out_proj_kernel(x_ref, ya_ref, hf_ref, hb_ref, gl_ref, of_ref, ob_ref, gr_ref, yd_ref, w_ref, g_ref, gate_ref,
                     o_ref):
    gl = gl_ref[...].astype(F32)
    gelu = 0.5 * gl * (1.0 + jnp.tanh(math.sqrt(2.0 / math.pi) * (gl + 0.044715 * (gl * gl * gl))))
    yb = (hf_ref[...].astype(F32) + hb_ref[...].astype(F32)) * gelu
    r = of_ref[...].astype(F32) + ob_ref[...].astype(F32)
    gr = gr_ref[...].astype(F32)
    yc = r * lax.rsqrt(_head_mean_sq(r) + EPS) * (gr * _sigmoid(gr))
    y = (_dot(ya_ref[...], w_ref[0:GW, :]) + _dot(yb.astype(BF16), w_ref[GW:2 * GW, :])
         + _dot(yc.astype(BF16), w_ref[2 * GW:3 * GW, :]) + _dot(yd_ref[...], w_ref[3 * GW:4 * GW, :]))
    o_ref[...] = x_ref[...] + gate_ref[0] * _rms(y, g_ref[...])


def _out_proj(x2d, ya, hf, hb, of, ob, yd, proj, w, layer, g, gate, b, n, tm):
    nt = n // tm
    tok = lambda col: pl.BlockSpec((tm, GW), lambda bi, i: (bi * nt + i, col))
    return pl.pallas_call(
        _out_proj_kernel,
        grid=(b, nt),
        in_specs=[pl.BlockSpec((tm, D_MODEL), lambda bi, i: (bi * nt + i, 0)),
                  tok(0), tok(0), tok(0), tok(C_LRU_G), tok(0), tok(0), tok(C_RET_G), tok(0),
                  pl.BlockSpec((None, D_MODEL, D_MODEL), lambda bi, i: (layer, 0, 0)),
                  pl.BlockSpec((1, D_MODEL), lambda bi, i: (0, 0)),
                  pl.BlockSpec((1, 1, D_MODEL), lambda bi, i: (bi, 0, 0))],
        out_specs=pl.BlockSpec((tm, D_MODEL), lambda bi, i: (bi * nt + i, 0)),
        out_shape=jax.ShapeDtypeStruct((b * n, D_MODEL), F32),
        compiler_params=_cparams("arbitrary", "arbitrary"),
        name="out_proj",
    )(x2d, ya, hf, hb, proj, of, ob, proj, yd, w, g, gate)


FFN_CHUNK = 256


def _ffn_kernel(x_ref, g1_ref, sh_ref, sc_ref, w1_ref, w2_ref, g2_ref, gate_ref, o_ref):
    x = x_ref[...]
    h = (_rms(x, g1_ref[...]) * (1.0 + sc_ref[0]) + sh_ref[0]).astype(BF16)
    acc = jnp.zeros(x.shape, F32)
    for c in range(D_FF // FFN_CHUNK):
        lo, hi = c * FFN_CHUNK, (c + 1) * FFN_CHUNK
        gt = _dot(h, w1_ref[:, lo:hi])
        up = _dot(h, w1_ref[:, D_FF + lo:D_FF + hi])
        act = (gt * _sigmoid(gt) * up).astype(BF16)
        acc = acc + _dot(act, w2_ref[lo:hi, :])
    o_ref[...] = x + gate_ref[0] * _rms(acc, g2_ref[...])


def _ffn(x2d, g1, shift, scale, w1, w2, layer, g2, gate, b, n, tm):
    nt = n // tm
    vec = pl.BlockSpec((1, D_MODEL), lambda bi, i: (0, 0))
    per_b = pl.BlockSpec((1, 1, D_MODEL), lambda bi, i: (bi, 0, 0))
    return pl.pallas_call(
        _ffn_kernel,
        grid=(b, nt),
        in_specs=[pl.BlockSpec((tm, D_MODEL), lambda bi, i: (bi * nt + i, 0)), vec, per_b, per_b,
                  pl.BlockSpec((None, D_MODEL, 2 * D_FF), lambda bi, i: (layer, 0, 0)),
                  pl.BlockSpec((None, D_FF, D_MODEL), lambda bi, i: (layer, 0, 0)), vec, per_b],
        out_specs=pl.BlockSpec((tm, D_MODEL), lambda bi, i: (bi * nt + i, 0)),
        out_shape=jax.ShapeDtypeStruct((b * n, D_MODEL), F32),
        compiler_params=_cparams("arbitrary", "arbitrary"),
        name="ffn",
    )(x2d, g1, shift, scale, w1, w2, g2, gate)


def _blockdiag(w):
    nb, bs, _ = w.shape
    return jnp.einsum('kcd,kj->kcjd', w, jnp.eye(nb, dtype=w.dtype)).reshape(nb * bs, nb * bs)


def _lru_params(conv_w, conv_b, gate_w, gate_b, lam):
    wg = jnp.stack([jnp.concatenate([_blockdiag(gate_w[d, 0]), _blockdiag(gate_w[d, 1])], axis=1)
                    for d in range(2)])
    bg = jnp.stack([jnp.concatenate([gate_b[d, 0].reshape(1, GW), gate_b[d, 1].reshape(1, GW)], axis=1)
                    for d in range(2)])
    return conv_w, conv_b.reshape(1, GW), (0.5 * wg).astype(BF16), 0.5 * bg, lam.reshape(2, 1, GW)


def kernel(x, c, ctx, c_ctx, w_mod, b_mod, g_pre_mix, g_post_mix, g_pre_ffn, g_post_ffn, w_in, na_rpb, lru_conv_w,
           lru_conv_b, lru_gate_w, lru_gate_b, lru_lambda, ret_decay, diff_lambda, diff_subln, w_out, w_ffn_in,
           w_ffn_out):
    b, n, _ = x.shape
    nc = ctx.shape[1]
    depth = w_mod.shape[0]
    assert n % TILE_PROJ == 0 and n % TILE_LRU == 0 and nc % RET_CHUNK == 0

    r = -(-(b + 1) // SUBLANES) * SUBLANES
    cc = jnp.zeros((r, D_MODEL), F32).at[:b].set(c).at[b].set(c_ctx)
    mod = _mod(cc, w_mod, b_mod)
    tabs = _rope_tables(n)

    xl = x.reshape(b * n, D_MODEL)
    xc = ctx.reshape(b * nc, D_MODEL)
    tm, tm_proj = TILE_FFN, TILE_PROJ
    vec = lambda a: a.reshape(1, D_MODEL)
    w_in_b, w_out_b, w1_b, w2_b = (w.astype(BF16) for w in (w_in, w_out, w_ffn_in, w_ffn_out))
    for l in range(depth):
        last = l == depth - 1
        lam_init = 0.8 - 0.6 * math.exp(-0.3 * l)
        ml = [mod[l, :b, k * D_MODEL:(k + 1) * D_MODEL].reshape(b, 1, D_MODEL) for k in range(6)]
        mc = [jnp.broadcast_to(mod[l, b, k * D_MODEL:(k + 1) * D_MODEL], (b, 1, D_MODEL)) for k in range(6)]

        pc, vt_c, kn_c = _in_proj(xc, vec(g_pre_mix[l]), mc[0], mc[1], w_in_b, l, None, b, nc, nc)
        pt, vt_l, kn_l = _in_proj(xl, vec(g_pre_mix[l]), ml[0], ml[1], w_in_b, l, tabs, b, n, tm_proj)

        lru_p = _lru_params(lru_conv_w[l], lru_conv_b[l], lru_gate_w[l], lru_gate_b[l], lru_lambda[l])
        hf_c, hb_c, hfin = _lru(pc, b, nc, nc, *lru_p, jnp.zeros((b, SUBLANES, GW), F32))
        hf_l, hb_l, _ = _lru(pt, b, n, TILE_LRU, *lru_p, hfin)

        log_g = jax.nn.log_sigmoid(ret_decay[l].astype(F32))
        lgl = jnp.repeat(log_g, HEAD_DIM, axis=-1).reshape(2, 1, GW)
        lgh = jnp.broadcast_to(log_g.reshape(2 * N_HEADS, 1), (2 * N_HEADS, LANES))
        of_c, ob_c, sfin = _ret(pc, b, nc, lgl, lgh, jnp.zeros((b, 2, GW, GW), F32))
        of_l, ob_l, _ = _ret(pt, b, n, lgl, lgh, sfin)

        ya_l = _na(pt, pc, _na_bias_tables(na_rpb[l]), b, n, nc)

        lq1, lk1, lq2, lk2 = diff_lambda[l].astype(F32)
        lam = jnp.exp(jnp.sum(lq1 * lk1)) - jnp.exp(jnp.sum(lq2 * lk2)) + lam_init
        lam_v = jnp.broadcast_to(lam, (1, GW)).astype(F32)
        g_sub = jnp.tile(diff_subln[l].astype(F32), N_HEADS).reshape(1, GW)
        yd_l = _diff_lat(pt, pc, vt_l, vt_c, jnp.maximum(kn_c, kn_l), lam_v, g_sub, b, n, nc, tm_proj, 1.0 - lam_init)

        x_mid = _out_proj(xl, ya_l, hf_l, hb_l, of_l, ob_l, yd_l, pt, w_out_b, l, vec(g_post_mix[l]), ml[2], b, n,
                          tm_proj)
        xl_new = _ffn(x_mid, vec(g_pre_ffn[l]), ml[3], ml[4], w1_b, w2_b, l, vec(g_post_ffn[l]), ml[5], b, n, tm)

        if not last:
            ya_c = _ctx_attn(pc, b, nc, C_NA_Q, C_NA_K, C_NA_V)
            yd_c = _ctx_attn(pc, b, nc, C_DF_Q, C_DF_K, C_DF_V, lam_v, g_sub, 1.0 - lam_init)
            xc_mid = _out_proj(xc, ya_c, hf_c, hb_c, of_c, ob_c, yd_c, pc, w_out_b, l, vec(g_post_mix[l]), mc[2],
                               b, nc, nc)
            xc = _ffn(xc_mid, vec(g_pre_ffn[l]), mc[3], mc[4], w1_b, w2_b, l, vec(g_post_ffn[l]), mc[5], b, nc, nc)
        xl = xl_new
    return xl.reshape(b, n, D_MODEL)
```

```python
import functools
import math

import numpy as np
import jax
import jax.numpy as jnp
from jax import lax
from jax.experimental import pallas as pl
from jax.experimental.pallas import tpu as pltpu

F32 = jnp.float32
BF16 = jnp.bfloat16

D_MODEL = 1024
GRID_W = 64
HEAD_DIM = 64
N_HEADS = 4
GW = N_HEADS * HEAD_DIM
N_PROJ = 12
N_MAPS = 2 * N_HEADS
MAP_DIM = HEAD_DIM // 2
V_AUG = HEAD_DIM + 16
D_FF = 2816
NA_WIN_ROWS = 8
NA_WIN_COLS = 16
LRU_C = 8.0
RET_CHUNK = 128
ROPE_BASE = 10000.0
EPS = 1e-6
NEG = -1e30

C_NA_Q, C_NA_K, C_NA_V, C_LRU_X, C_LRU_G, C_RET_Q, C_RET_K, C_RET_V, C_RET_G, C_DF_Q, C_DF_K, C_DF_V = range(12)
_COL_SCALE = {C_NA_Q: HEAD_DIM ** -0.5, C_RET_K: HEAD_DIM ** -0.5,
              C_DF_Q: (HEAD_DIM // 2) ** -0.5 * math.log2(math.e)}

LANES = 128
SUBLANES = 8
BF16_ROWS = 2 * SUBLANES
V7X_VMEM_LIMIT = 56 * 1024 * 1024

TILE_PROJ = 1024
TILE_FFN = 512
TILE_LRU = 1024
MOD_COL_TILE = 1536
LRU_HALO = BF16_ROWS


def _cparams(*sem):
    return pltpu.CompilerParams(dimension_semantics=sem, vmem_limit_bytes=V7X_VMEM_LIMIT)


def _dot(a, b):
    return jnp.dot(a, b, preferred_element_type=F32)


def _dot_nt(a, b):
    return lax.dot_general(a, b, (((1,), (1,)), ((), ())), preferred_element_type=F32)


def _rms(x, g):
    return x * lax.rsqrt(jnp.mean(x * x, axis=-1, keepdims=True) + EPS) * g


def _sigmoid(x):
    return 1.0 / (1.0 + jnp.exp(-x))


def _group_sum_sq(y, group):
    y2 = y * y
    hi = y2.astype(BF16)
    lo = (y2 - hi.astype(F32)).astype(BF16)
    r = lax.broadcasted_iota(jnp.int32, (GW, GW), 0) // group
    c = lax.broadcasted_iota(jnp.int32, (GW, GW), 1) // group
    bd = jnp.where(r == c, 1.0, 0.0).astype(BF16)
    return _dot(hi, bd) + _dot(lo, bd)


def _head_mean_sq(y):
    return _group_sum_sq(y, HEAD_DIM) * (1.0 / HEAD_DIM)


def _mod_kernel(c_ref, w_ref, b_ref, o_ref):
    c = c_ref[...]
    s = c * _sigmoid(c)
    o_ref[0] = jnp.dot(s, w_ref[0], preferred_element_type=F32, precision=lax.Precision.HIGHEST) + b_ref[0]


def _mod(cc, w_mod, b_mod):
    depth = w_mod.shape[0]
    r = cc.shape[0]
    tn = MOD_COL_TILE
    return pl.pallas_call(
        _mod_kernel,
        grid=(depth, 6 * D_MODEL // tn),
        in_specs=[pl.BlockSpec((r, D_MODEL), lambda l, j: (0, 0)),
                  pl.BlockSpec((1, D_MODEL, tn), lambda l, j: (l, 0, j)),
                  pl.BlockSpec((1, 1, tn), lambda l, j: (l, 0, j))],
        out_specs=pl.BlockSpec((1, r, tn), lambda l, j: (l, 0, j)),
        out_shape=jax.ShapeDtypeStruct((depth, r, 6 * D_MODEL), F32),
        compiler_params=_cparams("arbitrary", "arbitrary"),
        name="adaln_mod",
    )(cc, w_mod, b_mod.reshape(depth, 1, 6 * D_MODEL))


def _rope(p, cos, sin_signed, half):
    outs = []
    for c in range(GW // LANES):
        xs = p[:, c * LANES:(c + 1) * LANES]
        lane = lax.broadcasted_iota(jnp.int32, xs.shape, 1)
        first = (lane % (2 * half)) < half
        partner = jnp.where(first, pltpu.roll(xs, LANES - half, 1), pltpu.roll(xs, half, 1))
        outs.append(xs * cos + partner * sin_signed)
    return jnp.concatenate(outs, axis=1)


def _in_proj_kernel(*refs, rope):
    if rope:
        x_ref, g_ref, sh_ref, sc_ref, w_ref, tab_ref, o_ref, vt_ref, kn_ref = refs
    else:
        x_ref, g_ref, sh_ref, sc_ref, w_ref, o_ref, vt_ref, kn_ref = refs

    @pl.when(pl.program_id(1) == 0)
    def _():
        kn_ref[...] = jnp.zeros(kn_ref.shape, F32)

    h = _rms(x_ref[...], g_ref[...]) * (1.0 + sc_ref[0]) + sh_ref[0]
    hb = h.astype(BF16)
    for j in range(N_PROJ):
        p = _dot(hb, w_ref[:, j * GW:(j + 1) * GW])
        if j in _COL_SCALE:
            p = p * _COL_SCALE[j]
        if rope and j in (C_RET_Q, C_RET_K):
            p = _rope(p, tab_ref[0], tab_ref[1], HEAD_DIM // 2)
        if rope and j in (C_DF_Q, C_DF_K):
            p = _rope(p, tab_ref[2], tab_ref[3], HEAD_DIM // 4)
        pb = p.astype(BF16)
        o_ref[:, j * GW:(j + 1) * GW] = pb
        if j == C_DF_K:
            kf = pb.astype(F32)
            kn_ref[0] = jnp.maximum(kn_ref[0],
                                    jnp.broadcast_to(jnp.max(kf * kf, axis=0, keepdims=True), (SUBLANES, GW)))
        if j == C_DF_V:
            vt = p.T.astype(BF16)
            ones = jnp.ones((V_AUG - HEAD_DIM, vt.shape[1]), BF16)
            for hd in range(N_HEADS):
                vt_ref[0, hd * V_AUG:hd * V_AUG + HEAD_DIM, :] = vt[hd * HEAD_DIM:(hd + 1) * HEAD_DIM]
                vt_ref[0, hd * V_AUG + HEAD_DIM:(hd + 1) * V_AUG, :] = ones


def _in_proj(x2d, g, shift, scale, w, layer, tabs, b, n, tm):
    nt = n // tm
    rope = tabs is not None
    in_specs = [pl.BlockSpec((tm, D_MODEL), lambda bi, i: (bi * nt + i, 0)),
                pl.BlockSpec((1, D_MODEL), lambda bi, i: (0, 0)),
                pl.BlockSpec((1, 1, D_MODEL), lambda bi, i: (bi, 0, 0)),
                pl.BlockSpec((1, 1, D_MODEL), lambda bi, i: (bi, 0, 0)),
                pl.BlockSpec((None, D_MODEL, N_PROJ * GW), lambda bi, i: (layer, 0, 0))]
    args = [x2d, g, shift, scale, w]
    if rope:
        in_specs.append(pl.BlockSpec((4, tm, LANES), lambda bi, i: (0, i, 0)))
        args.append(tabs)
    return pl.pallas_call(
        functools.partial(_in_proj_kernel, rope=rope),
        grid=(b, nt),
        in_specs=in_specs,
        out_specs=[pl.BlockSpec((tm, N_PROJ * GW), lambda bi, i: (bi * nt + i, 0)),
                   pl.BlockSpec((1, N_HEADS * V_AUG, tm), lambda bi, i: (bi, 0, i)),
                   pl.BlockSpec((1, SUBLANES, GW), lambda bi, i: (bi, 0, 0))],
        out_shape=[jax.ShapeDtypeStruct((b * n, N_PROJ * GW), BF16),
                   jax.ShapeDtypeStruct((b, N_HEADS * V_AUG, n), BF16),
                   jax.ShapeDtypeStruct((b, SUBLANES, GW), F32)],
        compiler_params=_cparams("arbitrary", "arbitrary"),
        name="in_proj_rope" if rope else "in_proj",
    )(*args)


def _rope_tables(n):
    t = jnp.arange(n)
    row = (t // GRID_W).astype(F32)
    col = (t % GRID_W).astype(F32)

    def tab(dim):
        nf = dim // 4
        inv = ROPE_BASE ** (-jnp.arange(nf, dtype=F32) / nf)
        ang = jnp.concatenate([row[:, None] * inv, col[:, None] * inv], axis=-1)
        cos, sin = jnp.cos(ang), jnp.sin(ang)
        reps = LANES // dim
        return (jnp.tile(jnp.concatenate([cos, cos], axis=-1), (1, reps)),
                jnp.tile(jnp.concatenate([-sin, sin], axis=-1), (1, reps)))

    cr, sr = tab(HEAD_DIM)
    cd, sd = tab(HEAD_DIM // 2)
    return jnp.stack([cr, sr, cd, sd])


def _lru_kernel(xf_ref, xfp_ref, xfn_ref, xb_ref, xbp_ref, xbn_ref, cw_ref, cb_ref, wg_ref, bg_ref, lam_ref,
                h0_ref, hf_ref, hb_ref, hfin_ref, a_scr, u_scr, hc_scr, *, tn, nt):
    i = pl.program_id(1)

    @pl.when(i == 0)
    def _():
        hc_scr[...] = h0_ref[0]

    cw = cw_ref[...]
    s8 = SUBLANES
    ng8 = tn // s8
    sub = lax.broadcasted_iota(jnp.int32, (ng8, s8, GW), 1)
    row = lax.broadcasted_iota(jnp.int32, (tn, GW), 0)
    row8 = row % s8

    def shifted(xm, k):
        rot = pltpu.roll(xm.reshape(ng8, s8, GW), (-k) % s8, 1).reshape(tn, GW)
        if k < 0:
            return jnp.where(row8 < -k, pltpu.roll(rot, s8, 0), rot)
        return jnp.where(row8 >= s8 - k, pltpu.roll(rot, tn - s8, 0), rot)

    def coeffs(x_ref, xp_ref, xn_ref, tile, d):
        xm = x_ref[...].astype(F32)
        prev = jnp.where(tile > 0, xp_ref[LRU_HALO - 1:LRU_HALO, :].astype(F32), 0.0)
        nxt = jnp.where(tile < nt - 1, xn_ref[0:2, :].astype(F32), 0.0)
        xm1 = jnp.where(row == 0, prev, shifted(xm, -1))
        xp1 = jnp.where(row == tn - 1, nxt[0:1], shifted(xm, 1))
        xp2 = jnp.where(row == tn - 2, nxt[0:1], shifted(xm, 2))
        xp2 = jnp.where(row == tn - 1, nxt[1:2], xp2)
        xb = cw[0:1] * xm1 + cw[1:2] * xm + cw[2:3] * xp1 + cw[3:4] * xp2 + cb_ref[...]
        tg = jnp.tanh(_dot(xb.astype(BF16), wg_ref[d]) + bg_ref[d]) + 1.0
        nl = -lam_ref[d]
        softplus = jnp.maximum(nl, 0.0) + jnp.log1p(jnp.exp(-jnp.abs(nl)))
        log_a = (-0.5 * LRU_C * softplus) * tg[:, :GW]
        a = jnp.exp(log_a)
        th = jnp.tanh(log_a)
        v = -0.5 * th
        u = v * lax.rsqrt(jnp.maximum(v, 1e-37)) * lax.rsqrt(1.0 - th) * (tg[:, GW:] * xb)

        a = a.reshape(ng8, s8, GW)
        u = u.reshape(ng8, s8, GW)
        for sft in (1, 2, 4):
            keep = sub >= sft if d == 0 else sub < s8 - sft
            rot = sft if d == 0 else s8 - sft
            a_s = jnp.where(keep, pltpu.roll(a, rot, 1), 1.0)
            u_s = jnp.where(keep, pltpu.roll(u, rot, 1), 0.0)
            u = a * u_s + u
            a = a * a_s
        a_scr[d] = a.reshape(tn, GW)
        u_scr[d] = u.reshape(tn, GW)

    coeffs(xf_ref, xfp_ref, xfn_ref, i, 0)
    coeffs(xb_ref, xbp_ref, xbn_ref, nt - 1 - i, 1)

    gs = LRU_HALO
    ng = tn // gs

    def body(g, carry):
        hf, hb = carry
        fb = pl.multiple_of(g * gs, gs)
        bb = pl.multiple_of((ng - 1 - g) * gs, gs)
        f1 = a_scr[0, pl.ds(fb, s8), :] * hf + u_scr[0, pl.ds(fb, s8), :]
        f2 = a_scr[0, pl.ds(fb + s8, s8), :] * f1[s8 - 1:s8] + u_scr[0, pl.ds(fb + s8, s8), :]
        b2 = a_scr[1, pl.ds(bb + s8, s8), :] * hb + u_scr[1, pl.ds(bb + s8, s8), :]
        b1 = a_scr[1, pl.ds(bb, s8), :] * b2[0:1] + u_scr[1, pl.ds(bb, s8), :]
        hf_ref[pl.ds(fb, gs), :] = jnp.concatenate([f1, f2], axis=0).astype(BF16)
        hb_ref[pl.ds(bb, gs), :] = jnp.concatenate([b1, b2], axis=0).astype(BF16)
        return f2[s8 - 1:s8], b1[0:1]

    hf, hb = lax.fori_loop(0, ng, body, (hc_scr[0:1, :], hc_scr[1:2, :]))
    hc_scr[0:1, :] = hf
    hc_scr[1:2, :] = hb

    @pl.when(i == nt - 1)
    def _():
        hfin_ref[0] = jnp.concatenate([hf, hb, jnp.zeros((s8 - 2, GW), F32)], axis=0)


def _lru(proj, b, n, tn, cw, cb, wg, bg, lam, h0):
    nt = n // tn
    hb8 = tn // LRU_HALO

    def main(rev):
        return pl.BlockSpec((tn, GW), lambda bi, i: (bi * nt + (nt - 1 - i if rev else i), C_LRU_X))

    def prev(rev):
        def im(bi, i):
            t = nt - 1 - i if rev else i
            return (jnp.maximum((bi * nt + t) * hb8 - 1, 0), C_LRU_X)
        return pl.BlockSpec((LRU_HALO, GW), im)

    def nxt(rev):
        def im(bi, i):
            t = nt - 1 - i if rev else i
            return (jnp.minimum((bi * nt + t + 1) * hb8, b * nt * hb8 - 1), C_LRU_X)
        return pl.BlockSpec((LRU_HALO, GW), im)

    const2 = lambda bi, i: (0, 0)
    const3 = lambda bi, i: (0, 0, 0)
    return pl.pallas_call(
        functools.partial(_lru_kernel, tn=tn, nt=nt),
        grid=(b, nt),
        in_specs=[main(False), prev(False), nxt(False), main(True), prev(True), nxt(True),
                  pl.BlockSpec((4, GW), const2), pl.BlockSpec((1, GW), const2),
                  pl.BlockSpec((2, GW, 2 * GW), const3), pl.BlockSpec((2, 1, 2 * GW), const3),
                  pl.BlockSpec((2, 1, GW), const3),
                  pl.BlockSpec((1, SUBLANES, GW), lambda bi, i: (bi, 0, 0))],
        out_specs=[pl.BlockSpec((tn, GW), lambda bi, i: (bi * nt + i, 0)),
                   pl.BlockSpec((tn, GW), lambda bi, i: (bi * nt + nt - 1 - i, 0)),
                   pl.BlockSpec((1, SUBLANES, GW), lambda bi, i: (bi, 0, 0))],
        out_shape=[jax.ShapeDtypeStruct((b * n, GW), BF16), jax.ShapeDtypeStruct((b * n, GW), BF16),
                   jax.ShapeDtypeStruct((b, SUBLANES, GW), F32)],
        scratch_shapes=[pltpu.VMEM((2, tn, GW), F32), pltpu.VMEM((2, tn, GW), F32),
                        pltpu.VMEM((SUBLANES, GW), F32)],
        compiler_params=_cparams("arbitrary", "arbitrary"),
        name="rglru",
    )(proj, proj, proj, proj, proj, proj, cw, cb, wg, bg, lam, h0)


def _ret_kernel(qf_ref, kf_ref, vf_ref, qb_ref, kb_ref, vb_ref, lgl_ref, lgh_ref, s0_ref,
                of_ref, ob_ref, sfin_ref, s_scr, dm_scr, qd_scr, kd_scr, *, nc, cps):
    c = pl.program_id(1)
    cs = RET_CHUNK

    @pl.when(c == 0)
    def _():
        s_scr[...] = s0_ref[0]
        ii = lax.broadcasted_iota(jnp.int32, (cs, cs), 0)
        jj = lax.broadcasted_iota(jnp.int32, (cs, cs), 1)
        t = lax.broadcasted_iota(jnp.int32, (cs, GW), 0).astype(F32)
        for d in range(2):
            rel = ii - jj if d == 0 else jj - ii - 1
            ok = rel >= 0
            relf = jnp.where(ok, rel, 0).astype(F32)
            for h in range(N_HEADS):
                lg = lgh_ref[d * N_HEADS + h:d * N_HEADS + h + 1, :]
                dm_scr[d, h] = jnp.where(ok, jnp.exp(lg * relf), 0.0)
            lgl = lgl_ref[d]
            if d == 0:
                qd_scr[d] = jnp.exp(lgl * (t + 1.0))
                kd_scr[d] = jnp.exp(lgl * (cs - 1.0 - t))
            else:
                qd_scr[d] = jnp.exp(lgl * (cs - 1.0 - t))
                kd_scr[d] = jnp.exp(lgl * t)

    lane = lax.broadcasted_iota(jnp.int32, (cs, GW), 1) // HEAD_DIM
    br = lax.broadcasted_iota(jnp.int32, (GW, GW), 0) // HEAD_DIM
    bc = lax.broadcasted_iota(jnp.int32, (GW, GW), 1) // HEAD_DIM

    def direction(d, q_ref, k_ref, v_ref, o_ref):
        s = s_scr[d]
        cdec = jnp.exp(lgl_ref[d] * float(cs))
        for step in range(cps):
            j = step if d == 0 else cps - 1 - step
            rs = slice(j * cs, (j + 1) * cs)
            k = k_ref[rs, :]
            v = v_ref[rs, :]
            q32 = q_ref[rs, :].astype(F32)
            v32 = v.astype(F32)
            parts = []
            vparts = []
            for h in range(N_HEADS):
                mk = lane == h
                qh = jnp.where(mk, q32, 0.0).astype(BF16)
                parts.append((_dot_nt(qh, k) * dm_scr[d, h]).astype(BF16))
                vparts.append(jnp.where(mk, v32, 0.0).astype(BF16))
            p = jnp.concatenate(parts, axis=1)
            vs = jnp.concatenate(vparts, axis=0)
            o = _dot(p, vs) + _dot((q32 * qd_scr[d]).astype(BF16), s.astype(BF16))
            o_ref[rs, :] = o.astype(BF16)
            kd = (k.astype(F32) * kd_scr[d]).T.astype(BF16)
            s = s * cdec + jnp.where(br == bc, _dot(kd, v), 0.0)
        s_scr[d] = s

    direction(0, qf_ref, kf_ref, vf_ref, of_ref)
    direction(1, qb_ref, kb_ref, vb_ref, ob_ref)

    @pl.when(c == nc - 1)
    def _():
        sfin_ref[0] = s_scr[...]


RET_CHUNKS_PER_STEP = 8


def _ret(proj, b, n, lgl, lgh, s0):
    cps = min(RET_CHUNKS_PER_STEP, n // RET_CHUNK)
    tb = cps * RET_CHUNK
    nc = n // tb

    def blk(col, rev):
        return pl.BlockSpec((tb, GW), lambda bi, c: (bi * nc + (nc - 1 - c if rev else c), col))

    return pl.pallas_call(
        functools.partial(_ret_kernel, nc=nc, cps=cps),
        grid=(b, nc),
        in_specs=[blk(C_RET_Q, False), blk(C_RET_K, False), blk(C_RET_V, False),
                  blk(C_RET_Q, True), blk(C_RET_K, True), blk(C_RET_V, True),
                  pl.BlockSpec((2, 1, GW), lambda bi, c: (0, 0, 0)),
                  pl.BlockSpec((2 * N_HEADS, LANES), lambda bi, c: (0, 0)),
                  pl.BlockSpec((1, 2, GW, GW), lambda bi, c: (bi, 0, 0, 0))],
        out_specs=[pl.BlockSpec((tb, GW), lambda bi, c: (bi * nc + c, 0)),
                   pl.BlockSpec((tb, GW), lambda bi, c: (bi * nc + nc - 1 - c, 0)),
                   pl.BlockSpec((1, 2, GW, GW), lambda bi, c: (bi, 0, 0, 0))],
        out_shape=[jax.ShapeDtypeStruct((b * n, GW), BF16), jax.ShapeDtypeStruct((b * n, GW), BF16),
                   jax.ShapeDtypeStruct((b, 2, GW, GW), F32)],
        scratch_shapes=[pltpu.VMEM((2, GW, GW), F32), pltpu.VMEM((2, N_HEADS, RET_CHUNK, RET_CHUNK), F32),
                        pltpu.VMEM((2, RET_CHUNK, GW), F32), pltpu.VMEM((2, RET_CHUNK, GW), F32)],
        compiler_params=_cparams("arbitrary", "arbitrary"),
        name="retention",
    )(proj, proj, proj, proj, proj, proj, lgl, lgh, s0)


def _na_kernel(q_ref, k_ref, v_ref, kc_ref, vc_ref, bias_ref, o_ref, *, rows, rb):
    i0 = pl.program_id(1) * rb
    nwin = NA_WIN_ROWS * GRID_W
    kc = kc_ref[...]
    vc = vc_ref[...]
    lane = lax.broadcasted_iota(jnp.int32, (GRID_W, GW), 1) // HEAD_DIM

    def row(i, carry):
        r = i0 + i
        r0 = jnp.clip(r - NA_WIN_ROWS // 2, 0, rows - NA_WIN_ROWS)
        start = pl.multiple_of(r0 * GRID_W, GRID_W)
        qrow = pl.multiple_of(i * GRID_W, GRID_W)
        kw = k_ref[pl.ds(start, nwin), :]
        vw = v_ref[pl.ds(start, nwin), :]
        q32 = q_ref[pl.ds(qrow, GRID_W), :].astype(F32)
        qs = jnp.concatenate([jnp.where(lane == h, q32, 0.0) for h in range(N_HEADS)], axis=0).astype(BF16)
        sw = _dot_nt(qs, kw) + bias_ref[r0 - r + NA_WIN_ROWS - 1]
        sc = _dot_nt(qs, kc)
        m = jnp.maximum(jnp.max(sw, axis=-1, keepdims=True), jnp.max(sc, axis=-1, keepdims=True))
        ew = jnp.exp(sw - m)
        ec = jnp.exp(sc - m)
        l = jnp.sum(ew, axis=-1, keepdims=True) + jnp.sum(ec, axis=-1, keepdims=True)
        o = (_dot(ew.astype(BF16), vw) + _dot(ec.astype(BF16), vc)) * (1.0 / l)
        out = jnp.zeros((GRID_W, GW), F32)
        for h in range(N_HEADS):
            out = jnp.where(lane == h, o[h * GRID_W:(h + 1) * GRID_W], out)
        o_ref[pl.ds(qrow, GRID_W), :] = out.astype(BF16)
        return carry

    lax.fori_loop(0, rb, row, 0, unroll=NA_UNROLL)


NA_ROWS_PER_STEP = 16
NA_UNROLL = 16


def _na(proj_l, proj_c, bias, b, n, c):
    rows = n // GRID_W
    rb = NA_ROWS_PER_STEP
    nrb = rows // rb
    tq = rb * GRID_W
    return pl.pallas_call(
        functools.partial(_na_kernel, rows=rows, rb=rb),
        grid=(b, nrb),
        in_specs=[pl.BlockSpec((tq, GW), lambda bi, i: (bi * nrb + i, C_NA_Q)),
                  pl.BlockSpec((n, GW), lambda bi, i: (bi, C_NA_K)),
                  pl.BlockSpec((n, GW), lambda bi, i: (bi, C_NA_V)),
                  pl.BlockSpec((c, GW), lambda bi, i: (bi, C_NA_K)),
                  pl.BlockSpec((c, GW), lambda bi, i: (bi, C_NA_V)),
                  pl.BlockSpec((NA_WIN_ROWS, N_HEADS * GRID_W, NA_WIN_ROWS * GRID_W), lambda bi, i: (0, 0, 0))],
        out_specs=pl.BlockSpec((tq, GW), lambda bi, i: (bi * nrb + i, 0)),
        out_shape=jax.ShapeDtypeStruct((b * n, GW), BF16),
        compiler_params=_cparams("arbitrary", "arbitrary"),
        name="na_attention",
    )(proj_l, proj_l, proj_l, proj_c, proj_c, bias)


def _na_bias_tables(rpb):
    nr, ncol = 2 * NA_WIN_ROWS - 1, 2 * NA_WIN_COLS - 1
    cq = np.arange(GRID_W)[:, None]
    ck = np.arange(GRID_W)[None, :]
    cstart = np.clip(cq - NA_WIN_COLS // 2, 0, GRID_W - NA_WIN_COLS)
    okc = (ck >= cstart) & (ck < cstart + NA_WIN_COLS)
    dc = np.clip(ck - cq + NA_WIN_COLS - 1, 0, ncol - 1)
    ohc = (dc[..., None] == np.arange(ncol)) & okc[..., None]
    dr = np.arange(NA_WIN_ROWS)[:, None] + np.arange(NA_WIN_ROWS)[None, :]
    ohr = dr[..., None] == np.arange(nr)
    bias = jnp.einsum('vjr,hrd,ckd->vhcjk', jnp.asarray(ohr, F32), rpb.astype(F32), jnp.asarray(ohc, F32),
                      precision=lax.Precision.HIGHEST)
    bias = jnp.where(jnp.asarray(okc)[None, None, :, None, :], bias, NEG)
    return bias.reshape(NA_WIN_ROWS, N_HEADS * GRID_W, NA_WIN_ROWS * GRID_W)


def _ctx_attn_kernel(*refs, diff, out_scale):
    if diff:
        q_ref, k_ref, v_ref, lam_ref, g_ref, o_ref = refs
    else:
        q_ref, k_ref, v_ref, o_ref = refs
    k = k_ref[...]
    v = v_ref[...]
    q32 = q_ref[...].astype(F32)
    lane = lax.broadcasted_iota(jnp.int32, q32.shape, 1)

    def softmax_pv(mk):
        s = _dot_nt(jnp.where(mk, q32, 0.0).astype(BF16), k)
        z = s - jnp.max(s, axis=-1, keepdims=True)
        e = jnp.exp2(z) if diff else jnp.exp(z)
        return _dot(e.astype(BF16), v) * (1.0 / jnp.sum(e, axis=-1, keepdims=True))

    out = jnp.zeros(q32.shape, F32)
    for h in range(N_HEADS):
        if diff:
            o = (softmax_pv(lane // (HEAD_DIM // 2) == 2 * h)
                 - lam_ref[...] * softmax_pv(lane // (HEAD_DIM // 2) == 2 * h + 1))
        else:
            o = softmax_pv(lane // HEAD_DIM == h)
        out = jnp.where(lane // HEAD_DIM == h, o, out)
    if diff:
        out = out * lax.rsqrt(_head_mean_sq(out) + EPS) * g_ref[...] * out_scale
    o_ref[...] = out.astype(BF16)


def _ctx_attn(proj_c, b, c, qcol, kcol, vcol, lam=None, g=None, out_scale=1.0):
    diff = lam is not None
    in_specs = [pl.BlockSpec((c, GW), lambda bi: (bi, qcol)),
                pl.BlockSpec((c, GW), lambda bi: (bi, kcol)),
                pl.BlockSpec((c, GW), lambda bi: (bi, vcol))]
    args = [proj_c, proj_c, proj_c]
    if diff:
        in_specs += [pl.BlockSpec((1, GW), lambda bi: (0, 0)), pl.BlockSpec((1, GW), lambda bi: (0, 0))]
        args += [lam, g]
    return pl.pallas_call(
        functools.partial(_ctx_attn_kernel, diff=diff, out_scale=out_scale),
        grid=(b,),
        in_specs=in_specs,
        out_specs=pl.BlockSpec((c, GW), lambda bi: (bi, 0)),
        out_shape=jax.ShapeDtypeStruct((b * c, GW), BF16),
        compiler_params=_cparams("arbitrary"),
        name="ctx_diff_attention" if diff else "ctx_attention",
    )(*args)


DIFF_KEY_TILES = (2816, 768, 256)
DIFF_HEADROOM = 64.0
DIFF_MIN_SUM = 2.0 ** -100


def _diff_lat_kernel(q_ref, kc_ref, vtc_ref, kl_ref, vtl_ref, kn_ref, lam_ref, g_ref, o_ref,
                     k_scr, vt_scr, qm_scr, sh_scr, m_scr, acc_scr, *, tq, tk, n_ctx, n_lat, out_scale):
    nt = (n_ctx + n_lat) // tk

    @pl.when(pl.program_id(1) == 0)
    def _():
        k_scr[0:n_ctx, :] = kc_ref[...]
        k_scr[n_ctx:, :] = kl_ref[...]
        vt_scr[:, 0:n_ctx] = vtc_ref[0]
        vt_scr[:, n_ctx:] = vtl_ref[0]

    q32 = q_ref[...].astype(F32)
    qt = q32.T
    rowg = lax.broadcasted_iota(jnp.int32, (GW, tq), 0) // MAP_DIM
    for u in range(N_MAPS):
        qm_scr[u] = jnp.where(rowg == u, qt, 0.0).astype(BF16)
    kmax = jnp.sqrt(_group_sum_sq(jnp.sqrt(kn_ref[0]), MAP_DIM))[0:1, :]
    bound = jnp.sqrt(_group_sum_sq(q32 * kmax, MAP_DIM)) * (1.0 + 2.0 ** -8)
    sh_scr[...] = bound.T - DIFF_HEADROOM
    acc_scr[...] = jnp.zeros(acc_scr.shape, F32)

    def tile(t, u):
        st = pl.multiple_of(t * tk, tk)
        h = u // 2
        return k_scr[pl.ds(st, tk), :], vt_scr[h * V_AUG:(h + 1) * V_AUG, pl.ds(st, tk)]

    def fast(t, carry):
        for u in range(N_MAPS):
            kt, vt = tile(t, u)
            s = _dot(kt, qm_scr[u])
            e = jnp.exp2(s - sh_scr[u * MAP_DIM:u * MAP_DIM + 1, :]).astype(BF16)
            acc_scr[u] = acc_scr[u] + _dot(vt, e)
        return carry

    lax.fori_loop(0, nt, fast, 0)

    lmin = acc_scr[0, HEAD_DIM:HEAD_DIM + 1, :]
    for u in range(1, N_MAPS):
        lmin = jnp.minimum(lmin, acc_scr[u, HEAD_DIM:HEAD_DIM + 1, :])

    @pl.when(jnp.logical_not(jnp.min(lmin) >= DIFF_MIN_SUM))
    def _():
        m_scr[...] = jnp.full(m_scr.shape, NEG, F32)
        acc_scr[...] = jnp.zeros(acc_scr.shape, F32)

        def exact(t, carry):
            for u in range(N_MAPS):
                kt, vt = tile(t, u)
                s = _dot(kt, qm_scr[u])
                mo = m_scr[u:u + 1, :]
                mn = jnp.maximum(mo, jnp.max(s, axis=0, keepdims=True))
                m_scr[u:u + 1, :] = mn
                e = jnp.exp2(s - mn).astype(BF16)
                acc_scr[u] = jnp.exp2(mo - mn) * acc_scr[u] + _dot(vt, e)
            return carry

        lax.fori_loop(0, nt, exact, 0)

    lam = lam_ref[0:1, 0:1]
    outs = []
    for h in range(N_HEADS):
        a1 = acc_scr[2 * h]
        a2 = acc_scr[2 * h + 1]
        o1 = a1[:HEAD_DIM] * (1.0 / a1[HEAD_DIM:HEAD_DIM + 1])
        o2 = a2[:HEAD_DIM] * (1.0 / a2[HEAD_DIM:HEAD_DIM + 1])
        outs.append(o1 - lam * o2)
    out = jnp.concatenate(outs, axis=0).T
    out = out * lax.rsqrt(_head_mean_sq(out) + EPS) * g_ref[...] * out_scale
    o_ref[...] = out.astype(BF16)


def _diff_lat(proj_l, proj_c, vt_l, vt_c, kn2, lam, g, b, n, c, tq, out_scale):
    nt = n // tq
    nk = c + n
    tk = next(t for t in DIFF_KEY_TILES if nk % t == 0)
    once = pl.Buffered(1)
    return pl.pallas_call(
        functools.partial(_diff_lat_kernel, tq=tq, tk=tk, n_ctx=c, n_lat=n, out_scale=out_scale),
        grid=(b, nt),
        in_specs=[pl.BlockSpec((tq, GW), lambda bi, i: (bi * nt + i, C_DF_Q)),
                  pl.BlockSpec((c, GW), lambda bi, i: (bi, C_DF_K)),
                  pl.BlockSpec((1, N_HEADS * V_AUG, c), lambda bi, i: (bi, 0, 0)),
                  pl.BlockSpec((n, GW), lambda bi, i: (bi, C_DF_K), pipeline_mode=once),
                  pl.BlockSpec((1, N_HEADS * V_AUG, n), lambda bi, i: (bi, 0, 0), pipeline_mode=once),
                  pl.BlockSpec((1, SUBLANES, GW), lambda bi, i: (bi, 0, 0)),
                  pl.BlockSpec((1, GW), lambda bi, i: (0, 0)),
                  pl.BlockSpec((1, GW), lambda bi, i: (0, 0))],
        out_specs=pl.BlockSpec((tq, GW), lambda bi, i: (bi * nt + i, 0)),
        out_shape=jax.ShapeDtypeStruct((b * n, GW), BF16),
        scratch_shapes=[pltpu.VMEM((nk, GW), BF16), pltpu.VMEM((N_HEADS * V_AUG, nk), BF16),
                        pltpu.VMEM((N_MAPS, GW, tq), BF16), pltpu.VMEM((GW, tq), F32),
                        pltpu.VMEM((N_MAPS, tq), F32), pltpu.VMEM((N_MAPS, V_AUG, tq), F32)],
        compiler_params=_cparams("arbitrary", "arbitrary"),
        name="diff_attention_lat",
    )(proj_l, proj_c, vt_c, proj_l, vt_l, kn2, lam, g)


def _out_proj_kernel(x_ref, ya_ref, hf_ref, hb_ref, gl_ref, of_ref, ob_ref, gr_ref, yd_ref, w_ref, g_ref, gate_ref,
                     o_ref):
    gl = gl_ref[...].astype(F32)
    gelu = 0.5 * gl * (1.0 + jnp.tanh(math.sqrt(2.0 / math.pi) * (gl + 0.044715 * (gl * gl * gl))))
    yb = (hf_ref[...].astype(F32) + hb_ref[...].astype(F32)) * gelu
    r = of_ref[...].astype(F32) + ob_ref[...].astype(F32)
    gr = gr_ref[...].astype(F32)
    yc = r * lax.rsqrt(_head_mean_sq(r) + EPS) * (gr * _sigmoid(gr))
    y = (_dot(ya_ref[...], w_ref[0:GW, :]) + _dot(yb.astype(BF16), w_ref[GW:2 * GW, :])
         + _dot(yc.astype(BF16), w_ref[2 * GW:3 * GW, :]) + _dot(yd_ref[...], w_ref[3 * GW:4 * GW, :]))
    o_ref[...] = x_ref[...] + gate_ref[0] * _rms(y, g_ref[...])


def _out_proj(x2d, ya, hf, hb, of, ob, yd, proj, w, layer, g, gate, b, n, tm):
    nt = n // tm
    tok = lambda col: pl.BlockSpec((tm, GW), lambda bi, i: (bi * nt + i, col))
    return pl.pallas_call(
        _out_proj_kernel,
        grid=(b, nt),
        in_specs=[pl.BlockSpec((tm, D_MODEL), lambda bi, i: (bi * nt + i, 0)),
                  tok(0), tok(0), tok(0), tok(C_LRU_G), tok(0), tok(0), tok(C_RET_G), tok(0),
                  pl.BlockSpec((None, D_MODEL, D_MODEL), lambda bi, i: (layer, 0, 0)),
                  pl.BlockSpec((1, D_MODEL), lambda bi, i: (0, 0)),
                  pl.BlockSpec((1, 1, D_MODEL), lambda bi, i: (bi, 0, 0))],
        out_specs=pl.BlockSpec((tm, D_MODEL), lambda bi, i: (bi * nt + i, 0)),
        out_shape=jax.ShapeDtypeStruct((b * n, D_MODEL), F32),
        compiler_params=_cparams("arbitrary", "arbitrary"),
        name="out_proj",
    )(x2d, ya, hf, hb, proj, of, ob, proj, yd, w, g, gate)


FFN_CHUNK = 256


def _ffn_kernel(x_ref, g1_ref, sh_ref, sc_ref, w1_ref, w2_ref, g2_ref, gate_ref, o_ref):
    x = x_ref[...]
    h = (_rms(x, g1_ref[...]) * (1.0 + sc_ref[0]) + sh_ref[0]).astype(BF16)
    acc = jnp.zeros(x.shape, F32)
    for c in range(D_FF // FFN_CHUNK):
        lo, hi = c * FFN_CHUNK, (c + 1) * FFN_CHUNK
        gt = _dot(h, w1_ref[:, lo:hi])
        up = _dot(h, w1_ref[:, D_FF + lo:D_FF + hi])
        act = (gt * _sigmoid(gt) * up).astype(BF16)
        acc = acc + _dot(act, w2_ref[lo:hi, :])
    o_ref[...] = x + gate_ref[0] * _rms(acc, g2_ref[...])


def _ffn(x2d, g1, shift, scale, w1, w2, layer, g2, gate, b, n, tm):
    nt = n // tm
    vec = pl.BlockSpec((1, D_MODEL), lambda bi, i: (0, 0))
    per_b = pl.BlockSpec((1, 1, D_MODEL), lambda bi, i: (bi, 0, 0))
    return pl.pallas_call(
        _ffn_kernel,
        grid=(b, nt),
        in_specs=[pl.BlockSpec((tm, D_MODEL), lambda bi, i: (bi * nt + i, 0)), vec, per_b, per_b,
                  pl.BlockSpec((None, D_MODEL, 2 * D_FF), lambda bi, i: (layer, 0, 0)),
                  pl.BlockSpec((None, D_FF, D_MODEL), lambda bi, i: (layer, 0, 0)), vec, per_b],
        out_specs=pl.BlockSpec((tm, D_MODEL), lambda bi, i: (bi * nt + i, 0)),
        out_shape=jax.ShapeDtypeStruct((b * n, D_MODEL), F32),
        compiler_params=_cparams("arbitrary", "arbitrary"),
        name="ffn",
    )(x2d, g1, shift, scale, w1, w2, g2, gate)


def _blockdiag(w):
    nb, bs, _ = w.shape
    return jnp.einsum('kcd,kj->kcjd', w, jnp.eye(nb, dtype=w.dtype)).reshape(nb * bs, nb * bs)


def _lru_params(conv_w, conv_b, gate_w, gate_b, lam):
    wg = jnp.stack([jnp.concatenate([_blockdiag(gate_w[d, 0]), _blockdiag(gate_w[d, 1])], axis=1)
                    for d in range(2)])
    bg = jnp.stack([jnp.concatenate([gate_b[d, 0].reshape(1, GW), gate_b[d, 1].reshape(1, GW)], axis=1)
                    for d in range(2)])
    return conv_w, conv_b.reshape(1, GW), (0.5 * wg).astype(BF16), 0.5 * bg, lam.reshape(2, 1, GW)


def kernel(x, c, ctx, c_ctx, w_mod, b_mod, g_pre_mix, g_post_mix, g_pre_ffn, g_post_ffn, w_in, na_rpb, lru_conv_w,
           lru_conv_b, lru_gate_w, lru_gate_b, lru_lambda, ret_decay, diff_lambda, diff_subln, w_out, w_ffn_in,
           w_ffn_out):
    b, n, _ = x.shape
    nc = ctx.shape[1]
    depth = w_mod.shape[0]
    assert n % TILE_PROJ == 0 and n % TILE_LRU == 0 and nc % RET_CHUNK == 0

    r = -(-(b + 1) // SUBLANES) * SUBLANES
    cc = jnp.zeros((r, D_MODEL), F32).at[:b].set(c).at[b].set(c_ctx)
    mod = _mod(cc, w_mod, b_mod)
    tabs = _rope_tables(n)

    xl = x.reshape(b * n, D_MODEL)
    xc = ctx.reshape(b * nc, D_MODEL)
    tm, tm_proj = TILE_FFN, TILE_PROJ
    vec = lambda a: a.reshape(1, D_MODEL)
    w_in_b, w_out_b, w1_b, w2_b = (w.astype(BF16) for w in (w_in, w_out, w_ffn_in, w_ffn_out))
    for l in range(depth):
        last = l == depth - 1
        lam_init = 0.8 - 0.6 * math.exp(-0.3 * l)
        ml = [mod[l, :b, k * D_MODEL:(k + 1) * D_MODEL].reshape(b, 1, D_MODEL) for k in range(6)]
        mc = [jnp.broadcast_to(mod[l, b, k * D_MODEL:(k + 1) * D_MODEL], (b, 1, D_MODEL)) for k in range(6)]

        pc, vt_c, kn_c = _in_proj(xc, vec(g_pre_mix[l]), mc[0], mc[1], w_in_b, l, None, b, nc, nc)
        pt, vt_l, kn_l = _in_proj(xl, vec(g_pre_mix[l]), ml[0], ml[1], w_in_b, l, tabs, b, n, tm_proj)

        lru_p = _lru_params(lru_conv_w[l], lru_conv_b[l], lru_gate_w[l], lru_gate_b[l], lru_lambda[l])
        hf_c, hb_c, hfin = _lru(pc, b, nc, nc, *lru_p, jnp.zeros((b, SUBLANES, GW), F32))
        hf_l, hb_l, _ = _lru(pt, b, n, TILE_LRU, *lru_p, hfin)

        log_g = jax.nn.log_sigmoid(ret_decay[l].astype(F32))
        lgl = jnp.repeat(log_g, HEAD_DIM, axis=-1).reshape(2, 1, GW)
        lgh = jnp.broadcast_to(log_g.reshape(2 * N_HEADS, 1), (2 * N_HEADS, LANES))
        of_c, ob_c, sfin = _ret(pc, b, nc, lgl, lgh, jnp.zeros((b, 2, GW, GW), F32))
        of_l, ob_l, _ = _ret(pt, b, n, lgl, lgh, sfin)

        ya_l = _na(pt, pc, _na_bias_tables(na_rpb[l]), b, n, nc)

        lq1, lk1, lq2, lk2 = diff_lambda[l].astype(F32)
        lam = jnp.exp(jnp.sum(lq1 * lk1)) - jnp.exp(jnp.sum(lq2 * lk2)) + lam_init
        lam_v = jnp.broadcast_to(lam, (1, GW)).astype(F32)
        g_sub = jnp.tile(diff_subln[l].astype(F32), N_HEADS).reshape(1, GW)
        yd_l = _diff_lat(pt, pc, vt_l, vt_c, jnp.maximum(kn_c, kn_l), lam_v, g_sub, b, n, nc, tm_proj, 1.0 - lam_init)

        x_mid = _out_proj(xl, ya_l, hf_l, hb_l, of_l, ob_l, yd_l, pt, w_out_b, l, vec(g_post_mix[l]), ml[2], b, n,
                          tm_proj)
        xl_new = _ffn(x_mid, vec(g_pre_ffn[l]), ml[3], ml[4], w1_b, w2_b, l, vec(g_post_ffn[l]), ml[5], b, n, tm)

        if not last:
            ya_c = _ctx_attn(pc, b, nc, C_NA_Q, C_NA_K, C_NA_V)
            yd_c = _ctx_attn(pc, b, nc, C_DF_Q, C_DF_K, C_DF_V, lam_v, g_sub, 1.0 - lam_init)
            xc_mid = _out_proj(xc, ya_c, hf_c, hb_c, of_c, ob_c, yd_c, pc, w_out_b, l, vec(g_post_mix[l]), mc[2],
                               b, nc, nc)
            xc = _ffn(xc_mid, vec(g_pre_ffn[l]), mc[3], mc[4], w1_b, w2_b, l, vec(g_post_ffn[l]), mc[5], b, nc, nc)
        xl = xl_new
    return xl.reshape(b, n, D_MODEL)
```

```python
import functools
import math

import numpy as np
import jax
import jax.numpy as jnp
from jax import lax
from jax.experimental import pallas as pl
from jax.experimental.pallas import tpu as pltpu

F32 = jnp.float32
BF16 = jnp.bfloat16

D_MODEL = 1024
GRID_W = 64
HEAD_DIM = 64
N_HEADS = 4
GW = N_HEADS * HEAD_DIM
N_PROJ = 12
N_MAPS = 2 * N_HEADS
MAP_DIM = HEAD_DIM // 2
V_AUG = HEAD_DIM + 16
D_FF = 2816
NA_WIN_ROWS = 8
NA_WIN_COLS = 16
LRU_C = 8.0
RET_CHUNK = 128
ROPE_BASE = 10000.0
EPS = 1e-6
NEG = -1e30

C_NA_Q, C_NA_K, C_NA_V, C_LRU_X, C_LRU_G, C_RET_Q, C_RET_K, C_RET_V, C_RET_G, C_DF_Q, C_DF_K, C_DF_V = range(12)
_COL_SCALE = {C_NA_Q: HEAD_DIM ** -0.5, C_RET_K: HEAD_DIM ** -0.5,
              C_DF_Q: (HEAD_DIM // 2) ** -0.5 * math.log2(math.e)}

LANES = 128
SUBLANES = 8
BF16_ROWS = 2 * SUBLANES
V7X_VMEM_LIMIT = 56 * 1024 * 1024

TILE_PROJ = 1024
TILE_FFN = 512
TILE_LRU = 1024
MOD_COL_TILE = 1536
LRU_HALO = BF16_ROWS


def _cparams(*sem):
    return pltpu.CompilerParams(dimension_semantics=sem, vmem_limit_bytes=V7X_VMEM_LIMIT)


def _dot(a, b):
    return jnp.dot(a, b, preferred_element_type=F32)


def _dot_nt(a, b):
    return lax.dot_general(a, b, (((1,), (1,)), ((), ())), preferred_element_type=F32)


def _rms(x, g):
    return x * lax.rsqrt(jnp.mean(x * x, axis=-1, keepdims=True) + EPS) * g


def _sigmoid(x):
    return 1.0 / (1.0 + jnp.exp(-x))


def _group_sum_sq(y, group):
    y2 = y * y
    hi = y2.astype(BF16)
    lo = (y2 - hi.astype(F32)).astype(BF16)
    r = lax.broadcasted_iota(jnp.int32, (GW, GW), 0) // group
    c = lax.broadcasted_iota(jnp.int32, (GW, GW), 1) // group
    bd = jnp.where(r == c, 1.0, 0.0).astype(BF16)
    return _dot(hi, bd) + _dot(lo, bd)


def _head_mean_sq(y):
    return _group_sum_sq(y, HEAD_DIM) * (1.0 / HEAD_DIM)


def _mod_kernel(c_ref, w_ref, b_ref, o_ref):
    c = c_ref[...]
    s = c * _sigmoid(c)
    o_ref[0] = jnp.dot(s, w_ref[0], preferred_element_type=F32, precision=lax.Precision.HIGHEST) + b_ref[0]


def _mod(cc, w_mod, b_mod):
    depth = w_mod.shape[0]
    r = cc.shape[0]
    tn = MOD_COL_TILE
    return pl.pallas_call(
        _mod_kernel,
        grid=(depth, 6 * D_MODEL // tn),
        in_specs=[pl.BlockSpec((r, D_MODEL), lambda l, j: (0, 0)),
                  pl.BlockSpec((1, D_MODEL, tn), lambda l, j: (l, 0, j)),
                  pl.BlockSpec((1, 1, tn), lambda l, j: (l, 0, j))],
        out_specs=pl.BlockSpec((1, r, tn), lambda l, j: (l, 0, j)),
        out_shape=jax.ShapeDtypeStruct((depth, r, 6 * D_MODEL), F32),
        compiler_params=_cparams("arbitrary", "arbitrary"),
        name="adaln_mod",
    )(cc, w_mod, b_mod.reshape(depth, 1, 6 * D_MODEL))


def _rope(p, cos, sin_signed, half):
    outs = []
    for c in range(GW // LANES):
        xs = p[:, c * LANES:(c + 1) * LANES]
        lane = lax.broadcasted_iota(jnp.int32, xs.shape, 1)
        first = (lane % (2 * half)) < half
        partner = jnp.where(first, pltpu.roll(xs, LANES - half, 1), pltpu.roll(xs, half, 1))
        outs.append(xs * cos + partner * sin_signed)
    return jnp.concatenate(outs, axis=1)


def _in_proj_kernel(*refs, rope):
    if rope:
        x_ref, g_ref, sh_ref, sc_ref, w_ref, tab_ref, o_ref, vt_ref, kn_ref = refs
    else:
        x_ref, g_ref, sh_ref, sc_ref, w_ref, o_ref, vt_ref, kn_ref = refs

    @pl.when(pl.program_id(1) == 0)
    def _():
        kn_ref[...] = jnp.zeros(kn_ref.shape, F32)

    h = _rms(x_ref[...], g_ref[...]) * (1.0 + sc_ref[0]) + sh_ref[0]
    hb = h.astype(BF16)
    for j in range(N_PROJ):
        p = _dot(hb, w_ref[:, j * GW:(j + 1) * GW])
        if j in _COL_SCALE:
            p = p * _COL_SCALE[j]
        if rope and j in (C_RET_Q, C_RET_K):
            p = _rope(p, tab_ref[0], tab_ref[1], HEAD_DIM // 2)
        if rope and j in (C_DF_Q, C_DF_K):
            p = _rope(p, tab_ref[2], tab_ref[3], HEAD_DIM // 4)
        pb = p.astype(BF16)
        o_ref[:, j * GW:(j + 1) * GW] = pb
        if j == C_DF_K:
            kf = pb.astype(F32)
            kn_ref[0] = jnp.maximum(kn_ref[0],
                                    jnp.broadcast_to(jnp.max(kf * kf, axis=0, keepdims=True), (SUBLANES, GW)))
        if j == C_DF_V:
            vt = p.T.astype(BF16)
            ones = jnp.ones((V_AUG - HEAD_DIM, vt.shape[1]), BF16)
            for hd in range(N_HEADS):
                vt_ref[0, hd * V_AUG:hd * V_AUG + HEAD_DIM, :] = vt[hd * HEAD_DIM:(hd + 1) * HEAD_DIM]
                vt_ref[0, hd * V_AUG + HEAD_DIM:(hd + 1) * V_AUG, :] = ones


def _in_proj(x2d, g, shift, scale, w, layer, tabs, b, n, tm):
    nt = n // tm
    rope = tabs is not None
    in_specs = [pl.BlockSpec((tm, D_MODEL), lambda bi, i: (bi * nt + i, 0)),
                pl.BlockSpec((1, D_MODEL), lambda bi, i: (0, 0)),
                pl.BlockSpec((1, 1, D_MODEL), lambda bi, i: (bi, 0, 0)),
                pl.BlockSpec((1, 1, D_MODEL), lambda bi, i: (bi, 0, 0)),
                pl.BlockSpec((None, D_MODEL, N_PROJ * GW), lambda bi, i: (layer, 0, 0))]
    args = [x2d, g, shift, scale, w]
    if rope:
        in_specs.append(pl.BlockSpec((4, tm, LANES), lambda bi, i: (0, i, 0)))
        args.append(tabs)
    return pl.pallas_call(
        functools.partial(_in_proj_kernel, rope=rope),
        grid=(b, nt),
        in_specs=in_specs,
        out_specs=[pl.BlockSpec((tm, N_PROJ * GW), lambda bi, i: (bi * nt + i, 0)),
                   pl.BlockSpec((1, N_HEADS * V_AUG, tm), lambda bi, i: (bi, 0, i)),
                   pl.BlockSpec((1, SUBLANES, GW), lambda bi, i: (bi, 0, 0))],
        out_shape=[jax.ShapeDtypeStruct((b * n, N_PROJ * GW), BF16),
                   jax.ShapeDtypeStruct((b, N_HEADS * V_AUG, n), BF16),
                   jax.ShapeDtypeStruct((b, SUBLANES, GW), F32)],
        compiler_params=_cparams("arbitrary", "arbitrary"),
        name="in_proj_rope" if rope else "in_proj",
    )(*args)


def _rope_tables(n):
    t = jnp.arange(n)
    row = (t // GRID_W).astype(F32)
    col = (t % GRID_W).astype(F32)

    def tab(dim):
        nf = dim // 4
        inv = ROPE_BASE ** (-jnp.arange(nf, dtype=F32) / nf)
        ang = jnp.concatenate([row[:, None] * inv, col[:, None] * inv], axis=-1)
        cos, sin = jnp.cos(ang), jnp.sin(ang)
        reps = LANES // dim
        return (jnp.tile(jnp.concatenate([cos, cos], axis=-1), (1, reps)),
                jnp.tile(jnp.concatenate([-sin, sin], axis=-1), (1, reps)))

    cr, sr = tab(HEAD_DIM)
    cd, sd = tab(HEAD_DIM // 2)
    return jnp.stack([cr, sr, cd, sd])


def _lru_kernel(xf_ref, xfp_ref, xfn_ref, xb_ref, xbp_ref, xbn_ref, cw_ref, cb_ref, wg_ref, bg_ref, lam_ref,
                h0_ref, hf_ref, hb_ref, hfin_ref, a_scr, u_scr, hc_scr, *, tn, nt):
    i = pl.program_id(1)

    @pl.when(i == 0)
    def _():
        hc_scr[...] = h0_ref[0]

    cw = cw_ref[...]
    s8 = SUBLANES
    ng8 = tn // s8
    sub = lax.broadcasted_iota(jnp.int32, (ng8, s8, GW), 1)
    row = lax.broadcasted_iota(jnp.int32, (tn, GW), 0)
    row8 = row % s8

    def shifted(xm, k):
        rot = pltpu.roll(xm.reshape(ng8, s8, GW), (-k) % s8, 1).reshape(tn, GW)
        if k < 0:
            return jnp.where(row8 < -k, pltpu.roll(rot, s8, 0), rot)
        return jnp.where(row8 >= s8 - k, pltpu.roll(rot, tn - s8, 0), rot)

    def coeffs(x_ref, xp_ref, xn_ref, tile, d):
        xm = x_ref[...].astype(F32)
        prev = jnp.where(tile > 0, xp_ref[LRU_HALO - 1:LRU_HALO, :].astype(F32), 0.0)
        nxt = jnp.where(tile < nt - 1, xn_ref[0:2, :].astype(F32), 0.0)
        xm1 = jnp.where(row == 0, prev, shifted(xm, -1))
        xp1 = jnp.where(row == tn - 1, nxt[0:1], shifted(xm, 1))
        xp2 = jnp.where(row == tn - 2, nxt[0:1], shifted(xm, 2))
        xp2 = jnp.where(row == tn - 1, nxt[1:2], xp2)
        xb = cw[0:1] * xm1 + cw[1:2] * xm + cw[2:3] * xp1 + cw[3:4] * xp2 + cb_ref[...]
        tg = jnp.tanh(_dot(xb.astype(BF16), wg_ref[d]) + bg_ref[d]) + 1.0
        nl = -lam_ref[d]
        softplus = jnp.maximum(nl, 0.0) + jnp.log1p(jnp.exp(-jnp.abs(nl)))
        log_a = (-0.5 * LRU_C * softplus) * tg[:, :GW]
        a = jnp.exp(log_a)
        th = jnp.tanh(log_a)
        v = -0.5 * th
        u = v * lax.rsqrt(jnp.maximum(v, 1e-37)) * lax.rsqrt(1.0 - th) * (tg[:, GW:] * xb)

        a = a.reshape(ng8, s8, GW)
        u = u.reshape(ng8, s8, GW)
        for sft in (1, 2, 4):
            keep = sub >= sft if d == 0 else sub < s8 - sft
            rot = sft if d == 0 else s8 - sft
            a_s = jnp.where(keep, pltpu.roll(a, rot, 1), 1.0)
            u_s = jnp.where(keep, pltpu.roll(u, rot, 1), 0.0)
            u = a * u_s + u
            a = a * a_s
        a_scr[d] = a.reshape(tn, GW)
        u_scr[d] = u.reshape(tn, GW)

    coeffs(xf_ref, xfp_ref, xfn_ref, i, 0)
    coeffs(xb_ref, xbp_ref, xbn_ref, nt - 1 - i, 1)

    gs = LRU_HALO
    ng = tn // gs

    def body(g, carry):
        hf, hb = carry
        fb = pl.multiple_of(g * gs, gs)
        bb = pl.multiple_of((ng - 1 - g) * gs, gs)
        f1 = a_scr[0, pl.ds(fb, s8), :] * hf + u_scr[0, pl.ds(fb, s8), :]
        f2 = a_scr[0, pl.ds(fb + s8, s8), :] * f1[s8 - 1:s8] + u_scr[0, pl.ds(fb + s8, s8), :]
        b2 = a_scr[1, pl.ds(bb + s8, s8), :] * hb + u_scr[1, pl.ds(bb + s8, s8), :]
        b1 = a_scr[1, pl.ds(bb, s8), :] * b2[0:1] + u_scr[1, pl.ds(bb, s8), :]
        hf_ref[pl.ds(fb, gs), :] = jnp.concatenate([f1, f2], axis=0).astype(BF16)
        hb_ref[pl.ds(bb, gs), :] = jnp.concatenate([b1, b2], axis=0).astype(BF16)
        return f2[s8 - 1:s8], b1[0:1]

    hf, hb = lax.fori_loop(0, ng, body, (hc_scr[0:1, :], hc_scr[1:2, :]))
    hc_scr[0:1, :] = hf
    hc_scr[1:2, :] = hb

    @pl.when(i == nt - 1)
    def _():
        hfin_ref[0] = jnp.concatenate([hf, hb, jnp.zeros((s8 - 2, GW), F32)], axis=0)


def _lru(proj, b, n, tn, cw, cb, wg, bg, lam, h0):
    nt = n // tn
    hb8 = tn // LRU_HALO

    def main(rev):
        return pl.BlockSpec((tn, GW), lambda bi, i: (bi * nt + (nt - 1 - i if rev else i), C_LRU_X))

    def prev(rev):
        def im(bi, i):
            t = nt - 1 - i if rev else i
            return (jnp.maximum((bi * nt + t) * hb8 - 1, 0), C_LRU_X)
        return pl.BlockSpec((LRU_HALO, GW), im)

    def nxt(rev):
        def im(bi, i):
            t = nt - 1 - i if rev else i
            return (jnp.minimum((bi * nt + t + 1) * hb8, b * nt * hb8 - 1), C_LRU_X)
        return pl.BlockSpec((LRU_HALO, GW), im)

    const2 = lambda bi, i: (0, 0)
    const3 = lambda bi, i: (0, 0, 0)
    return pl.pallas_call(
        functools.partial(_lru_kernel, tn=tn, nt=nt),
        grid=(b, nt),
        in_specs=[main(False), prev(False), nxt(False), main(True), prev(True), nxt(True),
                  pl.BlockSpec((4, GW), const2), pl.BlockSpec((1, GW), const2),
                  pl.BlockSpec((2, GW, 2 * GW), const3), pl.BlockSpec((2, 1, 2 * GW), const3),
                  pl.BlockSpec((2, 1, GW), const3),
                  pl.BlockSpec((1, SUBLANES, GW), lambda bi, i: (bi, 0, 0))],
        out_specs=[pl.BlockSpec((tn, GW), lambda bi, i: (bi * nt + i, 0)),
                   pl.BlockSpec((tn, GW), lambda bi, i: (bi * nt + nt - 1 - i, 0)),
                   pl.BlockSpec((1, SUBLANES, GW), lambda bi, i: (bi, 0, 0))],
        out_shape=[jax.ShapeDtypeStruct((b * n, GW), BF16), jax.ShapeDtypeStruct((b * n, GW), BF16),
                   jax.ShapeDtypeStruct((b, SUBLANES, GW), F32)],
        scratch_shapes=[pltpu.VMEM((2, tn, GW), F32), pltpu.VMEM((2, tn, GW), F32),
                        pltpu.VMEM((SUBLANES, GW), F32)],
        compiler_params=_cparams("arbitrary", "arbitrary"),
        name="rglru",
    )(proj, proj, proj, proj, proj, proj, cw, cb, wg, bg, lam, h0)


def _ret_kernel(qf_ref, kf_ref, vf_ref, qb_ref, kb_ref, vb_ref, lgl_ref, lgh_ref, s0_ref,
                of_ref, ob_ref, sfin_ref, s_scr, dm_scr, qd_scr, kd_scr, *, nc, cps):
    c = pl.program_id(1)
    cs = RET_CHUNK

    @pl.when(c == 0)
    def _():
        s_scr[...] = s0_ref[0]
        ii = lax.broadcasted_iota(jnp.int32, (cs, cs), 0)
        jj = lax.broadcasted_iota(jnp.int32, (cs, cs), 1)
        t = lax.broadcasted_iota(jnp.int32, (cs, GW), 0).astype(F32)
        for d in range(2):
            rel = ii - jj if d == 0 else jj - ii - 1
            ok = rel >= 0
            relf = jnp.where(ok, rel, 0).astype(F32)
            for h in range(N_HEADS):
                lg = lgh_ref[d * N_HEADS + h:d * N_HEADS + h + 1, :]
                dm_scr[d, h] = jnp.where(ok, jnp.exp(lg * relf), 0.0)
            lgl = lgl_ref[d]
            if d == 0:
                qd_scr[d] = jnp.exp(lgl * (t + 1.0))
                kd_scr[d] = jnp.exp(lgl * (cs - 1.0 - t))
            else:
                qd_scr[d] = jnp.exp(lgl * (cs - 1.0 - t))
                kd_scr[d] = jnp.exp(lgl * t)

    lane = lax.broadcasted_iota(jnp.int32, (cs, GW), 1) // HEAD_DIM
    br = lax.broadcasted_iota(jnp.int32, (GW, GW), 0) // HEAD_DIM
    bc = lax.broadcasted_iota(jnp.int32, (GW, GW), 1) // HEAD_DIM

    def direction(d, q_ref, k_ref, v_ref, o_ref):
        s = s_scr[d]
        cdec = jnp.exp(lgl_ref[d] * float(cs))
        for step in range(cps):
            j = step if d == 0 else cps - 1 - step
            rs = slice(j * cs, (j + 1) * cs)
            k = k_ref[rs, :]
            v = v_ref[rs, :]
            q32 = q_ref[rs, :].astype(F32)
            v32 = v.astype(F32)
            parts = []
            vparts = []
            for h in range(N_HEADS):
                mk = lane == h
                qh = jnp.where(mk, q32, 0.0).astype(BF16)
                parts.append((_dot_nt(qh, k) * dm_scr[d, h]).astype(BF16))
                vparts.append(jnp.where(mk, v32, 0.0).astype(BF16))
            p = jnp.concatenate(parts, axis=1)
            vs = jnp.concatenate(vparts, axis=0)
            o = _dot(p, vs) + _dot((q32 * qd_scr[d]).astype(BF16), s.astype(BF16))
            o_ref[rs, :] = o.astype(BF16)
            kd = (k.astype(F32) * kd_scr[d]).T.astype(BF16)
            s = s * cdec + jnp.where(br == bc, _dot(kd, v), 0.0)
        s_scr[d] = s

    direction(0, qf_ref, kf_ref, vf_ref, of_ref)
    direction(1, qb_ref, kb_ref, vb_ref, ob_ref)

    @pl.when(c == nc - 1)
    def _():
        sfin_ref[0] = s_scr[...]


RET_CHUNKS_PER_STEP = 8


def _ret(proj, b, n, lgl, lgh, s0):
    cps = min(RET_CHUNKS_PER_STEP, n // RET_CHUNK)
    tb = cps * RET_CHUNK
    nc = n // tb

    def blk(col, rev):
        return pl.BlockSpec((tb, GW), lambda bi, c: (bi * nc + (nc - 1 - c if rev else c), col))

    return pl.pallas_call(
        functools.partial(_ret_kernel, nc=nc, cps=cps),
        grid=(b, nc),
        in_specs=[blk(C_RET_Q, False), blk(C_RET_K, False), blk(C_RET_V, False),
                  blk(C_RET_Q, True), blk(C_RET_K, True), blk(C_RET_V, True),
                  pl.BlockSpec((2, 1, GW), lambda bi, c: (0, 0, 0)),
                  pl.BlockSpec((2 * N_HEADS, LANES), lambda bi, c: (0, 0)),
                  pl.BlockSpec((1, 2, GW, GW), lambda bi, c: (bi, 0, 0, 0))],
        out_specs=[pl.BlockSpec((tb, GW), lambda bi, c: (bi * nc + c, 0)),
                   pl.BlockSpec((tb, GW), lambda bi, c: (bi * nc + nc - 1 - c, 0)),
                   pl.BlockSpec((1, 2, GW, GW), lambda bi, c: (bi, 0, 0, 0))],
        out_shape=[jax.ShapeDtypeStruct((b * n, GW), BF16), jax.ShapeDtypeStruct((b * n, GW), BF16),
                   jax.ShapeDtypeStruct((b, 2, GW, GW), F32)],
        scratch_shapes=[pltpu.VMEM((2, GW, GW), F32), pltpu.VMEM((2, N_HEADS, RET_CHUNK, RET_CHUNK), F32),
                        pltpu.VMEM((2, RET_CHUNK, GW), F32), pltpu.VMEM((2, RET_CHUNK, GW), F32)],
        compiler_params=_cparams("arbitrary", "arbitrary"),
        name="retention",
    )(proj, proj, proj, proj, proj, proj, lgl, lgh, s0)


def _na_kernel(q_ref, k_ref, v_ref, kc_ref, vc_ref, bias_ref, o_ref, *, rows, rb):
    i0 = pl.program_id(1) * rb
    nwin = NA_WIN_ROWS * GRID_W
    kc = kc_ref[...]
    vc = vc_ref[...]
    lane = lax.broadcasted_iota(jnp.int32, (GRID_W, GW), 1) // HEAD_DIM

    def row(i, carry):
        r = i0 + i
        r0 = jnp.clip(r - NA_WIN_ROWS // 2, 0, rows - NA_WIN_ROWS)
        start = pl.multiple_of(r0 * GRID_W, GRID_W)
        qrow = pl.multiple_of(i * GRID_W, GRID_W)
        kw = k_ref[pl.ds(start, nwin), :]
        vw = v_ref[pl.ds(start, nwin), :]
        q32 = q_ref[pl.ds(qrow, GRID_W), :].astype(F32)
        qs = jnp.concatenate([jnp.where(lane == h, q32, 0.0) for h in range(N_HEADS)], axis=0).astype(BF16)
        sw = _dot_nt(qs, kw) + bias_ref[r0 - r + NA_WIN_ROWS - 1]
        sc = _dot_nt(qs, kc)
        m = jnp.maximum(jnp.max(sw, axis=-1, keepdims=True), jnp.max(sc, axis=-1, keepdims=True))
        ew = jnp.exp(sw - m)
        ec = jnp.exp(sc - m)
        l = jnp.sum(ew, axis=-1, keepdims=True) + jnp.sum(ec, axis=-1, keepdims=True)
        o = (_dot(ew.astype(BF16), vw) + _dot(ec.astype(BF16), vc)) * (1.0 / l)
        out = jnp.zeros((GRID_W, GW), F32)
        for h in range(N_HEADS):
            out = jnp.where(lane == h, o[h * GRID_W:(h + 1) * GRID_W], out)
        o_ref[pl.ds(qrow, GRID_W), :] = out.astype(BF16)
        return carry

    lax.fori_loop(0, rb, row, 0, unroll=NA_UNROLL)


NA_ROWS_PER_STEP = 16
NA_UNROLL = 16


def _na(proj_l, proj_c, bias, b, n, c):
    rows = n // GRID_W
    rb = NA_ROWS_PER_STEP
    nrb = rows // rb
    tq = rb * GRID_W
    return pl.pallas_call(
        functools.partial(_na_kernel, rows=rows, rb=rb),
        grid=(b, nrb),
        in_specs=[pl.BlockSpec((tq, GW), lambda bi, i: (bi * nrb + i, C_NA_Q)),
                  pl.BlockSpec((n, GW), lambda bi, i: (bi, C_NA_K)),
                  pl.BlockSpec((n, GW), lambda bi, i: (bi, C_NA_V)),
                  pl.BlockSpec((c, GW), lambda bi, i: (bi, C_NA_K)),
                  pl.BlockSpec((c, GW), lambda bi, i: (bi, C_NA_V)),
                  pl.BlockSpec((NA_WIN_ROWS, N_HEADS * GRID_W, NA_WIN_ROWS * GRID_W), lambda bi, i: (0, 0, 0))],
        out_specs=pl.BlockSpec((tq, GW), lambda bi, i: (bi * nrb + i, 0)),
        out_shape=jax.ShapeDtypeStruct((b * n, GW), BF16),
        compiler_params=_cparams("arbitrary", "arbitrary"),
        name="na_attention",
    )(proj_l, proj_l, proj_l, proj_c, proj_c, bias)


def _na_bias_tables(rpb):
    nr, ncol = 2 * NA_WIN_ROWS - 1, 2 * NA_WIN_COLS - 1
    cq = np.arange(GRID_W)[:, None]
    ck = np.arange(GRID_W)[None, :]
    cstart = np.clip(cq - NA_WIN_COLS // 2, 0, GRID_W - NA_WIN_COLS)
    okc = (ck >= cstart) & (ck < cstart + NA_WIN_COLS)
    dc = np.clip(ck - cq + NA_WIN_COLS - 1, 0, ncol - 1)
    ohc = (dc[..., None] == np.arange(ncol)) & okc[..., None]
    dr = np.arange(NA_WIN_ROWS)[:, None] + np.arange(NA_WIN_ROWS)[None, :]
    ohr = dr[..., None] == np.arange(nr)
    bias = jnp.einsum('vjr,hrd,ckd->vhcjk', jnp.asarray(ohr, F32), rpb.astype(F32), jnp.asarray(ohc, F32),
                      precision=lax.Precision.HIGHEST)
    bias = jnp.where(jnp.asarray(okc)[None, None, :, None, :], bias, NEG)
    return bias.reshape(NA_WIN_ROWS, N_HEADS * GRID_W, NA_WIN_ROWS * GRID_W)


def _ctx_attn_kernel(*refs, diff, out_scale):
    if diff:
        q_ref, k_ref, v_ref, lam_ref, g_ref, o_ref = refs
    else:
        q_ref, k_ref, v_ref, o_ref = refs
    k = k_ref[...]
    v = v_ref[...]
    q32 = q_ref[...].astype(F32)
    lane = lax.broadcasted_iota(jnp.int32, q32.shape, 1)

    def softmax_pv(mk):
        s = _dot_nt(jnp.where(mk, q32, 0.0).astype(BF16), k)
        z = s - jnp.max(s, axis=-1, keepdims=True)
        e = jnp.exp2(z) if diff else jnp.exp(z)
        return _dot(e.astype(BF16), v) * (1.0 / jnp.sum(e, axis=-1, keepdims=True))

    out = jnp.zeros(q32.shape, F32)
    for h in range(N_HEADS):
        if diff:
            o = (softmax_pv(lane // (HEAD_DIM // 2) == 2 * h)
                 - lam_ref[...] * softmax_pv(lane // (HEAD_DIM // 2) == 2 * h + 1))
        else:
            o = softmax_pv(lane // HEAD_DIM == h)
        out = jnp.where(lane // HEAD_DIM == h, o, out)
    if diff:
        out = out * lax.rsqrt(_head_mean_sq(out) + EPS) * g_ref[...] * out_scale
    o_ref[...] = out.astype(BF16)


def _ctx_attn(proj_c, b, c, qcol, kcol, vcol, lam=None, g=None, out_scale=1.0):
    diff = lam is not None
    in_specs = [pl.BlockSpec((c, GW), lambda bi: (bi, qcol)),
                pl.BlockSpec((c, GW), lambda bi: (bi, kcol)),
                pl.BlockSpec((c, GW), lambda bi: (bi, vcol))]
    args = [proj_c, proj_c, proj_c]
    if diff:
        in_specs += [pl.BlockSpec((1, GW), lambda bi: (0, 0)), pl.BlockSpec((1, GW), lambda bi: (0, 0))]
        args += [lam, g]
    return pl.pallas_call(
        functools.partial(_ctx_attn_kernel, diff=diff, out_scale=out_scale),
        grid=(b,),
        in_specs=in_specs,
        out_specs=pl.BlockSpec((c, GW), lambda bi: (bi, 0)),
        out_shape=jax.ShapeDtypeStruct((b * c, GW), BF16),
        compiler_params=_cparams("arbitrary"),
        name="ctx_diff_attention" if diff else "ctx_attention",
    )(*args)


DIFF_KEY_TILES = (2816, 768, 256)
DIFF_HEADROOM = 64.0
DIFF_MIN_SUM = 2.0 ** -100


def _diff_lat_kernel(q_ref, kc_ref, vtc_ref, kl_ref, vtl_ref, kn_ref, lam_ref, g_ref, o_ref,
                     k_scr, vt_scr, qm_scr, sh_scr, m_scr, acc_scr, *, tq, tk, n_ctx, n_lat, out_scale):
    nt = (n_ctx + n_lat) // tk

    @pl.when(pl.program_id(1) == 0)
    def _():
        k_scr[0:n_ctx, :] = kc_ref[...]
        k_scr[n_ctx:, :] = kl_ref[...]
        vt_scr[:, 0:n_ctx] = vtc_ref[0]
        vt_scr[:, n_ctx:] = vtl_ref[0]

    q32 = q_ref[...].astype(F32)
    qt = q32.T
    qm_scr[...] = qt.astype(BF16)
    kmax = jnp.sqrt(_group_sum_sq(jnp.sqrt(kn_ref[0]), MAP_DIM))[0:1, :]
    bound = jnp.sqrt(_group_sum_sq(q32 * kmax, MAP_DIM)) * (1.0 + 2.0 ** -8)
    sh_scr[...] = bound.T - DIFF_HEADROOM
    acc_scr[...] = jnp.zeros(acc_scr.shape, F32)

    def tile(t, u):
        st = pl.multiple_of(t * tk, tk)
        h = u // 2
        return (k_scr[pl.ds(st, tk), u * MAP_DIM:(u + 1) * MAP_DIM],
                vt_scr[h * V_AUG:(h + 1) * V_AUG, pl.ds(st, tk)])

    def fast(t, carry):
        for u in range(N_MAPS):
            kt, vt = tile(t, u)
            s = _dot(kt, qm_scr[u * MAP_DIM:(u + 1) * MAP_DIM, :])
            e = jnp.exp2(s - sh_scr[u * MAP_DIM:u * MAP_DIM + 1, :]).astype(BF16)
            acc_scr[u] = acc_scr[u] + _dot(vt, e)
        return carry

    lax.fori_loop(0, nt, fast, 0)

    lmin = acc_scr[0, HEAD_DIM:HEAD_DIM + 1, :]
    for u in range(1, N_MAPS):
        lmin = jnp.minimum(lmin, acc_scr[u, HEAD_DIM:HEAD_DIM + 1, :])

    @pl.when(jnp.logical_not(jnp.min(lmin) >= DIFF_MIN_SUM))
    def _():
        m_scr[...] = jnp.full(m_scr.shape, NEG, F32)
        acc_scr[...] = jnp.zeros(acc_scr.shape, F32)

        def exact(t, carry):
            for u in range(N_MAPS):
                kt, vt = tile(t, u)
                s = _dot(kt, qm_scr[u * MAP_DIM:(u + 1) * MAP_DIM, :])
                mo = m_scr[u:u + 1, :]
                mn = jnp.maximum(mo, jnp.max(s, axis=0, keepdims=True))
                m_scr[u:u + 1, :] = mn
                e = jnp.exp2(s - mn).astype(BF16)
                acc_scr[u] = jnp.exp2(mo - mn) * acc_scr[u] + _dot(vt, e)
            return carry

        lax.fori_loop(0, nt, exact, 0)

    lam = lam_ref[0:1, 0:1]
    outs = []
    for h in range(N_HEADS):
        a1 = acc_scr[2 * h]
        a2 = acc_scr[2 * h + 1]
        o1 = a1[:HEAD_DIM] * (1.0 / a1[HEAD_DIM:HEAD_DIM + 1])
        o2 = a2[:HEAD_DIM] * (1.0 / a2[HEAD_DIM:HEAD_DIM + 1])
        outs.append(o1 - lam * o2)
    out = jnp.concatenate(outs, axis=0).T
    out = out * lax.rsqrt(_head_mean_sq(out) + EPS) * g_ref[...] * out_scale
    o_ref[...] = out.astype(BF16)


def _diff_lat(proj_l, proj_c, vt_l, vt_c, kn2, lam, g, b, n, c, tq, out_scale):
    nt = n // tq
    nk = c + n
    tk = next(t for t in DIFF_KEY_TILES if nk % t == 0)
    once = pl.Buffered(1)
    return pl.pallas_call(
        functools.partial(_diff_lat_kernel, tq=tq, tk=tk, n_ctx=c, n_lat=n, out_scale=out_scale),
        grid=(b, nt),
        in_specs=[pl.BlockSpec((tq, GW), lambda bi, i: (bi * nt + i, C_DF_Q)),
                  pl.BlockSpec((c, GW), lambda bi, i: (bi, C_DF_K)),
                  pl.BlockSpec((1, N_HEADS * V_AUG, c), lambda bi, i: (bi, 0, 0)),
                  pl.BlockSpec((n, GW), lambda bi, i: (bi, C_DF_K), pipeline_mode=once),
                  pl.BlockSpec((1, N_HEADS * V_AUG, n), lambda bi, i: (bi, 0, 0), pipeline_mode=once),
                  pl.BlockSpec((1, SUBLANES, GW), lambda bi, i: (bi, 0, 0)),
                  pl.BlockSpec((1, GW), lambda bi, i: (0, 0)),
                  pl.BlockSpec((1, GW), lambda bi, i: (0, 0))],
        out_specs=pl.BlockSpec((tq, GW), lambda bi, i: (bi * nt + i, 0)),
        out_shape=jax.ShapeDtypeStruct((b * n, GW), BF16),
        scratch_shapes=[pltpu.VMEM((nk, GW), BF16), pltpu.VMEM((N_HEADS * V_AUG, nk), BF16),
                        pltpu.VMEM((GW, tq), BF16), pltpu.VMEM((GW, tq), F32),
                        pltpu.VMEM((N_MAPS, tq), F32), pltpu.VMEM((N_MAPS, V_AUG, tq), F32)],
        compiler_params=_cparams("arbitrary", "arbitrary"),
        name="diff_attention_lat",
    )(proj_l, proj_c, vt_c, proj_l, vt_l, kn2, lam, g)


def _out_proj_kernel(x_ref, ya_ref, hf_ref, hb_ref, gl_ref, of_ref, ob_ref, gr_ref, yd_ref, w_ref, g_ref, gate_ref,
                     o_ref):
    gl = gl_ref[...].astype(F32)
    gelu = 0.5 * gl * (1.0 + jnp.tanh(math.sqrt(2.0 / math.pi) * (gl + 0.044715 * (gl * gl * gl))))
    yb = (hf_ref[...].astype(F32) + hb_ref[...].astype(F32)) * gelu
    r = of_ref[...].astype(F32) + ob_ref[...].astype(F32)
    gr = gr_ref[...].astype(F32)
    yc = r * lax.rsqrt(_head_mean_sq(r) + EPS) * (gr * _sigmoid(gr))
    y = (_dot(ya_ref[...], w_ref[0:GW, :]) + _dot(yb.astype(BF16), w_ref[GW:2 * GW, :])
         + _dot(yc.astype(BF16), w_ref[2 * GW:3 * GW, :]) + _dot(yd_ref[...], w_ref[3 * GW:4 * GW, :]))
    o_ref[...] = x_ref[...] + gate_ref[0] * _rms(y, g_ref[...])


def _out_proj(x2d, ya, hf, hb, of, ob, yd, proj, w, layer, g, gate, b, n, tm):
    nt = n // tm
    tok = lambda col: pl.BlockSpec((tm, GW), lambda bi, i: (bi * nt + i, col))
    return pl.pallas_call(
        _out_proj_kernel,
        grid=(b, nt),
        in_specs=[pl.BlockSpec((tm, D_MODEL), lambda bi, i: (bi * nt + i, 0)),
                  tok(0), tok(0), tok(0), tok(C_LRU_G), tok(0), tok(0), tok(C_RET_G), tok(0),
                  pl.BlockSpec((None, D_MODEL, D_MODEL), lambda bi, i: (layer, 0, 0)),
                  pl.BlockSpec((1, D_MODEL), lambda bi, i: (0, 0)),
                  pl.BlockSpec((1, 1, D_MODEL), lambda bi, i: (bi, 0, 0))],
        out_specs=pl.BlockSpec((tm, D_MODEL), lambda bi, i: (bi * nt + i, 0)),
        out_shape=jax.ShapeDtypeStruct((b * n, D_MODEL), F32),
        compiler_params=_cparams("arbitrary", "arbitrary"),
        name="out_proj",
    )(x2d, ya, hf, hb, proj, of, ob, proj, yd, w, g, gate)


FFN_CHUNK = 256


def _ffn_kernel(x_ref, g1_ref, sh_ref, sc_ref, w1_ref, w2_ref, g2_ref, gate_ref, o_ref):
    x = x_ref[...]
    h = (_rms(x, g1_ref[...]) * (1.0 + sc_ref[0]) + sh_ref[0]).astype(BF16)
    acc = jnp.zeros(x.shape, F32)
    for c in range(D_FF // FFN_CHUNK):
        lo, hi = c * FFN_CHUNK, (c + 1) * FFN_CHUNK
        gt = _dot(h, w1_ref[:, lo:hi])
        up = _dot(h, w1_ref[:, D_FF + lo:D_FF + hi])
        act = (gt * _sigmoid(gt) * up).astype(BF16)
        acc = acc + _dot(act, w2_ref[lo:hi, :])
    o_ref[...] = x + gate_ref[0] * _rms(acc, g2_ref[...])


def _ffn(x2d, g1, shift, scale, w1, w2, layer, g2, gate, b, n, tm):
    nt = n // tm
    vec = pl.BlockSpec((1, D_MODEL), lambda bi, i: (0, 0))
    per_b = pl.BlockSpec((1, 1, D_MODEL), lambda bi, i: (bi, 0, 0))
    return pl.pallas_call(
        _ffn_kernel,
        grid=(b, nt),
        in_specs=[pl.BlockSpec((tm, D_MODEL), lambda bi, i: (bi * nt + i, 0)), vec, per_b, per_b,
                  pl.BlockSpec((None, D_MODEL, 2 * D_FF), lambda bi, i: (layer, 0, 0)),
                  pl.BlockSpec((None, D_FF, D_MODEL), lambda bi, i: (layer, 0, 0)), vec, per_b],
        out_specs=pl.BlockSpec((tm, D_MODEL), lambda bi, i: (bi * nt + i, 0)),
        out_shape=jax.ShapeDtypeStruct((b * n, D_MODEL), F32),
        compiler_params=_cparams("arbitrary", "arbitrary"),
        name="ffn",
    )(x2d, g1, shift, scale, w1, w2, g2, gate)


def _blockdiag(w):
    nb, bs, _ = w.shape
    return jnp.einsum('kcd,kj->kcjd', w, jnp.eye(nb, dtype=w.dtype)).reshape(nb * bs, nb * bs)


def _lru_params(conv_w, conv_b, gate_w, gate_b, lam):
    wg = jnp.stack([jnp.concatenate([_blockdiag(gate_w[d, 0]), _blockdiag(gate_w[d, 1])], axis=1)
                    for d in range(2)])
    bg = jnp.stack([jnp.concatenate([gate_b[d, 0].reshape(1, GW), gate_b[d, 1].reshape(1, GW)], axis=1)
                    for d in range(2)])
    return conv_w, conv_b.reshape(1, GW), (0.5 * wg).astype(BF16), 0.5 * bg, lam.reshape(2, 1, GW)


def kernel(x, c, ctx, c_ctx, w_mod, b_mod, g_pre_mix, g_post_mix, g_pre_ffn, g_post_ffn, w_in, na_rpb, lru_conv_w,
           lru_conv_b, lru_gate_w, lru_gate_b, lru_lambda, ret_decay, diff_lambda, diff_subln, w_out, w_ffn_in,
           w_ffn_out):
    b, n, _ = x.shape
    nc = ctx.shape[1]
    depth = w_mod.shape[0]
    assert n % TILE_PROJ == 0 and n % TILE_LRU == 0 and nc % RET_CHUNK == 0

    r = -(-(b + 1) // SUBLANES) * SUBLANES
    cc = jnp.zeros((r, D_MODEL), F32).at[:b].set(c).at[b].set(c_ctx)
    mod = _mod(cc, w_mod, b_mod)
    tabs = _rope_tables(n)

    xl = x.reshape(b * n, D_MODEL)
    xc = ctx.reshape(b * nc, D_MODEL)
    tm, tm_proj = TILE_FFN, TILE_PROJ
    vec = lambda a: a.reshape(1, D_MODEL)
    w_in_b, w_out_b, w1_b, w2_b = (w.astype(BF16) for w in (w_in, w_out, w_ffn_in, w_ffn_out))
    for l in range(depth):
        last = l == depth - 1
        lam_init = 0.8 - 0.6 * math.exp(-0.3 * l)
        ml = [mod[l, :b, k * D_MODEL:(k + 1) * D_MODEL].reshape(b, 1, D_MODEL) for k in range(6)]
        mc = [jnp.broadcast_to(mod[l, b, k * D_MODEL:(k + 1) * D_MODEL], (b, 1, D_MODEL)) for k in range(6)]

        pc, vt_c, kn_c = _in_proj(xc, vec(g_pre_mix[l]), mc[0], mc[1], w_in_b, l, None, b, nc, nc)
        pt, vt_l, kn_l = _in_proj(xl, vec(g_pre_mix[l]), ml[0], ml[1], w_in_b, l, tabs, b, n, tm_proj)

        lru_p = _lru_params(lru_conv_w[l], lru_conv_b[l], lru_gate_w[l], lru_gate_b[l], lru_lambda[l])
        hf_c, hb_c, hfin = _lru(pc, b, nc, nc, *lru_p, jnp.zeros((b, SUBLANES, GW), F32))
        hf_l, hb_l, _ = _lru(pt, b, n, TILE_LRU, *lru_p, hfin)

        log_g = jax.nn.log_sigmoid(ret_decay[l].astype(F32))
        lgl = jnp.repeat(log_g, HEAD_DIM, axis=-1).reshape(2, 1, GW)
        lgh = jnp.broadcast_to(log_g.reshape(2 * N_HEADS, 1), (2 * N_HEADS, LANES))
        of_c, ob_c, sfin = _ret(pc, b, nc, lgl, lgh, jnp.zeros((b, 2, GW, GW), F32))
        of_l, ob_l, _ = _ret(pt, b, n, lgl, lgh, sfin)

        ya_l = _na(pt, pc, _na_bias_tables(na_rpb[l]), b, n, nc)

        lq1, lk1, lq2, lk2 = diff_lambda[l].astype(F32)
        lam = jnp.exp(jnp.sum(lq1 * lk1)) - jnp.exp(jnp.sum(lq2 * lk2)) + lam_init
        lam_v = jnp.broadcast_to(lam, (1, GW)).astype(F32)
        g_sub = jnp.tile(diff_subln[l].astype(F32), N_HEADS).reshape(1, GW)
        yd_l = _diff_lat(pt, pc, vt_l, vt_c, jnp.maximum(kn_c, kn_l), lam_v, g_sub, b, n, nc, tm_proj, 1.0 - lam_init)

        x_mid = _out_proj(xl, ya_l, hf_l, hb_l, of_l, ob_l, yd_l, pt, w_out_b, l, vec(g_post_mix[l]), ml[2], b, n,
                          tm_proj)
        xl_new = _ffn(x_mid, vec(g_pre_ffn[l]), ml[3], ml[4], w1_b, w2_b, l, vec(g_post_ffn[l]), ml[5], b, n, tm)

        if not last:
            ya_c = _ctx_attn(pc, b, nc, C_NA_Q, C_NA_K, C_NA_V)
            yd_c = _ctx_attn(pc, b, nc, C_DF_Q, C_DF_K, C_DF_V, lam_v, g_sub, 1.0 - lam_init)
            xc_mid = _out_proj(xc, ya_c, hf_c, hb_c, of_c, ob_c, yd_c, pc, w_out_b, l, vec(g_post_mix[l]), mc[2],
                               b, nc, nc)
            xc = _ffn(xc_mid, vec(g_pre_ffn[l]), mc[3], mc[4], w1_b, w2_b, l, vec(g_post_ffn[l]), mc[5], b, nc, nc)
        xl = xl_new
    return xl.reshape(b, n, D_MODEL)
```
